```python
import math
import jax
import jax.numpy as jnp
from jax import lax
import numpy as np

D_MODEL = 1024
BATCH = 2
SEQ = 8192
DEPTH = 1
DEC_BATCH = 128
DEC_SEQ = 8
PAST_LEN = 8192
PAGE_SIZE = 128

MIX_W = D_MODEL
ATTN_W = MIX_W // 2
POOL_W = MIX_W - ATTN_W
HEAD_DIM = 64
N_HEADS = ATTN_W // HEAD_DIM
N_KV_HEADS = 2
HEADS_PER_KV = N_HEADS // N_KV_HEADS
SCALE = HEAD_DIM ** -0.5
ROT_DIM = HEAD_DIM // 4
ROPE_THETA = 500000.0
CMP_LEN = 32
CMP_STRIDE = 16
CMP_HID = HEAD_DIM
SEL_BLOCK = 64
SEL_TOPN = 16
WINDOW = 512
Q_BLOCK = 128
FORCE_BONUS = 1000.0
POOL_WINDOWS = (2, 4, 8, 16)
N_POOL_GROUPS = len(POOL_WINDOWS)
POOL_CH = POOL_W // N_POOL_GROUPS
POOL_HIST = max(POOL_WINDOWS) - 1
N_MEM = 256
MEM_HEADS = 4
MEM_HEAD_DIM = 128
MEM_W = MEM_HEADS * MEM_HEAD_DIM
D_FF = 4 * D_MODEL
EPS = 1e-6
KV_W = 2 * N_KV_HEADS * HEAD_DIM
IN_SPLITS = (ATTN_W, ATTN_W + KV_W, ATTN_W + 2 * KV_W, ATTN_W + 3 * KV_W, ATTN_W + 3 * KV_W + 3 * N_HEADS)
N_IN = ATTN_W + 3 * KV_W + 3 * N_HEADS + POOL_W

kernel_name = 'nsa_pool_hybrid_decode_step'


def rms_norm(x, g):
    xf = x.astype(jnp.float32)
    y = xf * lax.rsqrt(jnp.mean(xf * xf, axis=-1, keepdims=True) + EPS)
    return (y * g.astype(jnp.float32)).astype(x.dtype)


def rope(x, pos):
    half = ROT_DIM // 2
    inv = ROPE_THETA ** (-jnp.arange(half, dtype=jnp.float32) / half)
    ang = pos.astype(jnp.float32)[:, None] * inv[None, :]
    cos = jnp.cos(ang)[None, :, None, :]
    sin = jnp.sin(ang)[None, :, None, :]
    xf = x.astype(jnp.float32)
    x1, x2, rest = xf[..., :half], xf[..., half:ROT_DIM], xf[..., ROT_DIM:]
    return jnp.concatenate([x1 * cos - x2 * sin, x2 * cos + x1 * sin, rest], axis=-1).astype(x.dtype)


def masked_softmax(s, mask):
    s = jnp.where(mask, s.astype(jnp.float32), -1e30)
    m = jnp.max(s, axis=-1, keepdims=True)
    e = jnp.where(mask, jnp.exp(s - m), 0.0)
    return e / jnp.maximum(jnp.sum(e, axis=-1, keepdims=True), 1e-30)


def project(xn, w_in, pos):
    B, T, _ = xn.shape
    q, kc, ks, kw, gl, pool_in = jnp.split(xn @ w_in, IN_SPLITS, axis=-1)
    q = q.reshape(B, T, N_HEADS, HEAD_DIM)

    def kv(a, rotate):
        a = a.reshape(B, T, 2, N_KV_HEADS, HEAD_DIM)
        if not rotate:
            return a
        return jnp.stack([rope(a[:, :, 0], pos), a[:, :, 1]], axis=2)

    gates = jax.nn.sigmoid(gl.reshape(B, T, N_HEADS, 3).astype(jnp.float32)).astype(xn.dtype)
    return q, rope(q, pos), kv(kc, False), kv(ks, True), kv(kw, True), gates, pool_in


def compress(kv, pe, w1, w2):
    B, L = kv.shape[:2]
    R = CMP_LEN // CMP_STRIDE
    nch = L // CMP_STRIDE
    n_cmp = nch - R + 1
    ch = kv[:, :nch * CMP_STRIDE].reshape(B, nch, CMP_STRIDE, 2, N_KV_HEADS, HEAD_DIM)
    w1b = w1.reshape(2, CMP_LEN, HEAD_DIM, CMP_HID)
    w1r = w1.reshape(2, R, CMP_STRIDE, HEAD_DIM, CMP_HID)
    h = jnp.einsum('cld,cldh->ch', pe, w1b)[None, None, :, None, :]
    for r in range(R):
        u = jnp.einsum('bnscgd,csdh->bncgh', ch, w1r[:, r])
        h = h + u[:, r:r + n_cmp]
    out = jnp.einsum('bncgh,chd->bncgd', jax.nn.gelu(h), w2)
    return out[:, :, 0], out[:, :, 1]


def cat_blocks(parts):
    a = parts[0]
    L = sum(pt.shape[1] for pt in parts)
    n_sel = -(-L // SEL_BLOCK)
    pad = jnp.zeros((a.shape[0], n_sel * SEL_BLOCK - L) + a.shape[2:], a.dtype)
    full = jnp.concatenate(list(parts) + [pad], axis=1)
    return full.reshape((a.shape[0], n_sel, SEL_BLOCK) + a.shape[2:])


def nsa_attention(q, qr, gates, pos_base, cmp_full, sel_blk, win_ext, win_pos0, p):
    B, T = q.shape[:2]
    dt = q.dtype
    ck, cv = compress(cmp_full, p['pe_cmp'], p['w_cmp1'], p['w_cmp2'])
    n_cmp = ck.shape[1]
    n_sel = sel_blk.shape[1]
    k_top = min(SEL_TOPN, n_sel)
    ci = jnp.arange(n_cmp)[:, None] * CMP_STRIDE
    sj = jnp.arange(n_sel)[None, :] * SEL_BLOCK
    overlap = jnp.maximum(jnp.minimum(ci + CMP_LEN, sj + SEL_BLOCK) - jnp.maximum(ci, sj), 0).astype(jnp.float32) / CMP_LEN
    cmp_end = jnp.arange(n_cmp) * CMP_STRIDE + CMP_LEN - 1
    blk_ids = jnp.arange(n_sel)
    qb = math.gcd(T, Q_BLOCK)
    b_idx = jnp.arange(B)[:, None, None, None]
    g_idx = jnp.arange(N_KV_HEADS)[None, None, :, None]
    m_sel = k_top * SEL_BLOCK

    def grp(a):
        return a.reshape((B, qb, N_KV_HEADS, HEADS_PER_KV) + a.shape[3:])

    def block(q0):
        pos = pos_base + q0 + jnp.arange(qb)
        qc = grp(lax.dynamic_slice_in_dim(q, q0, qb, 1))
        qs = grp(lax.dynamic_slice_in_dim(qr, q0, qb, 1))
        g = grp(lax.dynamic_slice_in_dim(gates, q0, qb, 1))
        s = jnp.einsum('bqghd,bngd->bqghn', qc, ck) * SCALE
        p_c = masked_softmax(s, (cmp_end[None, :] <= pos[:, None])[None, :, None, None, :])
        o_c = jnp.einsum('bqghn,bngd->bqghd', p_c.astype(dt), cv)
        imp = jnp.einsum('bqgn,nj->bqgj', p_c.sum(axis=3), overlap)
        cur = (pos // SEL_BLOCK)[:, None]
        valid = (blk_ids[None, :] * SEL_BLOCK <= pos[:, None])[None, :, None, :]
        forced = ((blk_ids[None, :] == 0) | (blk_ids[None, :] == cur) | (blk_ids[None, :] == cur - 1))[None, :, None, :]
        score = jnp.where(valid, imp + jnp.where(forced, FORCE_BONUS, 0.0), -1e9)
        _, idx = lax.top_k(score, k_top)
        kv_s = sel_blk[b_idx, idx, :, :, g_idx]
        kpos = (idx[..., None] * SEL_BLOCK + jnp.arange(SEL_BLOCK)).reshape(B, qb, N_KV_HEADS, 1, m_sel)
        s = jnp.einsum('bqghd,bqgkld->bqghkl', qs, kv_s[..., 0, :]) * SCALE
        p_s = masked_softmax(s.reshape(B, qb, N_KV_HEADS, HEADS_PER_KV, m_sel), kpos <= pos[None, :, None, None, None])
        o_s = jnp.einsum('bqghm,bqgmd->bqghd', p_s.astype(dt), kv_s[..., 1, :].reshape(B, qb, N_KV_HEADS, m_sel, HEAD_DIM))
        kw = lax.dynamic_slice_in_dim(win_ext, q0, WINDOW + qb, 1)
        kp = win_pos0 + q0 + jnp.arange(WINDOW + qb)
        wm = (kp[None, :] <= pos[:, None]) & (kp[None, :] >= pos[:, None] - WINDOW) & (kp[None, :] >= 0)
        s = jnp.einsum('bqghd,bkgd->bqghk', qs, kw[:, :, 0]) * SCALE
        p_w = masked_softmax(s, wm[None, :, None, None, :])
        o_w = jnp.einsum('bqghk,bkgd->bqghd', p_w.astype(dt), kw[:, :, 1])
        o = g[..., 0:1] * o_c + g[..., 1:2] * o_s + g[..., 2:3] * o_w
        return o.reshape(B, qb, N_HEADS, HEAD_DIM)

    out = lax.map(block, jnp.arange(T // qb) * qb)
    return jnp.moveaxis(out, 0, 1).reshape(B, T, ATTN_W)


def pool_mix(ext, pos_base, w_pool):
    B, Lx, _ = ext.shape
    T = Lx - POOL_HIST
    xf = ext.astype(jnp.float32)
    cs = jnp.concatenate([jnp.zeros((B, 1, POOL_W), jnp.float32), jnp.cumsum(xf, axis=1)], axis=1)
    hi = cs[:, POOL_HIST + 1:]
    pos = pos_base + jnp.arange(T)
    means = []
    for gi, w in enumerate(POOL_WINDOWS):
        c = slice(gi * POOL_CH, (gi + 1) * POOL_CH)
        lo = cs[:, POOL_HIST + 1 - w:POOL_HIST + 1 - w + T, c]
        cnt = jnp.minimum(pos + 1, w).astype(jnp.float32)[None, :, None]
        means.append((hi[:, :, c] - lo) / cnt)
    d = (jnp.concatenate(means, axis=-1) - xf[:, POOL_HIST:]).astype(ext.dtype)
    d = d.reshape(B, T, N_POOL_GROUPS, POOL_CH)
    return jnp.einsum('btgc,gcd->btgd', d, w_pool).reshape(B, T, POOL_W)


def cross_attend(xn, mem_kv, w_mq, w_mo):
    B, T, _ = xn.shape
    q = (xn @ w_mq).reshape(B, T, MEM_HEADS, MEM_HEAD_DIM)
    s = jnp.einsum('bthd,bmhd->bhtm', q, mem_kv[:, :, 0]).astype(jnp.float32) * (MEM_HEAD_DIM ** -0.5)
    pr = jax.nn.softmax(s, axis=-1).astype(xn.dtype)
    o = jnp.einsum('bhtm,bmhd->bthd', pr, mem_kv[:, :, 1]).reshape(B, T, MEM_W)
    return o @ w_mo


def layer_tail(x, q, qr, gates, cmp_full, sel_blk, win_ext, win_pos0, pool_ext, mem_kv, pos_base, p):
    a = nsa_attention(q, qr, gates, pos_base, cmp_full, sel_blk, win_ext, win_pos0, p)
    m = pool_mix(pool_ext, pos_base, p['w_pool'])
    mix = jnp.concatenate([rms_norm(a, p['g_attn_out']), rms_norm(m, p['pool_scale'])], axis=-1)
    x = x + mix @ p['w_out']
    x = x + cross_attend(rms_norm(x, p['g_norm2']), mem_kv, p['w_mq'], p['w_mo'])
    h = jax.nn.relu(rms_norm(x, p['g_norm3']) @ p['w_up'])
    return x + (h * h) @ p['w_down']


def prompt_layer(x, mem, p):
    B, S, _ = x.shape
    pos = jnp.arange(S)
    q, qr, cmp_kv, sel_kv, win_kv, gates, pool_in = project(rms_norm(x, p['g_norm1']), p['w_in'], pos)
    win_ext = jnp.concatenate([jnp.zeros((B, WINDOW) + win_kv.shape[2:], win_kv.dtype), win_kv], axis=1)
    pool_ext = jnp.concatenate([jnp.zeros((B, POOL_HIST, POOL_W), pool_in.dtype), pool_in], axis=1)
    mem_kv = (rms_norm(mem, p['g_mem']) @ p['w_mkv']).reshape(B, N_MEM, 2, MEM_HEADS, MEM_HEAD_DIM)
    y = layer_tail(x, q, qr, gates, cmp_kv, cat_blocks([sel_kv]), win_ext, -WINDOW, pool_ext, mem_kv, 0, p)
    state = (cmp_kv, sel_kv, win_kv[:, -min(WINDOW, S):], pool_ext[:, -POOL_HIST:], mem_kv)
    return y, state


def sample_layer(x, cache_cmp, cache_sel, cache_win, st_pool, mem_kv, page_table, p):
    DB, T, _ = x.shape
    past = page_table.shape[1] * PAGE_SIZE
    pos = past + jnp.arange(T)
    q, qr, cmp_kv, sel_kv, win_kv, gates, pool_in = project(rms_norm(x, p['g_norm1']), p['w_in'], pos)

    def gather(pool):
        return pool[page_table].reshape((DB, past) + pool.shape[2:])

    cmp_full = jnp.concatenate([gather(cache_cmp), cmp_kv], axis=1)
    sel_blk = cat_blocks([gather(cache_sel), sel_kv])
    wb = cache_win.shape[1]
    win_ext = jnp.concatenate([jnp.zeros((DB, WINDOW - wb) + win_kv.shape[2:], win_kv.dtype), cache_win, win_kv], axis=1)
    pool_ext = jnp.concatenate([st_pool, pool_in], axis=1)
    y = layer_tail(x, q, qr, gates, cmp_full, sel_blk, win_ext, past - WINDOW, pool_ext, mem_kv, past, p)
    state = (cmp_kv, sel_kv, win_ext[:, -min(WINDOW, wb + T):], pool_ext[:, -POOL_HIST:])
    return y, state


def setup_inputs(seed: int = 0) -> dict:
    key = jax.random.key(seed)
    k = jax.random.split(key, 32)
    n_pages = PAST_LEN // PAGE_SIZE
    n_used = DEC_BATCH * n_pages
    n_phys = n_used + (n_used + 3) // 4
    wb = min(WINDOW, PAST_LEN)

    def nrm(kk, shape, scale=1.0):
        return jax.random.normal(kk, shape, jnp.float32) * scale

    def gain(kk, shape):
        return 1.0 + 0.05 * jax.random.normal(kk, shape, jnp.float32)

    kv_page = (DEPTH, n_phys, PAGE_SIZE, 2, N_KV_HEADS, HEAD_DIM)
    page_table = jax.random.permutation(k[7], n_phys)[:n_used].reshape(DEC_BATCH, n_pages).astype(jnp.int32)
    return {
        'x_prompt': nrm(k[0], (BATCH, SEQ, D_MODEL)),
        'x_sample': nrm(k[1], (DEC_BATCH, DEC_SEQ, D_MODEL)),
        'cache_cmp_kv': nrm(k[2], kv_page),
        'cache_sel_kv': nrm(k[3], kv_page),
        'cache_win_kv': nrm(k[4], (DEPTH, DEC_BATCH, wb, 2, N_KV_HEADS, HEAD_DIM)),
        'state_pool': nrm(k[5], (DEPTH, DEC_BATCH, POOL_HIST, POOL_W)),
        'cache_mem_kv': nrm(k[6], (DEPTH, DEC_BATCH, N_MEM, 2, MEM_HEADS, MEM_HEAD_DIM)),
        'page_table': page_table,
        'mem_prompt': nrm(k[8], (BATCH, N_MEM, D_MODEL)),
        'w_in': nrm(k[9], (DEPTH, D_MODEL, N_IN), D_MODEL ** -0.5),
        'pe_cmp': nrm(k[10], (DEPTH, 2, CMP_LEN, HEAD_DIM), 0.1),
        'w_cmp1': nrm(k[11], (DEPTH, 2, CMP_LEN * HEAD_DIM, CMP_HID), (CMP_LEN * HEAD_DIM) ** -0.5),
        'w_cmp2': nrm(k[12], (DEPTH, 2, CMP_HID, HEAD_DIM), CMP_HID ** -0.5),
        'w_pool': nrm(k[13], (DEPTH, N_POOL_GROUPS, POOL_CH, POOL_CH), POOL_CH ** -0.5),
        'g_attn_out': gain(k[14], (DEPTH, ATTN_W)),
        'pool_scale': gain(k[15], (DEPTH, POOL_W)),
        'w_out': nrm(k[16], (DEPTH, MIX_W, D_MODEL), MIX_W ** -0.5),
        'g_norm1': gain(k[17], (DEPTH, D_MODEL)),
        'g_norm2': gain(k[18], (DEPTH, D_MODEL)),
        'g_mem': gain(k[19], (DEPTH, D_MODEL)),
        'w_mq': nrm(k[20], (DEPTH, D_MODEL, MEM_W), D_MODEL ** -0.5),
        'w_mkv': nrm(k[21], (DEPTH, D_MODEL, 2 * MEM_W), D_MODEL ** -0.5),
        'w_mo': nrm(k[22], (DEPTH, MEM_W, D_MODEL), MEM_W ** -0.5),
        'g_norm3': gain(k[23], (DEPTH, D_MODEL)),
        'w_up': nrm(k[24], (DEPTH, D_MODEL, D_FF), D_MODEL ** -0.5),
        'w_down': nrm(k[25], (DEPTH, D_FF, D_MODEL), D_FF ** -0.5),
        'g_final': gain(k[26], (D_MODEL,)),
    }


def reference(x_prompt, x_sample, cache_cmp_kv, cache_sel_kv, cache_win_kv, state_pool, cache_mem_kv, page_table, mem_prompt, w_in, pe_cmp, w_cmp1, w_cmp2, w_pool, g_attn_out, pool_scale, w_out, g_norm1, g_norm2, g_mem, w_mq, w_mkv, w_mo, g_norm3, w_up, w_down, g_final):
    yp, ys = x_prompt, x_sample
    st_p, st_s = [], []
    for l in range(DEPTH):
        p = {'w_in': w_in[l], 'pe_cmp': pe_cmp[l], 'w_cmp1': w_cmp1[l], 'w_cmp2': w_cmp2[l],
             'w_pool': w_pool[l], 'g_attn_out': g_attn_out[l], 'pool_scale': pool_scale[l],
             'w_out': w_out[l], 'g_norm1': g_norm1[l], 'g_norm2': g_norm2[l], 'g_mem': g_mem[l],
             'w_mq': w_mq[l], 'w_mkv': w_mkv[l], 'w_mo': w_mo[l], 'g_norm3': g_norm3[l],
             'w_up': w_up[l], 'w_down': w_down[l]}
        yp, sp = prompt_layer(yp, mem_prompt, p)
        ys, ss = sample_layer(ys, cache_cmp_kv[l], cache_sel_kv[l], cache_win_kv[l], state_pool[l], cache_mem_kv[l], page_table, p)
        st_p.append(sp)
        st_s.append(ss)
    return (rms_norm(yp, g_final), rms_norm(ys, g_final),
            jnp.stack([s[0] for s in st_p]), jnp.stack([s[1] for s in st_p]), jnp.stack([s[2] for s in st_p]),
            jnp.stack([s[3] for s in st_p]), jnp.stack([s[4] for s in st_p]),
            jnp.stack([s[0] for s in st_s]), jnp.stack([s[1] for s in st_s]), jnp.stack([s[2] for s in st_s]),
            jnp.stack([s[3] for s in st_s]))
```

```python
import functools
import math

import jax
import jax.numpy as jnp
from jax import lax
from jax.experimental import pallas as pl
from jax.experimental.pallas import tpu as pltpu

D_MODEL = 1024
PAGE_SIZE = 128
ATTN_W = 512
POOL_W = 512
HEAD_DIM = 64
N_HEADS = 8
N_KV_HEADS = 2
HEADS_PER_KV = N_HEADS // N_KV_HEADS
SCALE = HEAD_DIM ** -0.5
ROT_DIM = HEAD_DIM // 4
ROPE_THETA = 500000.0
CMP_LEN = 32
CMP_STRIDE = 16
CMP_HID = HEAD_DIM
SEL_BLOCK = 64
SEL_TOPN = 16
WINDOW = 512
Q_BLOCK = 128
FORCE_BONUS = 1000.0
POOL_WINDOWS = (2, 4, 8, 16)
N_POOL_GROUPS = len(POOL_WINDOWS)
POOL_CH = POOL_W // N_POOL_GROUPS
POOL_HIST = max(POOL_WINDOWS) - 1
N_MEM = 256
MEM_HEADS = 4
MEM_HEAD_DIM = 128
MEM_W = MEM_HEADS * MEM_HEAD_DIM
D_FF = 4 * D_MODEL
EPS = 1e-6
KV_W = 2 * N_KV_HEADS * HEAD_DIM
N_GATES = 3 * N_HEADS
LANES = 128
C_Q, C_CMP, C_SEL, C_WIN, C_POOL, C_GATE = 0, 512, 768, 1024, 1280, 1792
N_IN_PAD = C_GATE + LANES
VMEM_LIMIT = 56 * 1024 * 1024

F32 = jnp.float32
BF16 = jnp.bfloat16


def _rms(x, g):
    return x * lax.rsqrt(jnp.mean(x * x, axis=-1, keepdims=True) + EPS) * g


def _dot(a, b):
    return jnp.dot(a, b, preferred_element_type=F32)


def _dot_nt(a, b):
    return lax.dot_general(a, b, (((1,), (1,)), ((), ())), preferred_element_type=F32)


def _const_spec(shape):
    n = len(shape)
    return pl.BlockSpec(shape, lambda *_: (0,) * n, pipeline_mode=pl.Buffered(1))


def _params(*sem):
    return pltpu.CompilerParams(dimension_semantics=sem, vmem_limit_bytes=VMEM_LIMIT)


def _rope(v, cs, sa, sb):
    n = v.shape[1] // LANES
    w = v.shape[1]
    cs, sa, sb = (jnp.tile(t, (1, n)) for t in (cs, sa, sb))
    up = pltpu.roll(v, w - ROT_DIM // 2, 1)
    dn = pltpu.roll(v, ROT_DIM // 2, 1)
    return v * cs + up * sa + dn * sb


def _in_proj_kernel(x_ref, g_ref, w_ref, cs_ref, sa_ref, sb_ref,
                    q_ref, qr_ref, cmp_ref, sel_ref, win_ref, selb_ref, winb_ref, gate_ref, pool_ref):
    xb = _rms(x_ref[...], g_ref[...]).astype(BF16)
    cs, sa, sb = cs_ref[...], sa_ref[...], sb_ref[...]

    q = _dot(xb, w_ref[:, C_Q:C_CMP])
    q_ref[...] = (q * SCALE).astype(BF16)
    qr_ref[...] = (_rope(q, cs, sa, sb) * SCALE).astype(BF16)
    cmp_ref[...] = _dot(xb, w_ref[:, C_CMP:C_SEL])
    for lo, f_ref, b_ref in ((C_SEL, sel_ref, selb_ref), (C_WIN, win_ref, winb_ref)):
        kv = _dot(xb, w_ref[:, lo:lo + KV_W])
        kv = jnp.concatenate([_rope(kv[:, :LANES], cs, sa, sb), kv[:, LANES:]], axis=1)
        f_ref[...] = kv
        b_ref[...] = kv.astype(BF16)
    pool_ref[...] = _dot(xb, w_ref[:, C_POOL:C_GATE])
    gate_ref[...] = jax.nn.sigmoid(_dot(xb, w_ref[:, C_GATE:N_IN_PAD]))


def _rope_tables(pos):
    half = ROT_DIM // 2
    inv = ROPE_THETA ** (-jnp.arange(half, dtype=F32) / half)
    ang = pos.astype(F32)[:, None] * inv[None, :]
    cos, sin = jnp.cos(ang), jnp.sin(ang)
    z = jnp.zeros_like(cos)
    rest0 = jnp.zeros((pos.shape[0], HEAD_DIM - ROT_DIM), F32)
    cs = jnp.concatenate([cos, cos, rest0 + 1.0], axis=1)
    sa = jnp.concatenate([-sin, z, rest0], axis=1)
    sb = jnp.concatenate([z, sin, rest0], axis=1)
    return tuple(jnp.tile(t, (1, LANES // HEAD_DIM)) for t in (cs, sa, sb))


def _in_proj(x, g1, w_in_p, pos_rows, tm):
    n = x.shape[0]
    cs, sa, sb = _rope_tables(pos_rows)
    row = lambda w: pl.BlockSpec((tm, w), lambda i: (i, 0))
    sds = lambda w, dt: jax.ShapeDtypeStruct((n, w), dt)
    return pl.pallas_call(
        _in_proj_kernel,
        grid=(n // tm,),
        in_specs=[row(D_MODEL), _const_spec((1, D_MODEL)), _const_spec((D_MODEL, N_IN_PAD)),
                  row(LANES), row(LANES), row(LANES)],
        out_specs=[row(ATTN_W), row(ATTN_W), row(KV_W), row(KV_W), row(KV_W), row(KV_W), row(KV_W),
                   row(LANES), row(POOL_W)],
        out_shape=[sds(ATTN_W, BF16), sds(ATTN_W, BF16), sds(KV_W, F32), sds(KV_W, F32), sds(KV_W, F32),
                   sds(KV_W, BF16), sds(KV_W, BF16), sds(LANES, F32), sds(POOL_W, F32)],
        compiler_params=_params("parallel"),
        name="in_proj",
    )(x, g1, w_in_p, cs, sa, sb)


def _mem_kv_kernel(m_ref, g_ref, w_ref, o_ref, ob_ref):
    kv = _dot(_rms(m_ref[...], g_ref[...]).astype(BF16), w_ref[...])
    o_ref[...] = kv
    ob_ref[...] = kv.astype(BF16)


def _mem_kv(mem, g_mem, w_mkv_b, tm):
    n = mem.shape[0]
    row = lambda w: pl.BlockSpec((tm, w), lambda i: (i, 0))
    return pl.pallas_call(
        _mem_kv_kernel,
        grid=(n // tm,),
        in_specs=[row(D_MODEL), _const_spec((1, D_MODEL)), _const_spec((D_MODEL, 2 * MEM_W))],
        out_specs=[row(2 * MEM_W), row(2 * MEM_W)],
        out_shape=[jax.ShapeDtypeStruct((n, 2 * MEM_W), F32), jax.ShapeDtypeStruct((n, 2 * MEM_W), BF16)],
        compiler_params=_params("parallel"),
        name="mem_kv",
    )(mem, g_mem, w_mkv_b)


def _mix_kernel(x_ref, a_ref, m_ref, ga_ref, gp_ref, wo_ref, g2_ref, wq_ref, x1_ref, qm_ref):
    mix = jnp.concatenate([_rms(a_ref[...], ga_ref[...]), _rms(m_ref[...], gp_ref[...])], axis=1)
    x1 = x_ref[...] + _dot(mix.astype(BF16), wo_ref[...])
    x1_ref[...] = x1
    qm_ref[...] = _dot(_rms(x1, g2_ref[...]).astype(BF16), wq_ref[...]).astype(BF16)


def _mix(x, a, m, ga, gp, w_out_b, g2, w_mq_b, tm):
    n = x.shape[0]
    row = lambda w: pl.BlockSpec((tm, w), lambda i: (i, 0))
    return pl.pallas_call(
        _mix_kernel,
        grid=(n // tm,),
        in_specs=[row(D_MODEL), row(ATTN_W), row(POOL_W), _const_spec((1, ATTN_W)), _const_spec((1, POOL_W)),
                  _const_spec((D_MODEL, D_MODEL)), _const_spec((1, D_MODEL)), _const_spec((D_MODEL, MEM_W))],
        out_specs=[row(D_MODEL), row(MEM_W)],
        out_shape=[jax.ShapeDtypeStruct((n, D_MODEL), F32), jax.ShapeDtypeStruct((n, MEM_W), BF16)],
        compiler_params=_params("parallel"),
        name="mix_out",
    )(x, a, m, ga, gp, w_out_b, g2, w_mq_b)


def _cross_kernel(q_ref, kv_ref, o_ref, *, n_mem_blk, rows):
    for i in range(n_mem_blk):
        outs = []
        for h in range(MEM_HEADS):
            q = q_ref[i * rows:(i + 1) * rows, h * MEM_HEAD_DIM:(h + 1) * MEM_HEAD_DIM]
            k = kv_ref[i, :, h * MEM_HEAD_DIM:(h + 1) * MEM_HEAD_DIM].astype(BF16)
            v = kv_ref[i, :, MEM_W + h * MEM_HEAD_DIM:MEM_W + (h + 1) * MEM_HEAD_DIM].astype(BF16)
            s = _dot_nt(q, k) * (MEM_HEAD_DIM ** -0.5)
            e = jnp.exp(s - jnp.max(s, axis=-1, keepdims=True))
            o = _dot(e.astype(BF16), v) / jnp.sum(e, axis=-1, keepdims=True)
            outs.append(o)
        o_ref[i * rows:(i + 1) * rows, :] = jnp.concatenate(outs, axis=1).astype(BF16)


def _cross(qm, mem_kv, rows, n_mem_blk, tiles_per_mem):
    n = qm.shape[0]
    tm = rows * n_mem_blk
    return pl.pallas_call(
        functools.partial(_cross_kernel, n_mem_blk=n_mem_blk, rows=rows),
        grid=(n // tm,),
        in_specs=[pl.BlockSpec((tm, MEM_W), lambda i: (i, 0)),
                  pl.BlockSpec((n_mem_blk, N_MEM, 2 * MEM_W), lambda i: (i // tiles_per_mem, 0, 0))],
        out_specs=pl.BlockSpec((tm, MEM_W), lambda i: (i, 0)),
        out_shape=jax.ShapeDtypeStruct((n, MEM_W), BF16),
        compiler_params=_params("parallel"),
        name="cross_attn",
    )(qm, mem_kv)


FF_CHUNK = 1024


def _ffn_kernel(x1_ref, o_ref, wmo_ref, g3_ref, wup_ref, wdn_ref, gf_ref, y_ref):
    x2 = x1_ref[...] + _dot(o_ref[...], wmo_ref[...])
    xn = _rms(x2, g3_ref[...]).astype(BF16)
    acc = x2
    for c in range(D_FF // FF_CHUNK):
        h = jnp.maximum(_dot(xn, wup_ref[:, c * FF_CHUNK:(c + 1) * FF_CHUNK]), 0.0)
        acc = acc + _dot((h * h).astype(BF16), wdn_ref[c * FF_CHUNK:(c + 1) * FF_CHUNK, :])
    y_ref[...] = _rms(acc, gf_ref[...])


def _ffn(x1, o, w_mo_b, g3, w_up_b, w_down_b, g_final, tm):
    n = x1.shape[0]
    row = lambda w: pl.BlockSpec((tm, w), lambda i: (i, 0))
    return pl.pallas_call(
        _ffn_kernel,
        grid=(n // tm,),
        in_specs=[row(D_MODEL), row(MEM_W), _const_spec((MEM_W, D_MODEL)), _const_spec((1, D_MODEL)),
                  _const_spec((D_MODEL, D_FF)), _const_spec((D_FF, D_MODEL)), _const_spec((1, D_MODEL))],
        out_specs=row(D_MODEL),
        out_shape=jax.ShapeDtypeStruct((n, D_MODEL), F32),
        compiler_params=_params("parallel"),
        name="ffn",
    )(x1, o, w_mo_b, g3, w_up_b, w_down_b, g_final)


def _masked_softmax(s, mask):
    s = jnp.where(mask, s.astype(F32), -1e30)
    m = jnp.max(s, axis=-1, keepdims=True)
    e = jnp.where(mask, jnp.exp(s - m), 0.0)
    return e / jnp.maximum(jnp.sum(e, axis=-1, keepdims=True), 1e-30)


def _compress_jax(kv, pe, w1, w2):
    B, L = kv.shape[:2]
    R = CMP_LEN // CMP_STRIDE
    nch = L // CMP_STRIDE
    n_cmp = nch - R + 1
    ch = kv[:, :nch * CMP_STRIDE].reshape(B, nch, CMP_STRIDE, 2, N_KV_HEADS, HEAD_DIM)
    w1b = w1.reshape(2, CMP_LEN, HEAD_DIM, CMP_HID)
    w1r = w1.reshape(2, R, CMP_STRIDE, HEAD_DIM, CMP_HID)
    h = jnp.einsum('cld,cldh->ch', pe, w1b)[None, None, :, None, :]
    for r in range(R):
        u = jnp.einsum('bnscgd,csdh->bncgh', ch, w1r[:, r])
        h = h + u[:, r:r + n_cmp]
    out = jnp.einsum('bncgh,chd->bncgd', jax.nn.gelu(h), w2)
    return out[:, :, 0], out[:, :, 1]


def _cat_blocks(parts):
    a = parts[0]
    L = sum(pt.shape[1] for pt in parts)
    n_sel = -(-L // SEL_BLOCK)
    pad = jnp.zeros((a.shape[0], n_sel * SEL_BLOCK - L) + a.shape[2:], a.dtype)
    full = jnp.concatenate(list(parts) + [pad], axis=1)
    return full.reshape((a.shape[0], n_sel, SEL_BLOCK) + a.shape[2:])


def _nsa_jax(q, qr, gates, pos_base, cmp_full, sel_blk, win_ext, win_pos0, pe, w1, w2):
    B, T = q.shape[:2]
    dt = q.dtype
    ck, cv = _compress_jax(cmp_full, pe, w1, w2)
    n_cmp = ck.shape[1]
    n_sel = sel_blk.shape[1]
    k_top = min(SEL_TOPN, n_sel)
    ci = jnp.arange(n_cmp)[:, None] * CMP_STRIDE
    sj = jnp.arange(n_sel)[None, :] * SEL_BLOCK
    overlap = jnp.maximum(jnp.minimum(ci + CMP_LEN, sj + SEL_BLOCK) - jnp.maximum(ci, sj), 0).astype(F32) / CMP_LEN
    cmp_end = jnp.arange(n_cmp) * CMP_STRIDE + CMP_LEN - 1
    blk_ids = jnp.arange(n_sel)
    qb = math.gcd(T, Q_BLOCK)
    b_idx = jnp.arange(B)[:, None, None, None]
    g_idx = jnp.arange(N_KV_HEADS)[None, None, :, None]
    m_sel = k_top * SEL_BLOCK

    def grp(a):
        return a.reshape((B, qb, N_KV_HEADS, HEADS_PER_KV) + a.shape[3:])

    def block(q0):
        pos = pos_base + q0 + jnp.arange(qb)
        qc = grp(lax.dynamic_slice_in_dim(q, q0, qb, 1))
        qs = grp(lax.dynamic_slice_in_dim(qr, q0, qb, 1))
        g = grp(lax.dynamic_slice_in_dim(gates, q0, qb, 1))
        s = jnp.einsum('bqghd,bngd->bqghn', qc, ck) * SCALE
        p_c = _masked_softmax(s, (cmp_end[None, :] <= pos[:, None])[None, :, None, None, :])
        o_c = jnp.einsum('bqghn,bngd->bqghd', p_c.astype(dt), cv)
        imp = jnp.einsum('bqgn,nj->bqgj', p_c.sum(axis=3), overlap)
        cur = (pos // SEL_BLOCK)[:, None]
        valid = (blk_ids[None, :] * SEL_BLOCK <= pos[:, None])[None, :, None, :]
        forced = ((blk_ids[None, :] == 0) | (blk_ids[None, :] == cur) | (blk_ids[None, :] == cur - 1))[None, :, None, :]
        score = jnp.where(valid, imp + jnp.where(forced, FORCE_BONUS, 0.0), -1e9)
        _, idx = lax.top_k(score, k_top)
        kv_s = sel_blk[b_idx, idx, :, :, g_idx]
        kpos = (idx[..., None] * SEL_BLOCK + jnp.arange(SEL_BLOCK)).reshape(B, qb, N_KV_HEADS, 1, m_sel)
        s = jnp.einsum('bqghd,bqgkld->bqghkl', qs, kv_s[..., 0, :]) * SCALE
        p_s = _masked_softmax(s.reshape(B, qb, N_KV_HEADS, HEADS_PER_KV, m_sel), kpos <= pos[None, :, None, None, None])
        o_s = jnp.einsum('bqghm,bqgmd->bqghd', p_s.astype(dt), kv_s[..., 1, :].reshape(B, qb, N_KV_HEADS, m_sel, HEAD_DIM))
        kw = lax.dynamic_slice_in_dim(win_ext, q0, WINDOW + qb, 1)
        kp = win_pos0 + q0 + jnp.arange(WINDOW + qb)
        wm = (kp[None, :] <= pos[:, None]) & (kp[None, :] >= pos[:, None] - WINDOW) & (kp[None, :] >= 0)
        s = jnp.einsum('bqghd,bkgd->bqghk', qs, kw[:, :, 0]) * SCALE
        p_w = _masked_softmax(s, wm[None, :, None, None, :])
        o_w = jnp.einsum('bqghk,bkgd->bqghd', p_w.astype(dt), kw[:, :, 1])
        o = g[..., 0:1] * o_c + g[..., 1:2] * o_s + g[..., 2:3] * o_w
        return o.reshape(B, qb, N_HEADS, HEAD_DIM)

    out = lax.map(block, jnp.arange(T // qb) * qb)
    return jnp.moveaxis(out, 0, 1).reshape(B, T, ATTN_W)


def _pool_jax(ext, pos_base, w_pool):
    B, Lx, _ = ext.shape
    T = Lx - POOL_HIST
    xf = ext.astype(F32)
    cs = jnp.concatenate([jnp.zeros((B, 1, POOL_W), F32), jnp.cumsum(xf, axis=1)], axis=1)
    hi = cs[:, POOL_HIST + 1:]
    pos = pos_base + jnp.arange(T)
    means = []
    for gi, w in enumerate(POOL_WINDOWS):
        c = slice(gi * POOL_CH, (gi + 1) * POOL_CH)
        lo = cs[:, POOL_HIST + 1 - w:POOL_HIST + 1 - w + T, c]
        cnt = jnp.minimum(pos + 1, w).astype(F32)[None, :, None]
        means.append((hi[:, :, c] - lo) / cnt)
    d = (jnp.concatenate(means, axis=-1) - xf[:, POOL_HIST:]).astype(ext.dtype)
    d = d.reshape(B, T, N_POOL_GROUPS, POOL_CH)
    return jnp.einsum('btgc,gcd->btgd', d, w_pool).reshape(B, T, POOL_W)


def _permute_w_in(w_in):
    q, kc, ks, kw, gl, pool = jnp.split(w_in, (512, 768, 1024, 1280, 1280 + N_GATES), axis=1)
    pad = jnp.zeros((D_MODEL, LANES - N_GATES), w_in.dtype)
    return jnp.concatenate([q, kc, ks, kw, pool, gl, pad], axis=1).astype(BF16)


def _tile(n, pref):
    t = min(n, pref)
    assert n % t == 0
    return t


def kernel(x_prompt, x_sample, cache_cmp_kv, cache_sel_kv, cache_win_kv, state_pool, cache_mem_kv, page_table, mem_prompt, w_in, pe_cmp, w_cmp1, w_cmp2, w_pool, g_attn_out, pool_scale, w_out, g_norm1, g_norm2, g_mem, w_mq, w_mkv, w_mo, g_norm3, w_up, w_down, g_final):
    assert w_in.shape[0] == 1, "single-layer trunk"
    B, S, _ = x_prompt.shape
    DB, T, _ = x_sample.shape
    past = page_table.shape[1] * PAGE_SIZE
    kv6 = (2, N_KV_HEADS, HEAD_DIM)
    r2 = lambda v: v.reshape(1, -1)

    w_in_p = _permute_w_in(w_in[0])
    w_out_b, w_mq_b, w_mkv_b, w_mo_b = (w[0].astype(BF16) for w in (w_out, w_mq, w_mkv, w_mo))
    w_up_b, w_down_b = w_up[0].astype(BF16), w_down[0].astype(BF16)
    g1, g2, g3, gm, ga, gp, gf = (r2(v) for v in (g_norm1[0], g_norm2[0], g_norm3[0], g_mem[0], g_attn_out[0],
                                                  pool_scale[0], g_final))
    pe, w1, w2 = pe_cmp[0], w_cmp1[0], w_cmp2[0]

    def tail(x, a, m, mem_kv, rows, n_mem_blk, tiles_per_mem, tm):
        x1, qm = _mix(x, a, m, ga, gp, w_out_b, g2, w_mq_b, tm)
        o = _cross(qm, mem_kv, rows, n_mem_blk, tiles_per_mem)
        return _ffn(x1, o, w_mo_b, g3, w_up_b, w_down_b, gf, tm)

    np_tok = B * S
    tm_p = _tile(S, 512)
    xp = x_prompt.reshape(np_tok, D_MODEL)
    pos_p = jnp.tile(jnp.arange(S), B)
    q, qr, cmp_kv, sel_kv, win_kv, _, _, gates, pool_in = _in_proj(xp, g1, w_in_p, pos_p, tm_p)
    mem_f, mem_b = _mem_kv(mem_prompt.reshape(B * N_MEM, D_MODEL), gm, w_mkv_b, _tile(B * N_MEM, 256))

    qf = (q.astype(F32) / SCALE).reshape(B, S, N_HEADS, HEAD_DIM)
    qrf = (qr.astype(F32) / SCALE).reshape(B, S, N_HEADS, HEAD_DIM)
    g3d = gates[:, :N_GATES].reshape(B, S, N_HEADS, 3)
    cmp5, sel5, win5 = (v.reshape((B, S) + kv6) for v in (cmp_kv, sel_kv, win_kv))
    win_ext = jnp.concatenate([jnp.zeros((B, WINDOW) + kv6, F32), win5], axis=1)
    pool3 = pool_in.reshape(B, S, POOL_W)
    pool_ext = jnp.concatenate([jnp.zeros((B, POOL_HIST, POOL_W), F32), pool3], axis=1)
    a = _nsa_jax(qf, qrf, g3d, 0, cmp5, _cat_blocks([sel5]), win_ext, -WINDOW, pe, w1, w2)
    m = _pool_jax(pool_ext, 0, w_pool[0])
    y_p = tail(xp, a.reshape(np_tok, ATTN_W), m.reshape(np_tok, POOL_W), mem_b.reshape(B, N_MEM, 2 * MEM_W),
               tm_p, 1, S // tm_p, tm_p)

    ns_tok = DB * T
    xs = x_sample.reshape(ns_tok, D_MODEL)
    pos_s = jnp.tile(past + jnp.arange(T), DB)
    tm_s = _tile(ns_tok, 512)
    q, qr, cmp_kv, sel_kv, win_kv, _, _, gates, pool_in = _in_proj(xs, g1, w_in_p, pos_s, tm_s)
    qf = (q.astype(F32) / SCALE).reshape(DB, T, N_HEADS, HEAD_DIM)
    qrf = (qr.astype(F32) / SCALE).reshape(DB, T, N_HEADS, HEAD_DIM)
    g3d = gates[:, :N_GATES].reshape(DB, T, N_HEADS, 3)
    cmp5s, sel5s, win5s = (v.reshape((DB, T) + kv6) for v in (cmp_kv, sel_kv, win_kv))

    def gather(pool):
        return pool[page_table].reshape((DB, past) + pool.shape[2:])

    cmp_full = jnp.concatenate([gather(cache_cmp_kv[0]), cmp5s], axis=1)
    sel_blk = _cat_blocks([gather(cache_sel_kv[0]), sel5s])
    wb = cache_win_kv.shape[2]
    win_ext_s = jnp.concatenate([jnp.zeros((DB, WINDOW - wb) + kv6, F32), cache_win_kv[0], win5s], axis=1)
    pool_ext_s = jnp.concatenate([state_pool[0], pool_in.reshape(DB, T, POOL_W)], axis=1)
    a = _nsa_jax(qf, qrf, g3d, past, cmp_full, sel_blk, win_ext_s, past - WINDOW, pe, w1, w2)
    m = _pool_jax(pool_ext_s, past, w_pool[0])
    nb = _tile(DB, 4)
    mem_s = cache_mem_kv[0].reshape(DB, N_MEM, 2 * MEM_W)
    x1, qm = _mix(xs, a.reshape(ns_tok, ATTN_W), m.reshape(ns_tok, POOL_W), ga, gp, w_out_b, g2, w_mq_b, tm_s)
    o = _cross(qm, mem_s, T, nb, 1)
    y_s = _ffn(x1, o, w_mo_b, g3, w_up_b, w_down_b, gf, tm_s)

    return (y_p.reshape(B, S, D_MODEL), y_s.reshape(DB, T, D_MODEL),
            cmp5[None], sel5[None], win5[None, :, -min(WINDOW, S):], pool_ext[None, :, -POOL_HIST:],
            mem_f.reshape(B, N_MEM, 2, MEM_HEADS, MEM_HEAD_DIM)[None],
            cmp5s[None], sel5s[None], win_ext_s[None, :, -min(WINDOW, wb + T):], pool_ext_s[None, :, -POOL_HIST:])
```

```python
import functools

import jax
import jax.numpy as jnp
from jax import lax
from jax.experimental import pallas as pl
from jax.experimental.pallas import tpu as pltpu

D_MODEL = 1024
PAGE_SIZE = 128
ATTN_W = 512
POOL_W = 512
HEAD_DIM = 64
N_HEADS = 8
N_KV_HEADS = 2
HEADS_PER_KV = N_HEADS // N_KV_HEADS
SCALE = HEAD_DIM ** -0.5
ROT_DIM = HEAD_DIM // 4
ROPE_THETA = 500000.0
CMP_LEN = 32
CMP_STRIDE = 16
CMP_HID = HEAD_DIM
SEL_BLOCK = 64
SEL_TOPN = 16
WINDOW = 512
FORCE_BONUS = 1000.0
POOL_WINDOWS = (2, 4, 8, 16)
N_POOL_GROUPS = len(POOL_WINDOWS)
POOL_CH = POOL_W // N_POOL_GROUPS
POOL_HIST = max(POOL_WINDOWS) - 1
N_MEM = 256
MEM_HEADS = 4
MEM_HEAD_DIM = 128
MEM_W = MEM_HEADS * MEM_HEAD_DIM
D_FF = 4 * D_MODEL
EPS = 1e-6
KV_W = 2 * N_KV_HEADS * HEAD_DIM
N_GATES = 3 * N_HEADS
LANES = 128
VMEM_LIMIT = 56 * 1024 * 1024
QW = N_HEADS * LANES
TQ = 128
NEG = -1e30
CHUNK_W = CMP_STRIDE * KV_W
CMP_R = CMP_LEN // CMP_STRIDE
CHUNKS_PER_PAGE = PAGE_SIZE // CMP_STRIDE
HALO = 16
FF_CHUNK = 1024

F32 = jnp.float32
BF16 = jnp.bfloat16


def _rms(x, g):
    return x * lax.rsqrt(jnp.mean(x * x, axis=-1, keepdims=True) + EPS) * g


def _dot(a, b):
    return jnp.dot(a, b, preferred_element_type=F32)


def _dot_nt(a, b):
    return lax.dot_general(a, b, (((1,), (1,)), ((), ())), preferred_element_type=F32)


def _const_spec(shape):
    n = len(shape)
    return pl.BlockSpec(shape, lambda *_: (0,) * n, pipeline_mode=pl.Buffered(1))


def _params(*sem):
    return pltpu.CompilerParams(dimension_semantics=sem, vmem_limit_bytes=VMEM_LIMIT)


X_Q, X_CMP, X_SEL, X_WIN, X_POOL, X_GATE = 0, 1024, 1280, 1536, 1792, 2304
X_IN = X_GATE + LANES


def _permute_w_in(w_in):
    q, kc, ks, kw, gl, pool = jnp.split(w_in, (512, 768, 1024, 1280, 1280 + N_GATES), axis=1)
    z = jnp.zeros((D_MODEL, HEAD_DIM), w_in.dtype)
    qx = []
    for h in range(N_HEADS):
        qh = q[:, h * HEAD_DIM:(h + 1) * HEAD_DIM]
        qx += [qh, z] if h // HEADS_PER_KV == 0 else [z, qh]
    pad = jnp.zeros((D_MODEL, LANES - N_GATES), w_in.dtype)
    return jnp.concatenate(qx + [kc, ks, kw, pool, gl, pad], axis=1).astype(BF16)


def _rope_tables(pos):
    half = ROT_DIM // 2
    inv = ROPE_THETA ** (-jnp.arange(half, dtype=F32) / half)
    ang = pos.astype(F32)[:, None] * inv[None, :]
    cos, sin = jnp.cos(ang), jnp.sin(ang)
    z = jnp.zeros_like(cos)
    rest0 = jnp.zeros((pos.shape[0], HEAD_DIM - ROT_DIM), F32)
    cs = jnp.concatenate([cos, cos, rest0 + 1.0], axis=1)
    sa = jnp.concatenate([-sin, z, rest0], axis=1)
    sb = jnp.concatenate([z, sin, rest0], axis=1)
    return tuple(jnp.tile(t, (1, LANES // HEAD_DIM)) for t in (cs, sa, sb))


def _rope(v, cs, sa, sb):
    n = v.shape[1] // LANES
    w = v.shape[1]
    cs, sa, sb = (jnp.tile(t, (1, n)) for t in (cs, sa, sb))
    return v * cs + pltpu.roll(v, w - ROT_DIM // 2, 1) * sa + pltpu.roll(v, ROT_DIM // 2, 1) * sb


def _in_proj_kernel(x_ref, g_ref, w_ref, cs_ref, sa_ref, sb_ref,
                    q_ref, qr_ref, cmp_ref, sel_ref, win_ref, selb_ref, winb_ref, gate_ref, pool_ref):
    xb = _rms(x_ref[...], g_ref[...]).astype(BF16)
    cs, sa, sb = cs_ref[...], sa_ref[...], sb_ref[...]
    q = _dot(xb, w_ref[:, X_Q:X_CMP])
    q_ref[...] = (q * SCALE).astype(BF16)
    qr_ref[...] = (_rope(q, cs, sa, sb) * SCALE).astype(BF16)
    cmp_ref[...] = _dot(xb, w_ref[:, X_CMP:X_SEL])
    for lo, f_ref, b_ref in ((X_SEL, sel_ref, selb_ref), (X_WIN, win_ref, winb_ref)):
        kv = _dot(xb, w_ref[:, lo:lo + KV_W])
        kv = jnp.concatenate([_rope(kv[:, :LANES], cs, sa, sb), kv[:, LANES:]], axis=1)
        f_ref[...] = kv
        b_ref[...] = kv.astype(BF16)
    pool_ref[...] = _dot(xb, w_ref[:, X_POOL:X_GATE])
    gate_ref[...] = jax.nn.sigmoid(_dot(xb, w_ref[:, X_GATE:X_IN]))


def _in_proj(x, g1, w_in_p, pos_rows, tm):
    n = x.shape[0]
    cs, sa, sb = _rope_tables(pos_rows)
    row = lambda w: pl.BlockSpec((tm, w), lambda i: (i, 0))
    sds = lambda w, dt: jax.ShapeDtypeStruct((n, w), dt)
    return pl.pallas_call(
        _in_proj_kernel,
        grid=(n // tm,),
        in_specs=[row(D_MODEL), _const_spec((1, D_MODEL)), _const_spec((D_MODEL, X_IN)),
                  row(LANES), row(LANES), row(LANES)],
        out_specs=[row(QW), row(QW), row(KV_W), row(KV_W), row(KV_W), row(KV_W), row(KV_W), row(LANES), row(POOL_W)],
        out_shape=[sds(QW, BF16), sds(QW, BF16), sds(KV_W, F32), sds(KV_W, F32), sds(KV_W, F32),
                   sds(KV_W, BF16), sds(KV_W, BF16), sds(LANES, F32), sds(POOL_W, F32)],
        compiler_params=_params("parallel"),
        name="in_proj",
    )(x, g1, w_in_p, cs, sa, sb)


def _mem_kv_kernel(m_ref, g_ref, w_ref, o_ref, ob_ref):
    kv = _dot(_rms(m_ref[...], g_ref[...]).astype(BF16), w_ref[...])
    o_ref[...] = kv
    ob_ref[...] = kv.astype(BF16)


def _mem_kv(mem, g_mem, w_mkv_b, tm):
    n = mem.shape[0]
    row = lambda w: pl.BlockSpec((tm, w), lambda i: (i, 0))
    return pl.pallas_call(
        _mem_kv_kernel,
        grid=(n // tm,),
        in_specs=[row(D_MODEL), _const_spec((1, D_MODEL)), _const_spec((D_MODEL, 2 * MEM_W))],
        out_specs=[row(2 * MEM_W), row(2 * MEM_W)],
        out_shape=[jax.ShapeDtypeStruct((n, 2 * MEM_W), F32), jax.ShapeDtypeStruct((n, 2 * MEM_W), BF16)],
        compiler_params=_params("parallel"),
        name="mem_kv",
    )(mem, g_mem, w_mkv_b)


def _compress_weights(pe, w1, w2):
    w1r = w1.reshape(2, CMP_R, CMP_STRIDE, HEAD_DIM, CMP_HID)
    eye_c = jnp.eye(2, dtype=w1.dtype)
    eye_g = jnp.eye(N_KV_HEADS, dtype=w1.dtype)
    w1big = jnp.einsum('crsdh,cx,gy->scgdrxyh', w1r, eye_c, eye_g).reshape(CHUNK_W, CMP_R * KV_W)
    pe_term = jnp.einsum('cld,cldh->ch', pe, w1.reshape(2, CMP_LEN, HEAD_DIM, CMP_HID))
    bias = jnp.broadcast_to(pe_term[:, None, :], (2, N_KV_HEADS, CMP_HID)).reshape(1, KV_W)
    w2big = jnp.einsum('chd,cx,gy->cghxyd', w2, eye_c, eye_g).reshape(KV_W, KV_W)
    return w1big.astype(BF16), bias, w2big.astype(BF16)


def _compress_kernel(*refs, n_in, has_pt):
    if has_pt:
        refs = refs[1:]
    x_refs = refs[:n_in]
    w1_ref, b_ref, w2_ref, o_ref, carry_ref = refs[n_in:]

    @pl.when(pl.program_id(1) == 0)
    def _():
        carry_ref[...] = jnp.zeros_like(carry_ref)

    x = jnp.concatenate([r[0] for r in x_refs], axis=0) if n_in > 1 else x_refs[0][0]
    u = _dot(x.astype(BF16), w1_ref[...])
    u0, u1 = u[:, :KV_W], u[:, KV_W:]
    rows = u0.shape[0]
    row = lax.broadcasted_iota(jnp.int32, u0.shape, 0)
    prev = jnp.where(row == 0, carry_ref[0:1, :], pltpu.roll(u0, 1, 0))
    carry_ref[...] = jnp.broadcast_to(u0[rows - 1:rows, :], carry_ref.shape)
    h = prev + u1 + b_ref[...]
    o_ref[0] = _dot(jax.nn.gelu(h).astype(BF16), w2_ref[...]).astype(BF16)


def _compress_call(n_seq, nch, rows, n_in, has_pt, x_spec_fn):
    grid = (n_seq, nch // rows)
    in_specs = [x_spec_fn(k) for k in range(n_in)] + [
        _const_spec((CHUNK_W, CMP_R * KV_W)), _const_spec((1, KV_W)), _const_spec((KV_W, KV_W))]
    out_spec = pl.BlockSpec((1, rows, KV_W), (lambda b, i, *_: (b, i, 0)))
    kern = functools.partial(_compress_kernel, n_in=n_in, has_pt=has_pt)
    common = dict(out_shape=jax.ShapeDtypeStruct((n_seq, nch, KV_W), BF16),
                  compiler_params=_params("parallel", "arbitrary"), name="compress")
    scratch = [pltpu.VMEM((8, KV_W), F32)]
    if has_pt:
        return pl.pallas_call(kern, grid_spec=pltpu.PrefetchScalarGridSpec(
            num_scalar_prefetch=1, grid=grid, in_specs=in_specs, out_specs=out_spec, scratch_shapes=scratch), **common)
    return pl.pallas_call(kern, grid=grid, in_specs=in_specs, out_specs=out_spec, scratch_shapes=scratch, **common)


def _compress_prompt(cmp_kv, n_seq, cw):
    nch = cmp_kv.shape[0] // n_seq // CMP_STRIDE
    rows = min(nch, 128)
    x = cmp_kv.reshape(n_seq, nch, CHUNK_W)
    spec = lambda k: pl.BlockSpec((1, rows, CHUNK_W), lambda b, i: (b, i, 0))
    return _compress_call(n_seq, nch, rows, 1, False, spec)(x, *cw)


def _compress_paged(cache, page_table, cw, pages_per_step):
    n_seq, n_pages = page_table.shape
    P = min(pages_per_step, n_pages)
    assert n_pages % P == 0
    x = cache.reshape(cache.shape[0], CHUNKS_PER_PAGE, CHUNK_W)
    spec = lambda k: pl.BlockSpec((1, CHUNKS_PER_PAGE, CHUNK_W), lambda b, i, pt: (pt[b, i * P + k], 0, 0))
    call = _compress_call(n_seq, n_pages * CHUNKS_PER_PAGE, P * CHUNKS_PER_PAGE, P, True, spec)
    return call(page_table, *([x] * P), *cw)


def _overlap_t(nch, n_sel_pad):
    m = jnp.arange(nch)[None, :]
    j = jnp.arange(n_sel_pad)[:, None]
    lo = (m - 1) * CMP_STRIDE
    ov = jnp.maximum(jnp.minimum(lo + CMP_LEN, (j + 1) * SEL_BLOCK) - jnp.maximum(lo, j * SEL_BLOCK), 0)
    return jnp.where(m >= 1, ov.astype(F32) / CMP_LEN, 0.0).astype(BF16)


def _heads_to_rows(x):
    return jnp.concatenate([x[:, h * LANES:(h + 1) * LANES] for h in range(N_HEADS)], axis=0)


def _topk_bias(imp, pos, n_blk, k_top, axis):
    j = lax.broadcasted_iota(jnp.int32, imp.shape, axis)
    cur = lax.shift_right_logical(pos, 6)
    forced = (j == 0) | (j == cur) | (j == cur - 1)
    score = jnp.where(j * SEL_BLOCK <= pos, imp + jnp.where(forced, FORCE_BONUS, 0.0), -1e9)
    score = jnp.where(j < n_blk, score, -jnp.inf)
    bias = jnp.full(imp.shape, NEG, F32)
    for _ in range(k_top):
        mx = jnp.max(score, axis=axis, keepdims=True)
        idx = jnp.min(jnp.where(score == mx, j, LANES), axis=axis, keepdims=True)
        hit = j == idx
        bias = jnp.where(hit, 0.0, bias)
        score = jnp.where(hit, -jnp.inf, score)
    return bias


def _split_dot(dot, ps):
    hi = ps.astype(BF16)
    lo = (ps - hi.astype(F32)).astype(BF16)
    return dot(hi) + dot(lo)


def _softmax_cols(s, mask):
    s = jnp.where(mask, s, NEG)
    e = jnp.where(mask, jnp.exp(s - jnp.max(s, axis=0, keepdims=True)), 0.0)
    return e, jnp.maximum(jnp.sum(e, axis=0, keepdims=True), 1e-30)


def _nsa_prompt_kernel(q_ref, qr_ref, g_ref, ckv_ref, ksel_ref, kwin_ref, ovt_ref, a_ref, vts_ref, vtw_ref,
                       *, tk, n_sel, k_top):
    i = pl.program_id(1)
    q0 = pl.multiple_of(i * TQ, TQ)

    @pl.when(i == 0)
    def _():
        vts_ref[...] = jnp.zeros_like(vts_ref)
        vtw_ref[...] = jnp.zeros_like(vtw_ref)

    for k_ref, vt_ref in ((ksel_ref, vts_ref), (kwin_ref, vtw_ref)):
        v = k_ref[pl.ds(q0, TQ), LANES:2 * LANES].astype(F32)
        vt_ref[:, pl.ds(q0, TQ)] = v.T.astype(BF16)

    nl = N_HEADS * TQ
    lane = lax.broadcasted_iota(jnp.int32, (1, nl), 1)
    pos = q0 + (lane & (TQ - 1))
    qc = _heads_to_rows(q_ref)
    qr = _heads_to_rows(qr_ref)

    nc = ckv_ref.shape[1]
    ck = ckv_ref[0, :, 0:LANES]
    cv_t = ckv_ref[0, :, LANES:2 * LANES].astype(F32).T.astype(BF16)
    m_idx = lax.broadcasted_iota(jnp.int32, (nc, 1), 0)
    cmask = (m_idx >= 1) & (m_idx * CMP_STRIDE + (CMP_STRIDE - 1) <= pos)
    e, den = _softmax_cols(_dot_nt(ck, qc), cmask)
    p_c = e / den
    oc_t = _dot(cv_t, p_c.astype(BF16))

    pos_t = q0 + lax.broadcasted_iota(jnp.int32, (1, TQ), 1)
    biases = []
    for g in range(N_KV_HEADS):
        ps = sum(p_c[:, (g * HEADS_PER_KV + hh) * TQ:(g * HEADS_PER_KV + hh + 1) * TQ] for hh in range(HEADS_PER_KV))
        imp_t = _split_dot(lambda v: _dot(ovt_ref[...], v), ps)
        biases.append(_topk_bias(imp_t, pos_t, n_sel, k_top, 0).T.astype(BF16))
    bias_rows = jnp.concatenate([biases[h // HEADS_PER_KV] for h in range(N_HEADS)], axis=0)
    qs = jnp.concatenate([qr, bias_rows], axis=1)

    def sel_chunk(c0, masked, carry):
        m, l, acc = carry
        key = c0 + lax.broadcasted_iota(jnp.int32, (tk, 1), 0)
        blk = lax.shift_right_logical(key, 6)
        onehot = jnp.where(blk == lax.broadcasted_iota(jnp.int32, (tk, LANES), 1), 1.0, 0.0).astype(BF16)
        kx = jnp.concatenate([ksel_ref[pl.ds(c0, tk), 0:LANES], onehot], axis=1)
        s = _dot_nt(kx, qs)
        if masked:
            s = jnp.where(key <= pos, s, NEG)
        m_new = jnp.maximum(m, jnp.max(s, axis=0, keepdims=True))
        alpha = jnp.exp(m - m_new)
        p = jnp.exp(s - m_new)
        l = alpha * l + jnp.sum(p, axis=0, keepdims=True)
        acc = alpha * acc + _dot(vts_ref[:, pl.ds(c0, tk)], p.astype(BF16))
        return m_new, l, acc

    n_full = q0 // tk
    carry = (jnp.full((1, nl), NEG, F32), jnp.zeros((1, nl), F32), jnp.zeros((LANES, nl), F32))
    carry = lax.fori_loop(0, n_full, lambda c, cr: sel_chunk(pl.multiple_of(c * tk, tk), False, cr), carry)
    _, l, acc = sel_chunk(pl.multiple_of(n_full * tk, tk), True, carry)
    os_t = acc / jnp.maximum(l, 1e-30)

    wk = WINDOW + TQ
    w0 = pl.multiple_of(jnp.maximum(q0 - WINDOW, 0), TQ)
    kp = w0 + lax.broadcasted_iota(jnp.int32, (wk, 1), 0)
    wmask = (kp <= pos) & (kp >= pos - WINDOW)
    e, den = _softmax_cols(_dot_nt(kwin_ref[pl.ds(w0, wk), 0:LANES], qr), wmask)
    ow_t = _dot(vtw_ref[:, pl.ds(w0, wk)], e.astype(BF16)) / den

    g_t = g_ref[...].T
    outs = []
    for h in range(N_HEADS):
        g = h // HEADS_PER_KV
        blk_h = lambda o: o[g * HEAD_DIM:(g + 1) * HEAD_DIM, h * TQ:(h + 1) * TQ]
        gate = lambda j: g_t[3 * h + j:3 * h + j + 1, :]
        outs.append(gate(0) * blk_h(oc_t) + gate(1) * blk_h(os_t) + gate(2) * blk_h(ow_t))
    a_ref[...] = jnp.concatenate(outs, axis=0).T


def _nsa_prompt(q, qr, gates, ckv, ksel, kwin, n_seq):
    n = q.shape[0]
    seq = n // n_seq
    nch = ckv.shape[1]
    n_sel = seq // SEL_BLOCK
    assert seq % TQ == 0 and seq >= WINDOW + TQ and n_sel <= LANES and seq % SEL_BLOCK == 0
    tk = min(512, seq)
    assert seq % tk == 0
    nt = seq // TQ
    tile = lambda w: pl.BlockSpec((TQ, w), lambda b, i: (b * nt + i, 0))
    whole = lambda: pl.BlockSpec((seq, KV_W), lambda b, i: (b, 0), pipeline_mode=pl.Buffered(1))
    kern = functools.partial(_nsa_prompt_kernel, tk=tk, n_sel=n_sel, k_top=min(SEL_TOPN, n_sel))
    return pl.pallas_call(
        kern,
        grid=(n_seq, nt),
        in_specs=[tile(QW), tile(QW), tile(LANES),
                  pl.BlockSpec((1, nch, KV_W), lambda b, i: (b, 0, 0)),
                  whole(), whole(), _const_spec((LANES, nch))],
        out_specs=tile(ATTN_W),
        out_shape=jax.ShapeDtypeStruct((n, ATTN_W), F32),
        scratch_shapes=[pltpu.VMEM((LANES, seq), BF16), pltpu.VMEM((LANES, seq), BF16)],
        compiler_params=_params("arbitrary", "arbitrary"),
        name="nsa_prompt",
    )(q, qr, gates, ckv, ksel, kwin, _overlap_t(nch, LANES))


def _softmax_rows2(s_a, mask_a, s_b, mask_b):
    s_a = jnp.where(mask_a, s_a, NEG)
    s_b = jnp.where(mask_b, s_b, NEG)
    mx = jnp.maximum(jnp.max(s_a, axis=1, keepdims=True), jnp.max(s_b, axis=1, keepdims=True))
    e_a = jnp.where(mask_a, jnp.exp(s_a - mx), 0.0)
    e_b = jnp.where(mask_b, jnp.exp(s_b - mx), 0.0)
    den = jnp.sum(e_a, axis=1, keepdims=True) + jnp.sum(e_b, axis=1, keepdims=True)
    return e_a, e_b, jnp.maximum(den, 1e-30)


def _pad_rows(x, rows):
    return jnp.concatenate([x, jnp.zeros((rows - x.shape[0], x.shape[1]), x.dtype)], axis=0)


def _nsa_sample_kernel(*refs, n_pg, t_new, past, n_cb, k_cache):
    pg_refs = refs[1:1 + n_pg]
    (q_ref, qr_ref, g_ref, ckv_ref, seln_ref, cwin_ref, winn_ref, ovl_ref,
     a_ref, swin_ref, qs_ref, oc_ref, ow_ref, m_ref, l_ref, acc_ref) = refs[1 + n_pg:]
    j = pl.program_id(1)
    nr = N_HEADS * t_new
    row_t = lax.rem(lax.broadcasted_iota(jnp.int32, (nr, 1), 0), t_new)
    pos = past + row_t
    lane = lax.broadcasted_iota(jnp.int32, (1, LANES), 1)

    @pl.when(j == 0)
    def _():
        qc = _heads_to_rows(q_ref[0].astype(F32)).astype(BF16)
        qr = _heads_to_rows(qr_ref[0].astype(F32)).astype(BF16)
        nc = ckv_ref.shape[1]
        m_idx = lax.broadcasted_iota(jnp.int32, (1, nc), 1)
        cmask = (m_idx >= 1) & (m_idx * CMP_STRIDE + (CMP_STRIDE - 1) <= pos)
        s = jnp.where(cmask, _dot_nt(qc, ckv_ref[0, :, 0:LANES]), NEG)
        e = jnp.where(cmask, jnp.exp(s - jnp.max(s, axis=1, keepdims=True)), 0.0)
        p_c = e / jnp.maximum(jnp.sum(e, axis=1, keepdims=True), 1e-30)
        oc_ref[...] = _dot(p_c.astype(BF16), ckv_ref[0, :, LANES:2 * LANES])
        pos_t = past + lax.broadcasted_iota(jnp.int32, (t_new, 1), 0)
        biases = []
        for g in range(N_KV_HEADS):
            ps = sum(p_c[(g * HEADS_PER_KV + hh) * t_new:(g * HEADS_PER_KV + hh + 1) * t_new, :]
                     for hh in range(HEADS_PER_KV))
            imp = _split_dot(lambda v: _dot(v, ovl_ref[...]), ps)
            biases.append(_topk_bias(imp, pos_t, n_cb, k_cache, 1))
        bias_rows = jnp.concatenate([biases[h // HEADS_PER_KV] for h in range(N_HEADS)], axis=0)
        qs_ref[...] = jnp.concatenate([qr, bias_rows.astype(BF16)], axis=1)
        cw = cwin_ref[0]
        wn = _pad_rows(winn_ref[0], LANES)
        r_idx = lax.broadcasted_iota(jnp.int32, (1, WINDOW), 1)
        e_a, e_b, den = _softmax_rows2(
            _dot_nt(qr, cw[:, 0:LANES].astype(BF16)), r_idx >= row_t,
            _dot_nt(qr, wn[:, 0:LANES].astype(BF16)), lane <= row_t)
        ow_ref[...] = (_dot(e_a.astype(BF16), cw[:, LANES:].astype(BF16))
                       + _dot(e_b.astype(BF16), wn[:, LANES:].astype(BF16))) / den
        swin_ref[0, 0:WINDOW - t_new, :] = cw[t_new:, :]
        swin_ref[0, WINDOW - t_new:, :] = winn_ref[0]
        m_ref[...] = jnp.full_like(m_ref, NEG)
        l_ref[...] = jnp.zeros_like(l_ref)
        acc_ref[...] = jnp.zeros_like(acc_ref)

    def online(s, v):
        m = m_ref[:, 0:1]
        m_new = jnp.maximum(m, jnp.max(s, axis=1, keepdims=True))
        alpha = jnp.exp(m - m_new)
        p = jnp.exp(s - m_new)
        l_ref[...] = jnp.broadcast_to(alpha * l_ref[:, 0:1] + jnp.sum(p, axis=1, keepdims=True), l_ref.shape)
        acc_ref[...] = alpha * acc_ref[...] + _dot(p.astype(BF16), v)
        m_ref[...] = jnp.broadcast_to(m_new, m_ref.shape)

    x = jnp.concatenate([r[0] for r in pg_refs], axis=0) if n_pg > 1 else pg_refs[0][0]
    nk = x.shape[0]
    blk = lax.shift_right_logical(j * nk + lax.broadcasted_iota(jnp.int32, (nk, 1), 0), 6)
    onehot = jnp.where(blk == lane, 1.0, 0.0).astype(BF16)
    kx = jnp.concatenate([x[:, 0:LANES].astype(BF16), onehot], axis=1)
    online(_dot_nt(qs_ref[...], kx), x[:, LANES:].astype(BF16))

    @pl.when(j == pl.num_programs(1) - 1)
    def _():
        sn = _pad_rows(seln_ref[0], LANES)
        s = _dot_nt(qs_ref[:, 0:LANES], sn[:, 0:LANES].astype(BF16))
        online(jnp.where(lane <= row_t, s, NEG), sn[:, LANES:].astype(BF16))
        os = acc_ref[...] / jnp.maximum(l_ref[:, 0:1], 1e-30)
        gts = g_ref[0]
        outs = []
        for h in range(N_HEADS):
            g = h // HEADS_PER_KV
            blk_h = lambda o: o[h * t_new:(h + 1) * t_new, g * HEAD_DIM:(g + 1) * HEAD_DIM]
            gate = lambda k: gts[:, 3 * h + k:3 * h + k + 1]
            outs.append(gate(0) * blk_h(oc_ref[...]) + gate(1) * blk_h(os) + gate(2) * blk_h(ow_ref[...]))
        a_ref[0] = jnp.concatenate(outs, axis=1)


def _nsa_sample(q, qr, gates, ckv, cache_sel, page_table, sel_new, cache_win, win_new, pages_per_step):
    n_seq, t_new = q.shape[:2]
    n_pages = page_table.shape[1]
    past = n_pages * PAGE_SIZE
    n_cb = past // SEL_BLOCK
    assert cache_win.shape[1] == WINDOW and n_cb <= LANES and t_new <= SEL_BLOCK and t_new % 8 == 0
    k_cache = min(SEL_TOPN, n_cb + 1) - 1
    assert k_cache >= 2
    P = min(pages_per_step, n_pages)
    assert n_pages % P == 0
    nch = ckv.shape[1]
    nr = N_HEADS * t_new
    per_seq = lambda r, w: pl.BlockSpec((1, r, w), lambda b, j, pt: (b, 0, 0))
    page = lambda k: pl.BlockSpec((1, PAGE_SIZE, KV_W), lambda b, j, pt: (pt[b, j * P + k], 0, 0))
    kern = functools.partial(_nsa_sample_kernel, n_pg=P, t_new=t_new, past=past, n_cb=n_cb, k_cache=k_cache)
    return pl.pallas_call(
        kern,
        grid_spec=pltpu.PrefetchScalarGridSpec(
            num_scalar_prefetch=1,
            grid=(n_seq, n_pages // P),
            in_specs=[page(k) for k in range(P)] + [
                per_seq(t_new, QW), per_seq(t_new, QW), per_seq(t_new, LANES), per_seq(nch, KV_W),
                per_seq(t_new, KV_W), per_seq(WINDOW, KV_W), per_seq(t_new, KV_W),
                pl.BlockSpec((nch, LANES), lambda b, j, pt: (0, 0), pipeline_mode=pl.Buffered(1))],
            out_specs=[per_seq(t_new, ATTN_W), per_seq(WINDOW, KV_W)],
            scratch_shapes=[pltpu.VMEM((nr, 2 * LANES), BF16)] + [pltpu.VMEM((nr, LANES), F32)] * 5),
        out_shape=[jax.ShapeDtypeStruct((n_seq, t_new, ATTN_W), F32),
                   jax.ShapeDtypeStruct((n_seq, WINDOW, KV_W), F32)],
        compiler_params=_params("arbitrary", "arbitrary"),
        name="nsa_sample",
    )(page_table, *([cache_sel] * P), q, qr, gates, ckv, sel_new, cache_win, win_new, _overlap_t(nch, LANES).T)


def _pool_windows(ext_ref, n_rows, pos, w_ref, lead):
    outs = []
    for gi, w in enumerate(POOL_WINDOWS):
        lanes = slice(gi * POOL_CH, (gi + 1) * POOL_CH)
        tot = None
        for k in range(w):
            v = ext_ref[lead + (slice(HALO - k, HALO - k + n_rows), lanes)]
            tot = v if tot is None else tot + v
        cnt = jnp.minimum(pos + 1, w).astype(F32)
        d = tot / cnt - ext_ref[lead + (slice(HALO, HALO + n_rows), lanes)]
        outs.append((d, w_ref[gi]))
    return outs


def _pool_prompt_kernel(x_ref, h_ref, w_ref, o_ref, ext_ref, *, tm):
    i = pl.program_id(1)
    ext_ref[0:HALO, :] = jnp.where(i == 0, 0.0, h_ref[...])
    ext_ref[HALO:, :] = x_ref[...]
    pos = i * tm + lax.broadcasted_iota(jnp.int32, (tm, 1), 0)
    res = [_dot(d.astype(BF16), w) for d, w in _pool_windows(ext_ref, tm, pos, w_ref, ())]
    o_ref[...] = jnp.concatenate(res, axis=1)


def _pool_prompt(pool_in, w_pool_b, n_seq, tm):
    n = pool_in.shape[0]
    nt = n // n_seq // tm
    hb = tm // HALO
    return pl.pallas_call(
        functools.partial(_pool_prompt_kernel, tm=tm),
        grid=(n_seq, nt),
        in_specs=[pl.BlockSpec((tm, POOL_W), lambda b, i: (b * nt + i, 0)),
                  pl.BlockSpec((HALO, POOL_W), lambda b, i: (jnp.maximum((b * nt + i) * hb - 1, 0), 0)),
                  _const_spec((N_POOL_GROUPS, POOL_CH, POOL_CH))],
        out_specs=pl.BlockSpec((tm, POOL_W), lambda b, i: (b * nt + i, 0)),
        out_shape=jax.ShapeDtypeStruct((n, POOL_W), F32),
        scratch_shapes=[pltpu.VMEM((HALO + tm, POOL_W), F32)],
        compiler_params=_params("parallel", "arbitrary"),
        name="pool_prompt",
    )(pool_in, pool_in, w_pool_b)


def _pool_sample_kernel(x_ref, st_ref, w_ref, o_ref, ns_ref, ext_ref, *, t_new, past):
    nb = x_ref.shape[0]
    ext_ref[:, HALO - POOL_HIST:HALO, :] = st_ref[...]
    ext_ref[:, HALO:, :] = x_ref[...]
    pos = past + lax.broadcasted_iota(jnp.int32, (1, t_new, 1), 1)
    res = [_dot(d.reshape(nb * t_new, POOL_CH).astype(BF16), w)
           for d, w in _pool_windows(ext_ref, t_new, pos, w_ref, (slice(None),))]
    o_ref[...] = jnp.concatenate(res, axis=1)
    ns_ref[...] = ext_ref[:, HALO + t_new - POOL_HIST:HALO + t_new, :]


def _pool_sample(pool_in, state, w_pool_b, past, nb):
    n_seq, t_new, _ = pool_in.shape
    assert t_new % 8 == 0 and n_seq % nb == 0 and past >= POOL_HIST
    return pl.pallas_call(
        functools.partial(_pool_sample_kernel, t_new=t_new, past=past),
        grid=(n_seq // nb,),
        in_specs=[pl.BlockSpec((nb, t_new, POOL_W), lambda i: (i, 0, 0)),
                  pl.BlockSpec((nb, POOL_HIST, POOL_W), lambda i: (i, 0, 0)),
                  _const_spec((N_POOL_GROUPS, POOL_CH, POOL_CH))],
        out_specs=[pl.BlockSpec((nb * t_new, POOL_W), lambda i: (i, 0)),
                   pl.BlockSpec((nb, POOL_HIST, POOL_W), lambda i: (i, 0, 0))],
        out_shape=[jax.ShapeDtypeStruct((n_seq * t_new, POOL_W), F32),
                   jax.ShapeDtypeStruct((n_seq, POOL_HIST, POOL_W), F32)],
        scratch_shapes=[pltpu.VMEM((nb, HALO + t_new, POOL_W), F32)],
        compiler_params=_params("parallel"),
        name="pool_sample",
    )(pool_in, state, w_pool_b)


def _mix_kernel(x_ref, a_ref, m_ref, ga_ref, gp_ref, wo_ref, g2_ref, wq_ref, x1_ref, qm_ref):
    mix = jnp.concatenate([_rms(a_ref[...], ga_ref[...]), _rms(m_ref[...], gp_ref[...])], axis=1)
    x1 = x_ref[...] + _dot(mix.astype(BF16), wo_ref[...])
    x1_ref[...] = x1
    qm_ref[...] = _dot(_rms(x1, g2_ref[...]).astype(BF16), wq_ref[...]).astype(BF16)


def _mix(x, a, m, ga, gp, w_out_b, g2, w_mq_b, tm):
    n = x.shape[0]
    row = lambda w: pl.BlockSpec((tm, w), lambda i: (i, 0))
    return pl.pallas_call(
        _mix_kernel,
        grid=(n // tm,),
        in_specs=[row(D_MODEL), row(ATTN_W), row(POOL_W), _const_spec((1, ATTN_W)), _const_spec((1, POOL_W)),
                  _const_spec((D_MODEL, D_MODEL)), _const_spec((1, D_MODEL)), _const_spec((D_MODEL, MEM_W))],
        out_specs=[row(D_MODEL), row(MEM_W)],
        out_shape=[jax.ShapeDtypeStruct((n, D_MODEL), F32), jax.ShapeDtypeStruct((n, MEM_W), BF16)],
        compiler_params=_params("parallel"),
        name="mix_out",
    )(x, a, m, ga, gp, w_out_b, g2, w_mq_b)


def _cross_kernel(q_ref, kv_ref, o_ref, *, n_mem_blk, rows):
    for i in range(n_mem_blk):
        outs = []
        for h in range(MEM_HEADS):
            q = q_ref[i * rows:(i + 1) * rows, h * MEM_HEAD_DIM:(h + 1) * MEM_HEAD_DIM]
            k = kv_ref[i, :, h * MEM_HEAD_DIM:(h + 1) * MEM_HEAD_DIM].astype(BF16)
            v = kv_ref[i, :, MEM_W + h * MEM_HEAD_DIM:MEM_W + (h + 1) * MEM_HEAD_DIM].astype(BF16)
            s = _dot_nt(q, k) * (MEM_HEAD_DIM ** -0.5)
            e = jnp.exp(s - jnp.max(s, axis=-1, keepdims=True))
            o = _dot(e.astype(BF16), v) / jnp.sum(e, axis=-1, keepdims=True)
            outs.append(o)
        o_ref[i * rows:(i + 1) * rows, :] = jnp.concatenate(outs, axis=1).astype(BF16)


def _cross(qm, mem_kv, rows, n_mem_blk, tiles_per_mem):
    n = qm.shape[0]
    tm = rows * n_mem_blk
    return pl.pallas_call(
        functools.partial(_cross_kernel, n_mem_blk=n_mem_blk, rows=rows),
        grid=(n // tm,),
        in_specs=[pl.BlockSpec((tm, MEM_W), lambda i: (i, 0)),
                  pl.BlockSpec((n_mem_blk, N_MEM, 2 * MEM_W), lambda i: (i // tiles_per_mem, 0, 0))],
        out_specs=pl.BlockSpec((tm, MEM_W), lambda i: (i, 0)),
        out_shape=jax.ShapeDtypeStruct((n, MEM_W), BF16),
        compiler_params=_params("parallel"),
        name="cross_attn",
    )(qm, mem_kv)


def _ffn_kernel(x1_ref, o_ref, wmo_ref, g3_ref, wup_ref, wdn_ref, gf_ref, y_ref):
    x2 = x1_ref[...] + _dot(o_ref[...], wmo_ref[...])
    xn = _rms(x2, g3_ref[...]).astype(BF16)
    acc = x2
    for c in range(D_FF // FF_CHUNK):
        h = jnp.maximum(_dot(xn, wup_ref[:, c * FF_CHUNK:(c + 1) * FF_CHUNK]), 0.0)
        acc = acc + _dot((h * h).astype(BF16), wdn_ref[c * FF_CHUNK:(c + 1) * FF_CHUNK, :])
    y_ref[...] = _rms(acc, gf_ref[...])


def _ffn(x1, o, w_mo_b, g3, w_up_b, w_down_b, g_final, tm):
    n = x1.shape[0]
    row = lambda w: pl.BlockSpec((tm, w), lambda i: (i, 0))
    return pl.pallas_call(
        _ffn_kernel,
        grid=(n // tm,),
        in_specs=[row(D_MODEL), row(MEM_W), _const_spec((MEM_W, D_MODEL)), _const_spec((1, D_MODEL)),
                  _const_spec((D_MODEL, D_FF)), _const_spec((D_FF, D_MODEL)), _const_spec((1, D_MODEL))],
        out_specs=row(D_MODEL),
        out_shape=jax.ShapeDtypeStruct((n, D_MODEL), F32),
        compiler_params=_params("parallel"),
        name="ffn",
    )(x1, o, w_mo_b, g3, w_up_b, w_down_b, g_final)


def _tile(n, pref):
    t = min(n, pref)
    assert n % t == 0
    return t


def kernel(x_prompt, x_sample, cache_cmp_kv, cache_sel_kv, cache_win_kv, state_pool, cache_mem_kv, page_table, mem_prompt, w_in, pe_cmp, w_cmp1, w_cmp2, w_pool, g_attn_out, pool_scale, w_out, g_norm1, g_norm2, g_mem, w_mq, w_mkv, w_mo, g_norm3, w_up, w_down, g_final):
    assert w_in.shape[0] == 1, "single-layer trunk"
    B, S, _ = x_prompt.shape
    DB, T, _ = x_sample.shape
    n_phys = cache_sel_kv.shape[1]
    past = page_table.shape[1] * PAGE_SIZE
    kv5 = (2, N_KV_HEADS, HEAD_DIM)
    r2 = lambda v: v.reshape(1, -1)

    w_in_p = _permute_w_in(w_in[0])
    w_out_b, w_mq_b, w_mkv_b, w_mo_b = (w[0].astype(BF16) for w in (w_out, w_mq, w_mkv, w_mo))
    w_up_b, w_down_b, w_pool_b = w_up[0].astype(BF16), w_down[0].astype(BF16), w_pool[0].astype(BF16)
    g1, g2, g3, gm, ga, gp, gf = (r2(v) for v in (g_norm1[0], g_norm2[0], g_norm3[0], g_mem[0], g_attn_out[0],
                                                  pool_scale[0], g_final))
    cw = _compress_weights(pe_cmp[0], w_cmp1[0], w_cmp2[0])

    np_tok = B * S
    tm_p = _tile(S, 512)
    xp = x_prompt.reshape(np_tok, D_MODEL)
    q, qr, cmp_kv, sel_kv, win_kv, sel_b, win_b, gates, pool_in = _in_proj(
        xp, g1, w_in_p, jnp.tile(jnp.arange(S), B), tm_p)
    mem_f, mem_b = _mem_kv(mem_prompt.reshape(B * N_MEM, D_MODEL), gm, w_mkv_b, _tile(B * N_MEM, 256))
    ckv = _compress_prompt(cmp_kv, B, cw)
    a = _nsa_prompt(q, qr, gates, ckv, sel_b, win_b, B)
    m = _pool_prompt(pool_in, w_pool_b, B, tm_p)
    x1, qm = _mix(xp, a, m, ga, gp, w_out_b, g2, w_mq_b, tm_p)
    o = _cross(qm, mem_b.reshape(B, N_MEM, 2 * MEM_W), tm_p, 1, S // tm_p)
    y_p = _ffn(x1, o, w_mo_b, g3, w_up_b, w_down_b, gf, tm_p)
    p_state = (cmp_kv.reshape((1, B, S) + kv5), sel_kv.reshape((1, B, S) + kv5),
               win_kv.reshape((B, S) + kv5)[None, :, -min(WINDOW, S):],
               pool_in.reshape(B, S, POOL_W)[None, :, -POOL_HIST:],
               mem_f.reshape(1, B, N_MEM, 2, MEM_HEADS, MEM_HEAD_DIM))

    ns_tok = DB * T
    tm_s = _tile(ns_tok, 512)
    xs = x_sample.reshape(ns_tok, D_MODEL)
    q, qr, cmp_kv, sel_kv, win_kv, _, _, gates, pool_in = _in_proj(
        xs, g1, w_in_p, jnp.tile(past + jnp.arange(T), DB), tm_s)
    r3 = lambda v: v.reshape(DB, T, v.shape[-1])
    ckv = _compress_paged(cache_cmp_kv[0], page_table, cw, 16)
    a, s_win = _nsa_sample(r3(q), r3(qr), r3(gates), ckv, cache_sel_kv[0].reshape(n_phys, PAGE_SIZE, KV_W), page_table,
                           r3(sel_kv), cache_win_kv[0].reshape(DB, WINDOW, KV_W), r3(win_kv), 16)
    m, s_pool = _pool_sample(r3(pool_in), state_pool[0], w_pool_b, past, _tile(DB, 16))
    x1, qm = _mix(xs, a.reshape(ns_tok, ATTN_W), m, ga, gp, w_out_b, g2, w_mq_b, tm_s)
    o = _cross(qm, cache_mem_kv[0].reshape(DB, N_MEM, 2 * MEM_W), T, _tile(DB, 4), 1)
    y_s = _ffn(x1, o, w_mo_b, g3, w_up_b, w_down_b, gf, tm_s)
    s_state = (cmp_kv.reshape((1, DB, T) + kv5), sel_kv.reshape((1, DB, T) + kv5),
               s_win.reshape((1, DB, WINDOW) + kv5), s_pool[None])

    return (y_p.reshape(B, S, D_MODEL), y_s.reshape(DB, T, D_MODEL)) + p_state + s_state
```

```python
import functools

import jax
import jax.numpy as jnp
from jax import lax
from jax.experimental import pallas as pl
from jax.experimental.pallas import tpu as pltpu

D_MODEL = 1024
PAGE_SIZE = 128
ATTN_W = 512
POOL_W = 512
HEAD_DIM = 64
N_HEADS = 8
N_KV_HEADS = 2
HEADS_PER_KV = N_HEADS // N_KV_HEADS
SCALE = HEAD_DIM ** -0.5
ROT_DIM = HEAD_DIM // 4
ROPE_THETA = 500000.0
CMP_LEN = 32
CMP_STRIDE = 16
CMP_HID = HEAD_DIM
SEL_BLOCK = 64
SEL_TOPN = 16
WINDOW = 512
FORCE_BONUS = 1000.0
POOL_WINDOWS = (2, 4, 8, 16)
N_POOL_GROUPS = len(POOL_WINDOWS)
POOL_CH = POOL_W // N_POOL_GROUPS
POOL_HIST = max(POOL_WINDOWS) - 1
N_MEM = 256
MEM_HEADS = 4
MEM_HEAD_DIM = 128
MEM_W = MEM_HEADS * MEM_HEAD_DIM
D_FF = 4 * D_MODEL
EPS = 1e-6
KV_W = 2 * N_KV_HEADS * HEAD_DIM
N_GATES = 3 * N_HEADS
LANES = 128
VMEM_LIMIT = 56 * 1024 * 1024
QW = N_HEADS * LANES
TQ = 128
NEG = -1e30
CHUNK_W = CMP_STRIDE * KV_W
CMP_R = CMP_LEN // CMP_STRIDE
CHUNKS_PER_PAGE = PAGE_SIZE // CMP_STRIDE
HALO = 16
FF_CHUNK = 1024

F32 = jnp.float32
BF16 = jnp.bfloat16


def _rms(x, g):
    return x * lax.rsqrt(jnp.mean(x * x, axis=-1, keepdims=True) + EPS) * g


def _dot(a, b):
    return jnp.dot(a, b, preferred_element_type=F32)


def _dot_nt(a, b):
    return lax.dot_general(a, b, (((1,), (1,)), ((), ())), preferred_element_type=F32)


def _const_spec(shape):
    n = len(shape)
    return pl.BlockSpec(shape, lambda *_: (0,) * n, pipeline_mode=pl.Buffered(1))


def _params(*sem):
    return pltpu.CompilerParams(dimension_semantics=sem, vmem_limit_bytes=VMEM_LIMIT)


X_Q, X_CMP, X_SEL, X_WIN, X_POOL, X_GATE = 0, 1024, 1280, 1536, 1792, 2304
X_IN = X_GATE + LANES


def _permute_w_in(w_in):
    q, kc, ks, kw, gl, pool = jnp.split(w_in, (512, 768, 1024, 1280, 1280 + N_GATES), axis=1)
    z = jnp.zeros((D_MODEL, HEAD_DIM), w_in.dtype)
    qx = []
    for h in range(N_HEADS):
        qh = q[:, h * HEAD_DIM:(h + 1) * HEAD_DIM]
        qx += [qh, z] if h // HEADS_PER_KV == 0 else [z, qh]
    pad = jnp.zeros((D_MODEL, LANES - N_GATES), w_in.dtype)
    return jnp.concatenate(qx + [kc, ks, kw, pool, gl, pad], axis=1).astype(BF16)


def _rope_tables(pos):
    half = ROT_DIM // 2
    inv = ROPE_THETA ** (-jnp.arange(half, dtype=F32) / half)
    ang = pos.astype(F32)[:, None] * inv[None, :]
    cos, sin = jnp.cos(ang), jnp.sin(ang)
    z = jnp.zeros_like(cos)
    rest0 = jnp.zeros((pos.shape[0], HEAD_DIM - ROT_DIM), F32)
    cs = jnp.concatenate([cos, cos, rest0 + 1.0], axis=1)
    sa = jnp.concatenate([-sin, z, rest0], axis=1)
    sb = jnp.concatenate([z, sin, rest0], axis=1)
    return tuple(jnp.tile(t, (1, LANES // HEAD_DIM)) for t in (cs, sa, sb))


def _rope(v, cs, sa, sb):
    n = v.shape[1] // LANES
    w = v.shape[1]
    cs, sa, sb = (jnp.tile(t, (1, n)) for t in (cs, sa, sb))
    return v * cs + pltpu.roll(v, w - ROT_DIM // 2, 1) * sa + pltpu.roll(v, ROT_DIM // 2, 1) * sb


def _in_proj_kernel(x_ref, g_ref, w_ref, cs_ref, sa_ref, sb_ref,
                    q_ref, qr_ref, cmp_ref, sel_ref, win_ref, selb_ref, winb_ref, gate_ref, pool_ref):
    xb = _rms(x_ref[...], g_ref[...]).astype(BF16)
    cs, sa, sb = cs_ref[...], sa_ref[...], sb_ref[...]
    q = _dot(xb, w_ref[:, X_Q:X_CMP])
    q_ref[...] = (q * SCALE).astype(BF16)
    qr_ref[...] = (_rope(q, cs, sa, sb) * SCALE).astype(BF16)
    cmp_ref[...] = _dot(xb, w_ref[:, X_CMP:X_SEL])
    for lo, f_ref, b_ref in ((X_SEL, sel_ref, selb_ref), (X_WIN, win_ref, winb_ref)):
        kv = _dot(xb, w_ref[:, lo:lo + KV_W])
        kv = jnp.concatenate([_rope(kv[:, :LANES], cs, sa, sb), kv[:, LANES:]], axis=1)
        f_ref[...] = kv
        b_ref[...] = kv.astype(BF16)
    pool_ref[...] = _dot(xb, w_ref[:, X_POOL:X_GATE])
    gate_ref[...] = jax.nn.sigmoid(_dot(xb, w_ref[:, X_GATE:X_IN]))


def _in_proj(x, g1, w_in_p, pos_rows, tm):
    n = x.shape[0]
    cs, sa, sb = _rope_tables(pos_rows)
    row = lambda w: pl.BlockSpec((tm, w), lambda i: (i, 0))
    sds = lambda w, dt: jax.ShapeDtypeStruct((n, w), dt)
    return pl.pallas_call(
        _in_proj_kernel,
        grid=(n // tm,),
        in_specs=[row(D_MODEL), _const_spec((1, D_MODEL)), _const_spec((D_MODEL, X_IN)),
                  row(LANES), row(LANES), row(LANES)],
        out_specs=[row(QW), row(QW), row(KV_W), row(KV_W), row(KV_W), row(KV_W), row(KV_W), row(LANES), row(POOL_W)],
        out_shape=[sds(QW, BF16), sds(QW, BF16), sds(KV_W, F32), sds(KV_W, F32), sds(KV_W, F32),
                   sds(KV_W, BF16), sds(KV_W, BF16), sds(LANES, F32), sds(POOL_W, F32)],
        compiler_params=_params("parallel"),
        name="in_proj",
    )(x, g1, w_in_p, cs, sa, sb)


def _mem_kv_kernel(m_ref, g_ref, w_ref, o_ref, ob_ref):
    kv = _dot(_rms(m_ref[...], g_ref[...]).astype(BF16), w_ref[...])
    o_ref[...] = kv
    ob_ref[...] = kv.astype(BF16)


def _mem_kv(mem, g_mem, w_mkv_b, tm):
    n = mem.shape[0]
    row = lambda w: pl.BlockSpec((tm, w), lambda i: (i, 0))
    return pl.pallas_call(
        _mem_kv_kernel,
        grid=(n // tm,),
        in_specs=[row(D_MODEL), _const_spec((1, D_MODEL)), _const_spec((D_MODEL, 2 * MEM_W))],
        out_specs=[row(2 * MEM_W), row(2 * MEM_W)],
        out_shape=[jax.ShapeDtypeStruct((n, 2 * MEM_W), F32), jax.ShapeDtypeStruct((n, 2 * MEM_W), BF16)],
        compiler_params=_params("parallel"),
        name="mem_kv",
    )(mem, g_mem, w_mkv_b)


def _compress_weights(pe, w1, w2):
    w1r = w1.reshape(2, CMP_R, CMP_STRIDE, HEAD_DIM, CMP_HID)
    eye_c = jnp.eye(2, dtype=w1.dtype)
    eye_g = jnp.eye(N_KV_HEADS, dtype=w1.dtype)
    w1big = jnp.einsum('crsdh,cx,gy->scgdrxyh', w1r, eye_c, eye_g).reshape(CHUNK_W, CMP_R * KV_W)
    pe_term = jnp.einsum('cld,cldh->ch', pe, w1.reshape(2, CMP_LEN, HEAD_DIM, CMP_HID))
    bias = jnp.broadcast_to(pe_term[:, None, :], (2, N_KV_HEADS, CMP_HID)).reshape(1, KV_W)
    w2big = jnp.einsum('chd,cx,gy->cghxyd', w2, eye_c, eye_g).reshape(KV_W, KV_W)
    return w1big.astype(BF16), bias, w2big.astype(BF16)


def _compress_kernel(*refs, n_in, paged):
    if paged:
        refs = refs[1:]
    x_refs = refs[:n_in]
    w1_ref, b_ref, w2_ref, o_ref, carry_ref = refs[n_in:]

    @pl.when(pl.program_id(1) == 0)
    def _():
        carry_ref[...] = jnp.zeros_like(carry_ref)

    if paged:
        r_i = lax.broadcasted_iota(jnp.int32, (PAGE_SIZE, PAGE_SIZE), 0)
        t_i = lax.broadcasted_iota(jnp.int32, (PAGE_SIZE, PAGE_SIZE), 1)
        src = CMP_STRIDE * (r_i & (CHUNKS_PER_PAGE - 1)) + lax.shift_right_logical(r_i, 3)
        pick = jnp.where(t_i == src, 1.0, 0.0).astype(BF16)
        pages = [_dot_nt(pick, r[0].astype(BF16)) for r in x_refs]
        cols = [jnp.concatenate([pg[s * CHUNKS_PER_PAGE:(s + 1) * CHUNKS_PER_PAGE, :] for pg in pages], axis=0)
                for s in range(CMP_STRIDE)]
        x = jnp.concatenate(cols, axis=1)
    else:
        x = x_refs[0][0]
    u = _dot(x.astype(BF16), w1_ref[...])
    u0, u1 = u[:, :KV_W], u[:, KV_W:]
    rows = u0.shape[0]
    row = lax.broadcasted_iota(jnp.int32, u0.shape, 0)
    prev = jnp.where(row == 0, carry_ref[0:1, :], pltpu.roll(u0, 1, 0))
    carry_ref[...] = jnp.broadcast_to(u0[rows - 1:rows, :], carry_ref.shape)
    h = prev + u1 + b_ref[...]
    o_ref[0] = _dot(jax.nn.gelu(h).astype(BF16), w2_ref[...]).astype(BF16)


def _compress_call(n_seq, nch, rows, n_in, paged, x_spec_fn):
    grid = (n_seq, nch // rows)
    in_specs = [x_spec_fn(k) for k in range(n_in)] + [
        _const_spec((CHUNK_W, CMP_R * KV_W)), _const_spec((1, KV_W)), _const_spec((KV_W, KV_W))]
    out_spec = pl.BlockSpec((1, rows, KV_W), (lambda b, i, *_: (b, i, 0)))
    kern = functools.partial(_compress_kernel, n_in=n_in, paged=paged)
    common = dict(out_shape=jax.ShapeDtypeStruct((n_seq, nch, KV_W), BF16),
                  compiler_params=_params("parallel", "arbitrary"), name="compress")
    scratch = [pltpu.VMEM((8, KV_W), F32)]
    if paged:
        return pl.pallas_call(kern, grid_spec=pltpu.PrefetchScalarGridSpec(
            num_scalar_prefetch=1, grid=grid, in_specs=in_specs, out_specs=out_spec, scratch_shapes=scratch), **common)
    return pl.pallas_call(kern, grid=grid, in_specs=in_specs, out_specs=out_spec, scratch_shapes=scratch, **common)


def _compress_prompt(cmp_kv, n_seq, cw):
    nch = cmp_kv.shape[0] // n_seq // CMP_STRIDE
    rows = min(nch, 128)
    x = cmp_kv.reshape(n_seq, nch, CHUNK_W)
    spec = lambda k: pl.BlockSpec((1, rows, CHUNK_W), lambda b, i: (b, i, 0))
    return _compress_call(n_seq, nch, rows, 1, False, spec)(x, *cw)


def _pages_t(cache):
    return cache.transpose(0, 2, 3, 4, 1).reshape(cache.shape[0], KV_W, cache.shape[1])


def _compress_paged(cache_t, page_table, cw, pages_per_step):
    n_seq, n_pages = page_table.shape
    P = min(pages_per_step, n_pages)
    assert n_pages % P == 0
    spec = lambda k: pl.BlockSpec((1, KV_W, PAGE_SIZE), lambda b, i, pt: (pt[b, i * P + k], 0, 0))
    call = _compress_call(n_seq, n_pages * CHUNKS_PER_PAGE, P * CHUNKS_PER_PAGE, P, True, spec)
    return call(page_table, *([cache_t] * P), *cw)


def _overlap_t(nch, n_sel_pad):
    m = jnp.arange(nch)[None, :]
    j = jnp.arange(n_sel_pad)[:, None]
    lo = (m - 1) * CMP_STRIDE
    ov = jnp.maximum(jnp.minimum(lo + CMP_LEN, (j + 1) * SEL_BLOCK) - jnp.maximum(lo, j * SEL_BLOCK), 0)
    return jnp.where(m >= 1, ov.astype(F32) / CMP_LEN, 0.0).astype(BF16)


def _heads_to_rows(x):
    return jnp.concatenate([x[:, h * LANES:(h + 1) * LANES] for h in range(N_HEADS)], axis=0)


def _topk_bias(imp, pos, n_blk, k_top, axis):
    j = lax.broadcasted_iota(jnp.int32, imp.shape, axis)
    cur = lax.shift_right_logical(pos, 6)
    forced = (j == 0) | (j == cur) | (j == cur - 1)
    score = jnp.where(j * SEL_BLOCK <= pos, imp + jnp.where(forced, FORCE_BONUS, 0.0), -1e9)
    score = jnp.where(j < n_blk, score, -jnp.inf)
    bias = jnp.full(imp.shape, NEG, F32)
    for _ in range(k_top):
        mx = jnp.max(score, axis=axis, keepdims=True)
        idx = jnp.min(jnp.where(score == mx, j, LANES), axis=axis, keepdims=True)
        hit = j == idx
        bias = jnp.where(hit, 0.0, bias)
        score = jnp.where(hit, -jnp.inf, score)
    return bias


def _split_dot(dot, ps):
    hi = ps.astype(BF16)
    lo = (ps - hi.astype(F32)).astype(BF16)
    return dot(hi) + dot(lo)


def _softmax_cols(s, mask):
    s = jnp.where(mask, s, NEG)
    e = jnp.where(mask, jnp.exp(s - jnp.max(s, axis=0, keepdims=True)), 0.0)
    return e, jnp.maximum(jnp.sum(e, axis=0, keepdims=True), 1e-30)


def _nsa_prompt_kernel(q_ref, qr_ref, g_ref, ckv_ref, ksel_ref, kwin_ref, ovt_ref, a_ref, vts_ref, vtw_ref,
                       *, tk, n_sel, k_top):
    i = pl.program_id(1)
    q0 = pl.multiple_of(i * TQ, TQ)

    @pl.when(i == 0)
    def _():
        vts_ref[...] = jnp.zeros_like(vts_ref)
        vtw_ref[...] = jnp.zeros_like(vtw_ref)

    for k_ref, vt_ref in ((ksel_ref, vts_ref), (kwin_ref, vtw_ref)):
        v = k_ref[pl.ds(q0, TQ), LANES:2 * LANES].astype(F32)
        vt_ref[:, pl.ds(q0, TQ)] = v.T.astype(BF16)

    nl = N_HEADS * TQ
    lane = lax.broadcasted_iota(jnp.int32, (1, nl), 1)
    pos = q0 + (lane & (TQ - 1))
    qc = _heads_to_rows(q_ref)
    qr = _heads_to_rows(qr_ref)

    nc = ckv_ref.shape[1]
    ck = ckv_ref[0, :, 0:LANES]
    cv_t = ckv_ref[0, :, LANES:2 * LANES].astype(F32).T.astype(BF16)
    m_idx = lax.broadcasted_iota(jnp.int32, (nc, 1), 0)
    cmask = (m_idx >= 1) & (m_idx * CMP_STRIDE + (CMP_STRIDE - 1) <= pos)
    e, den = _softmax_cols(_dot_nt(ck, qc), cmask)
    p_c = e / den
    oc_t = _dot(cv_t, p_c.astype(BF16))

    pos_t = q0 + lax.broadcasted_iota(jnp.int32, (1, TQ), 1)
    biases = []
    for g in range(N_KV_HEADS):
        ps = sum(p_c[:, (g * HEADS_PER_KV + hh) * TQ:(g * HEADS_PER_KV + hh + 1) * TQ] for hh in range(HEADS_PER_KV))
        imp_t = _split_dot(lambda v: _dot(ovt_ref[...], v), ps)
        biases.append(_topk_bias(imp_t, pos_t, n_sel, k_top, 0).T.astype(BF16))
    bias_rows = jnp.concatenate([biases[h // HEADS_PER_KV] for h in range(N_HEADS)], axis=0)
    qs = jnp.concatenate([qr, bias_rows], axis=1)

    def sel_chunk(c0, masked, carry):
        m, l, acc = carry
        key = c0 + lax.broadcasted_iota(jnp.int32, (tk, 1), 0)
        blk = lax.shift_right_logical(key, 6)
        onehot = jnp.where(blk == lax.broadcasted_iota(jnp.int32, (tk, LANES), 1), 1.0, 0.0).astype(BF16)
        kx = jnp.concatenate([ksel_ref[pl.ds(c0, tk), 0:LANES], onehot], axis=1)
        s = _dot_nt(kx, qs)
        if masked:
            s = jnp.where(key <= pos, s, NEG)
        m_new = jnp.maximum(m, jnp.max(s, axis=0, keepdims=True))
        alpha = jnp.exp(m - m_new)
        p = jnp.exp(s - m_new)
        l = alpha * l + jnp.sum(p, axis=0, keepdims=True)
        acc = alpha * acc + _dot(vts_ref[:, pl.ds(c0, tk)], p.astype(BF16))
        return m_new, l, acc

    n_full = q0 // tk
    carry = (jnp.full((1, nl), NEG, F32), jnp.zeros((1, nl), F32), jnp.zeros((LANES, nl), F32))
    carry = lax.fori_loop(0, n_full, lambda c, cr: sel_chunk(pl.multiple_of(c * tk, tk), False, cr), carry)
    _, l, acc = sel_chunk(pl.multiple_of(n_full * tk, tk), True, carry)
    os_t = acc / jnp.maximum(l, 1e-30)

    wk = WINDOW + TQ
    w0 = pl.multiple_of(jnp.maximum(q0 - WINDOW, 0), TQ)
    kp = w0 + lax.broadcasted_iota(jnp.int32, (wk, 1), 0)
    wmask = (kp <= pos) & (kp >= pos - WINDOW)
    e, den = _softmax_cols(_dot_nt(kwin_ref[pl.ds(w0, wk), 0:LANES], qr), wmask)
    ow_t = _dot(vtw_ref[:, pl.ds(w0, wk)], e.astype(BF16)) / den

    g_t = g_ref[...].T
    outs = []
    for h in range(N_HEADS):
        g = h // HEADS_PER_KV
        blk_h = lambda o: o[g * HEAD_DIM:(g + 1) * HEAD_DIM, h * TQ:(h + 1) * TQ]
        gate = lambda j: g_t[3 * h + j:3 * h + j + 1, :]
        outs.append(gate(0) * blk_h(oc_t) + gate(1) * blk_h(os_t) + gate(2) * blk_h(ow_t))
    a_ref[...] = jnp.concatenate(outs, axis=0).T


def _nsa_prompt(q, qr, gates, ckv, ksel, kwin, n_seq):
    n = q.shape[0]
    seq = n // n_seq
    nch = ckv.shape[1]
    n_sel = seq // SEL_BLOCK
    assert seq % TQ == 0 and seq >= WINDOW + TQ and n_sel <= LANES and seq % SEL_BLOCK == 0
    tk = min(512, seq)
    assert seq % tk == 0
    nt = seq // TQ
    tile = lambda w: pl.BlockSpec((TQ, w), lambda b, i: (b * nt + i, 0))
    whole = lambda: pl.BlockSpec((seq, KV_W), lambda b, i: (b, 0), pipeline_mode=pl.Buffered(1))
    kern = functools.partial(_nsa_prompt_kernel, tk=tk, n_sel=n_sel, k_top=min(SEL_TOPN, n_sel))
    return pl.pallas_call(
        kern,
        grid=(n_seq, nt),
        in_specs=[tile(QW), tile(QW), tile(LANES),
                  pl.BlockSpec((1, nch, KV_W), lambda b, i: (b, 0, 0)),
                  whole(), whole(), _const_spec((LANES, nch))],
        out_specs=tile(ATTN_W),
        out_shape=jax.ShapeDtypeStruct((n, ATTN_W), F32),
        scratch_shapes=[pltpu.VMEM((LANES, seq), BF16), pltpu.VMEM((LANES, seq), BF16)],
        compiler_params=_params("arbitrary", "arbitrary"),
        name="nsa_prompt",
    )(q, qr, gates, ckv, ksel, kwin, _overlap_t(nch, LANES))


def _softmax_rows2(s_a, mask_a, s_b, mask_b):
    s_a = jnp.where(mask_a, s_a, NEG)
    s_b = jnp.where(mask_b, s_b, NEG)
    mx = jnp.maximum(jnp.max(s_a, axis=1, keepdims=True), jnp.max(s_b, axis=1, keepdims=True))
    e_a = jnp.where(mask_a, jnp.exp(s_a - mx), 0.0)
    e_b = jnp.where(mask_b, jnp.exp(s_b - mx), 0.0)
    den = jnp.sum(e_a, axis=1, keepdims=True) + jnp.sum(e_b, axis=1, keepdims=True)
    return e_a, e_b, jnp.maximum(den, 1e-30)


def _pad_rows(x, rows):
    return jnp.concatenate([x, jnp.zeros((rows - x.shape[0], x.shape[1]), x.dtype)], axis=0)


def _nsa_sample_kernel(*refs, n_pg, t_new, past, n_cb, k_cache):
    pg_refs = refs[1:1 + n_pg]
    (q_ref, qr_ref, g_ref, ckv_ref, seln_ref, cwin_ref, winn_ref, ovl_ref,
     a_ref, swin_ref, qs_ref, oc_ref, ow_ref, m_ref, l_ref, acc_ref) = refs[1 + n_pg:]
    j = pl.program_id(1)
    nr = N_HEADS * t_new
    row_t = lax.rem(lax.broadcasted_iota(jnp.int32, (nr, 1), 0), t_new)
    pos = past + row_t
    lane = lax.broadcasted_iota(jnp.int32, (1, LANES), 1)

    @pl.when(j == 0)
    def _():
        qc = _heads_to_rows(q_ref[0].astype(F32)).astype(BF16)
        qr = _heads_to_rows(qr_ref[0].astype(F32)).astype(BF16)
        nc = ckv_ref.shape[1]
        m_idx = lax.broadcasted_iota(jnp.int32, (1, nc), 1)
        cmask = (m_idx >= 1) & (m_idx * CMP_STRIDE + (CMP_STRIDE - 1) <= pos)
        s = jnp.where(cmask, _dot_nt(qc, ckv_ref[0, :, 0:LANES]), NEG)
        e = jnp.where(cmask, jnp.exp(s - jnp.max(s, axis=1, keepdims=True)), 0.0)
        p_c = e / jnp.maximum(jnp.sum(e, axis=1, keepdims=True), 1e-30)
        oc_ref[...] = _dot(p_c.astype(BF16), ckv_ref[0, :, LANES:2 * LANES])
        imps = []
        for g in range(N_KV_HEADS):
            ps = sum(p_c[(g * HEADS_PER_KV + hh) * t_new:(g * HEADS_PER_KV + hh + 1) * t_new, :]
                     for hh in range(HEADS_PER_KV))
            imps.append(_split_dot(lambda v: _dot(v, ovl_ref[...]), ps))
        imp_t = _pad_rows(jnp.concatenate(imps, axis=0), LANES).T
        bias = _topk_bias(imp_t, past + lax.rem(lane, t_new), n_cb, k_cache, 0).T
        biases = [bias[g * t_new:(g + 1) * t_new, :] for g in range(N_KV_HEADS)]
        bias_rows = jnp.concatenate([biases[h // HEADS_PER_KV] for h in range(N_HEADS)], axis=0)
        qs_ref[...] = jnp.concatenate([qr, bias_rows.astype(BF16)], axis=1)
        cw = jnp.concatenate([cwin_ref[0, c * LANES:(c + 1) * LANES, :].T for c in range(KV_W // LANES)], axis=1)
        wn = _pad_rows(winn_ref[0], LANES)
        r_idx = lax.broadcasted_iota(jnp.int32, (1, WINDOW), 1)
        e_a, e_b, den = _softmax_rows2(
            _dot_nt(qr, cw[:, 0:LANES].astype(BF16)), r_idx >= row_t,
            _dot_nt(qr, wn[:, 0:LANES].astype(BF16)), lane <= row_t)
        ow_ref[...] = (_dot(e_a.astype(BF16), cw[:, LANES:].astype(BF16))
                       + _dot(e_b.astype(BF16), wn[:, LANES:].astype(BF16))) / den
        swin_ref[0, 0:WINDOW - t_new, :] = cw[t_new:, :]
        swin_ref[0, WINDOW - t_new:, :] = winn_ref[0]
        m_ref[...] = jnp.full_like(m_ref, NEG)
        l_ref[...] = jnp.zeros_like(l_ref)
        acc_ref[...] = jnp.zeros_like(acc_ref)

    def online(s, pv):
        m = m_ref[:, 0:1]
        m_new = jnp.maximum(m, jnp.max(s, axis=1, keepdims=True))
        alpha = jnp.exp(m - m_new)
        p = jnp.exp(s - m_new)
        l_ref[...] = jnp.broadcast_to(alpha * l_ref[:, 0:1] + jnp.sum(p, axis=1, keepdims=True), l_ref.shape)
        acc_ref[...] = alpha * acc_ref[...] + pv(p.astype(BF16))
        m_ref[...] = jnp.broadcast_to(m_new, m_ref.shape)

    cat = lambda lo: jnp.concatenate([r[0, lo:lo + LANES, :] for r in pg_refs], axis=1).astype(BF16)
    k_t, v_t = cat(0), cat(LANES)
    nk = k_t.shape[1]
    blk = lax.shift_right_logical(j * nk + lax.broadcasted_iota(jnp.int32, (1, nk), 1), 6)
    onehot_t = jnp.where(blk == lax.broadcasted_iota(jnp.int32, (LANES, 1), 0), 1.0, 0.0).astype(BF16)
    online(_dot(qs_ref[...], jnp.concatenate([k_t, onehot_t], axis=0)), lambda p: _dot_nt(p, v_t))

    @pl.when(j == pl.num_programs(1) - 1)
    def _():
        sn = _pad_rows(seln_ref[0], LANES)
        s = _dot_nt(qs_ref[:, 0:LANES], sn[:, 0:LANES].astype(BF16))
        online(jnp.where(lane <= row_t, s, NEG), lambda p: _dot(p, sn[:, LANES:].astype(BF16)))
        os = acc_ref[...] / jnp.maximum(l_ref[:, 0:1], 1e-30)
        gts = g_ref[0]
        outs = []
        for h in range(N_HEADS):
            g = h // HEADS_PER_KV
            blk_h = lambda o: o[h * t_new:(h + 1) * t_new, g * HEAD_DIM:(g + 1) * HEAD_DIM]
            gate = lambda k: gts[:, 3 * h + k:3 * h + k + 1]
            outs.append(gate(0) * blk_h(oc_ref[...]) + gate(1) * blk_h(os) + gate(2) * blk_h(ow_ref[...]))
        a_ref[0] = jnp.concatenate(outs, axis=1)


def _nsa_sample(q, qr, gates, ckv, cache_sel, page_table, sel_new, cache_win, win_new, pages_per_step):
    n_seq, t_new = q.shape[:2]
    n_pages = page_table.shape[1]
    past = n_pages * PAGE_SIZE
    n_cb = past // SEL_BLOCK
    assert cache_win.shape[2] == WINDOW and n_cb <= LANES and t_new <= SEL_BLOCK and t_new % 8 == 0
    k_cache = min(SEL_TOPN, n_cb + 1) - 1
    assert k_cache >= 2
    P = min(pages_per_step, n_pages)
    assert n_pages % P == 0
    nch = ckv.shape[1]
    nr = N_HEADS * t_new
    per_seq = lambda r, w: pl.BlockSpec((1, r, w), lambda b, j, pt: (b, 0, 0))
    page = lambda k: pl.BlockSpec((1, KV_W, PAGE_SIZE), lambda b, j, pt: (pt[b, j * P + k], 0, 0))
    kern = functools.partial(_nsa_sample_kernel, n_pg=P, t_new=t_new, past=past, n_cb=n_cb, k_cache=k_cache)
    return pl.pallas_call(
        kern,
        grid_spec=pltpu.PrefetchScalarGridSpec(
            num_scalar_prefetch=1,
            grid=(n_seq, n_pages // P),
            in_specs=[page(k) for k in range(P)] + [
                per_seq(t_new, QW), per_seq(t_new, QW), per_seq(t_new, LANES), per_seq(nch, KV_W),
                per_seq(t_new, KV_W), per_seq(KV_W, WINDOW), per_seq(t_new, KV_W),
                pl.BlockSpec((nch, LANES), lambda b, j, pt: (0, 0), pipeline_mode=pl.Buffered(1))],
            out_specs=[per_seq(t_new, ATTN_W), per_seq(WINDOW, KV_W)],
            scratch_shapes=[pltpu.VMEM((nr, 2 * LANES), BF16)] + [pltpu.VMEM((nr, LANES), F32)] * 5),
        out_shape=[jax.ShapeDtypeStruct((n_seq, t_new, ATTN_W), F32),
                   jax.ShapeDtypeStruct((n_seq, WINDOW, KV_W), F32)],
        compiler_params=_params("arbitrary", "arbitrary"),
        name="nsa_sample",
    )(page_table, *([cache_sel] * P), q, qr, gates, ckv, sel_new, cache_win, win_new, _overlap_t(nch, LANES).T)


def _pool_windows(ext_ref, n_rows, pos, w_ref, lead):
    outs = []
    for gi, w in enumerate(POOL_WINDOWS):
        lanes = slice(gi * POOL_CH, (gi + 1) * POOL_CH)
        tot = None
        for k in range(w):
            v = ext_ref[lead + (slice(HALO - k, HALO - k + n_rows), lanes)]
            tot = v if tot is None else tot + v
        cnt = jnp.minimum(pos + 1, w).astype(F32)
        d = tot / cnt - ext_ref[lead + (slice(HALO, HALO + n_rows), lanes)]
        outs.append((d, w_ref[gi]))
    return outs


def _pool_prompt_kernel(x_ref, h_ref, w_ref, o_ref, ext_ref, *, tm):
    i = pl.program_id(1)
    ext_ref[0:HALO, :] = jnp.where(i == 0, 0.0, h_ref[...])
    ext_ref[HALO:, :] = x_ref[...]
    pos = i * tm + lax.broadcasted_iota(jnp.int32, (tm, 1), 0)
    res = [_dot(d.astype(BF16), w) for d, w in _pool_windows(ext_ref, tm, pos, w_ref, ())]
    o_ref[...] = jnp.concatenate(res, axis=1)


def _pool_prompt(pool_in, w_pool_b, n_seq, tm):
    n = pool_in.shape[0]
    nt = n // n_seq // tm
    hb = tm // HALO
    return pl.pallas_call(
        functools.partial(_pool_prompt_kernel, tm=tm),
        grid=(n_seq, nt),
        in_specs=[pl.BlockSpec((tm, POOL_W), lambda b, i: (b * nt + i, 0)),
                  pl.BlockSpec((HALO, POOL_W), lambda b, i: (jnp.maximum((b * nt + i) * hb - 1, 0), 0)),
                  _const_spec((N_POOL_GROUPS, POOL_CH, POOL_CH))],
        out_specs=pl.BlockSpec((tm, POOL_W), lambda b, i: (b * nt + i, 0)),
        out_shape=jax.ShapeDtypeStruct((n, POOL_W), F32),
        scratch_shapes=[pltpu.VMEM((HALO + tm, POOL_W), F32)],
        compiler_params=_params("parallel", "arbitrary"),
        name="pool_prompt",
    )(pool_in, pool_in, w_pool_b)


def _pool_sample_kernel(x_ref, st_ref, w_ref, o_ref, ns_ref, ext_ref, *, t_new, past):
    nb = x_ref.shape[0]
    ext_ref[:, HALO - POOL_HIST:HALO, :] = st_ref[...]
    ext_ref[:, HALO:, :] = x_ref[...]
    pos = past + lax.broadcasted_iota(jnp.int32, (1, t_new, 1), 1)
    res = [_dot(d.reshape(nb * t_new, POOL_CH).astype(BF16), w)
           for d, w in _pool_windows(ext_ref, t_new, pos, w_ref, (slice(None),))]
    o_ref[...] = jnp.concatenate(res, axis=1)
    ns_ref[...] = ext_ref[:, HALO + t_new - POOL_HIST:HALO + t_new, :]


def _pool_sample(pool_in, state, w_pool_b, past, nb):
    n_seq, t_new, _ = pool_in.shape
    assert t_new % 8 == 0 and n_seq % nb == 0 and past >= POOL_HIST
    return pl.pallas_call(
        functools.partial(_pool_sample_kernel, t_new=t_new, past=past),
        grid=(n_seq // nb,),
        in_specs=[pl.BlockSpec((nb, t_new, POOL_W), lambda i: (i, 0, 0)),
                  pl.BlockSpec((nb, POOL_HIST, POOL_W), lambda i: (i, 0, 0)),
                  _const_spec((N_POOL_GROUPS, POOL_CH, POOL_CH))],
        out_specs=[pl.BlockSpec((nb * t_new, POOL_W), lambda i: (i, 0)),
                   pl.BlockSpec((nb, POOL_HIST, POOL_W), lambda i: (i, 0, 0))],
        out_shape=[jax.ShapeDtypeStruct((n_seq * t_new, POOL_W), F32),
                   jax.ShapeDtypeStruct((n_seq, POOL_HIST, POOL_W), F32)],
        scratch_shapes=[pltpu.VMEM((nb, HALO + t_new, POOL_W), F32)],
        compiler_params=_params("parallel"),
        name="pool_sample",
    )(pool_in, state, w_pool_b)


def _mix_kernel(x_ref, a_ref, m_ref, ga_ref, gp_ref, wo_ref, g2_ref, wq_ref, x1_ref, qm_ref):
    mix = jnp.concatenate([_rms(a_ref[...], ga_ref[...]), _rms(m_ref[...], gp_ref[...])], axis=1)
    x1 = x_ref[...] + _dot(mix.astype(BF16), wo_ref[...])
    x1_ref[...] = x1
    qm_ref[...] = _dot(_rms(x1, g2_ref[...]).astype(BF16), wq_ref[...]).astype(BF16)


def _mix(x, a, m, ga, gp, w_out_b, g2, w_mq_b, tm):
    n = x.shape[0]
    row = lambda w: pl.BlockSpec((tm, w), lambda i: (i, 0))
    return pl.pallas_call(
        _mix_kernel,
        grid=(n // tm,),
        in_specs=[row(D_MODEL), row(ATTN_W), row(POOL_W), _const_spec((1, ATTN_W)), _const_spec((1, POOL_W)),
                  _const_spec((D_MODEL, D_MODEL)), _const_spec((1, D_MODEL)), _const_spec((D_MODEL, MEM_W))],
        out_specs=[row(D_MODEL), row(MEM_W)],
        out_shape=[jax.ShapeDtypeStruct((n, D_MODEL), F32), jax.ShapeDtypeStruct((n, MEM_W), BF16)],
        compiler_params=_params("parallel"),
        name="mix_out",
    )(x, a, m, ga, gp, w_out_b, g2, w_mq_b)


def _cross_kernel(q_ref, kv_ref, o_ref, *, n_mem_blk, rows):
    for i in range(n_mem_blk):
        outs = []
        for h in range(MEM_HEADS):
            q = q_ref[i * rows:(i + 1) * rows, h * MEM_HEAD_DIM:(h + 1) * MEM_HEAD_DIM]
            k = kv_ref[i, :, h * MEM_HEAD_DIM:(h + 1) * MEM_HEAD_DIM].astype(BF16)
            v = kv_ref[i, :, MEM_W + h * MEM_HEAD_DIM:MEM_W + (h + 1) * MEM_HEAD_DIM].astype(BF16)
            s = _dot_nt(q, k) * (MEM_HEAD_DIM ** -0.5)
            e = jnp.exp(s - jnp.max(s, axis=-1, keepdims=True))
            o = _dot(e.astype(BF16), v) / jnp.sum(e, axis=-1, keepdims=True)
            outs.append(o)
        o_ref[i * rows:(i + 1) * rows, :] = jnp.concatenate(outs, axis=1).astype(BF16)


def _cross(qm, mem_kv, rows, n_mem_blk, tiles_per_mem):
    n = qm.shape[0]
    tm = rows * n_mem_blk
    return pl.pallas_call(
        functools.partial(_cross_kernel, n_mem_blk=n_mem_blk, rows=rows),
        grid=(n // tm,),
        in_specs=[pl.BlockSpec((tm, MEM_W), lambda i: (i, 0)),
                  pl.BlockSpec((n_mem_blk, N_MEM, 2 * MEM_W), lambda i: (i // tiles_per_mem, 0, 0))],
        out_specs=pl.BlockSpec((tm, MEM_W), lambda i: (i, 0)),
        out_shape=jax.ShapeDtypeStruct((n, MEM_W), BF16),
        compiler_params=_params("parallel"),
        name="cross_attn",
    )(qm, mem_kv)


def _ffn_kernel(x1_ref, o_ref, wmo_ref, g3_ref, wup_ref, wdn_ref, gf_ref, y_ref):
    x2 = x1_ref[...] + _dot(o_ref[...], wmo_ref[...])
    xn = _rms(x2, g3_ref[...]).astype(BF16)
    acc = x2
    for c in range(D_FF // FF_CHUNK):
        h = jnp.maximum(_dot(xn, wup_ref[:, c * FF_CHUNK:(c + 1) * FF_CHUNK]), 0.0)
        acc = acc + _dot((h * h).astype(BF16), wdn_ref[c * FF_CHUNK:(c + 1) * FF_CHUNK, :])
    y_ref[...] = _rms(acc, gf_ref[...])


def _ffn(x1, o, w_mo_b, g3, w_up_b, w_down_b, g_final, tm):
    n = x1.shape[0]
    row = lambda w: pl.BlockSpec((tm, w), lambda i: (i, 0))
    return pl.pallas_call(
        _ffn_kernel,
        grid=(n // tm,),
        in_specs=[row(D_MODEL), row(MEM_W), _const_spec((MEM_W, D_MODEL)), _const_spec((1, D_MODEL)),
                  _const_spec((D_MODEL, D_FF)), _const_spec((D_FF, D_MODEL)), _const_spec((1, D_MODEL))],
        out_specs=row(D_MODEL),
        out_shape=jax.ShapeDtypeStruct((n, D_MODEL), F32),
        compiler_params=_params("parallel"),
        name="ffn",
    )(x1, o, w_mo_b, g3, w_up_b, w_down_b, g_final)


def _tile(n, pref):
    t = min(n, pref)
    assert n % t == 0
    return t


def kernel(x_prompt, x_sample, cache_cmp_kv, cache_sel_kv, cache_win_kv, state_pool, cache_mem_kv, page_table, mem_prompt, w_in, pe_cmp, w_cmp1, w_cmp2, w_pool, g_attn_out, pool_scale, w_out, g_norm1, g_norm2, g_mem, w_mq, w_mkv, w_mo, g_norm3, w_up, w_down, g_final):
    assert w_in.shape[0] == 1, "single-layer trunk"
    B, S, _ = x_prompt.shape
    DB, T, _ = x_sample.shape
    n_phys = cache_sel_kv.shape[1]
    past = page_table.shape[1] * PAGE_SIZE
    kv5 = (2, N_KV_HEADS, HEAD_DIM)
    r2 = lambda v: v.reshape(1, -1)

    w_in_p = _permute_w_in(w_in[0])
    w_out_b, w_mq_b, w_mkv_b, w_mo_b = (w[0].astype(BF16) for w in (w_out, w_mq, w_mkv, w_mo))
    w_up_b, w_down_b, w_pool_b = w_up[0].astype(BF16), w_down[0].astype(BF16), w_pool[0].astype(BF16)
    g1, g2, g3, gm, ga, gp, gf = (r2(v) for v in (g_norm1[0], g_norm2[0], g_norm3[0], g_mem[0], g_attn_out[0],
                                                  pool_scale[0], g_final))
    cw = _compress_weights(pe_cmp[0], w_cmp1[0], w_cmp2[0])

    np_tok = B * S
    tm_p = _tile(S, 512)
    xp = x_prompt.reshape(np_tok, D_MODEL)
    q, qr, cmp_kv, sel_kv, win_kv, sel_b, win_b, gates, pool_in = _in_proj(
        xp, g1, w_in_p, jnp.tile(jnp.arange(S), B), tm_p)
    mem_f, mem_b = _mem_kv(mem_prompt.reshape(B * N_MEM, D_MODEL), gm, w_mkv_b, _tile(B * N_MEM, 256))
    ckv = _compress_prompt(cmp_kv, B, cw)
    a = _nsa_prompt(q, qr, gates, ckv, sel_b, win_b, B)
    m = _pool_prompt(pool_in, w_pool_b, B, tm_p)
    x1, qm = _mix(xp, a, m, ga, gp, w_out_b, g2, w_mq_b, tm_p)
    o = _cross(qm, mem_b.reshape(B, N_MEM, 2 * MEM_W), tm_p, 1, S // tm_p)
    y_p = _ffn(x1, o, w_mo_b, g3, w_up_b, w_down_b, gf, tm_p)
    p_state = (cmp_kv.reshape((1, B, S) + kv5), sel_kv.reshape((1, B, S) + kv5),
               win_kv.reshape((B, S) + kv5)[None, :, -min(WINDOW, S):],
               pool_in.reshape(B, S, POOL_W)[None, :, -POOL_HIST:],
               mem_f.reshape(1, B, N_MEM, 2, MEM_HEADS, MEM_HEAD_DIM))

    ns_tok = DB * T
    tm_s = _tile(ns_tok, 512)
    xs = x_sample.reshape(ns_tok, D_MODEL)
    q, qr, cmp_kv, sel_kv, win_kv, _, _, gates, pool_in = _in_proj(
        xs, g1, w_in_p, jnp.tile(past + jnp.arange(T), DB), tm_s)
    r3 = lambda v: v.reshape(DB, T, v.shape[-1])
    ckv = _compress_paged(_pages_t(cache_cmp_kv[0]), page_table, cw, 64)
    a, s_win = _nsa_sample(r3(q), r3(qr), r3(gates), ckv, _pages_t(cache_sel_kv[0]), page_table,
                           r3(sel_kv), _pages_t(cache_win_kv[0]), r3(win_kv), 32)
    m, s_pool = _pool_sample(r3(pool_in), state_pool[0], w_pool_b, past, _tile(DB, 16))
    x1, qm = _mix(xs, a.reshape(ns_tok, ATTN_W), m, ga, gp, w_out_b, g2, w_mq_b, tm_s)
    o = _cross(qm, cache_mem_kv[0].reshape(DB, N_MEM, 2 * MEM_W), T, _tile(DB, 4), 1)
    y_s = _ffn(x1, o, w_mo_b, g3, w_up_b, w_down_b, gf, tm_s)
    s_state = (cmp_kv.reshape((1, DB, T) + kv5), sel_kv.reshape((1, DB, T) + kv5),
               s_win.reshape((1, DB, WINDOW) + kv5), s_pool[None])

    return (y_p.reshape(B, S, D_MODEL), y_s.reshape(DB, T, D_MODEL)) + p_state + s_state
```

```python
import functools

import jax
import jax.numpy as jnp
from jax import lax
from jax.experimental import pallas as pl
from jax.experimental.pallas import tpu as pltpu

D_MODEL = 1024
PAGE_SIZE = 128
ATTN_W = 512
POOL_W = 512
HEAD_DIM = 64
N_HEADS = 8
N_KV_HEADS = 2
HEADS_PER_KV = N_HEADS // N_KV_HEADS
SCALE = HEAD_DIM ** -0.5
LOG2E = 1.4426950408889634
Q_SCALE = SCALE * LOG2E
ROT_DIM = HEAD_DIM // 4
ROPE_THETA = 500000.0
CMP_LEN = 32
CMP_STRIDE = 16
CMP_HID = HEAD_DIM
SEL_BLOCK = 64
SEL_TOPN = 16
WINDOW = 512
FORCE_BONUS = 1000.0
POOL_WINDOWS = (2, 4, 8, 16)
N_POOL_GROUPS = len(POOL_WINDOWS)
POOL_CH = POOL_W // N_POOL_GROUPS
POOL_HIST = max(POOL_WINDOWS) - 1
N_MEM = 256
MEM_HEADS = 4
MEM_HEAD_DIM = 128
MEM_W = MEM_HEADS * MEM_HEAD_DIM
D_FF = 4 * D_MODEL
EPS = 1e-6
KV_W = 2 * N_KV_HEADS * HEAD_DIM
N_GATES = 3 * N_HEADS
LANES = 128
VMEM_LIMIT = 56 * 1024 * 1024
QW = N_HEADS * LANES
TQ = 128
NEG = -1e30
CHUNK_W = CMP_STRIDE * KV_W
CMP_R = CMP_LEN // CMP_STRIDE
CHUNKS_PER_PAGE = PAGE_SIZE // CMP_STRIDE
HALO = 16
FF_CHUNK = 1024

F32 = jnp.float32
BF16 = jnp.bfloat16


def _rms(x, g):
    return x * lax.rsqrt(jnp.mean(x * x, axis=-1, keepdims=True) + EPS) * g


def _dot(a, b):
    return jnp.dot(a, b, preferred_element_type=F32)


def _dot_nt(a, b):
    return lax.dot_general(a, b, (((1,), (1,)), ((), ())), preferred_element_type=F32)


def _const_spec(shape):
    n = len(shape)
    return pl.BlockSpec(shape, lambda *_: (0,) * n, pipeline_mode=pl.Buffered(1))


def _params(*sem):
    return pltpu.CompilerParams(dimension_semantics=sem, vmem_limit_bytes=VMEM_LIMIT)


X_Q, X_CMP, X_SEL, X_WIN, X_POOL, X_GATE = 0, 1024, 1280, 1536, 1792, 2304
X_IN = X_GATE + LANES


def _permute_w_in(w_in):
    q, kc, ks, kw, gl, pool = jnp.split(w_in, (512, 768, 1024, 1280, 1280 + N_GATES), axis=1)
    z = jnp.zeros((D_MODEL, HEAD_DIM), w_in.dtype)
    qx = []
    for h in range(N_HEADS):
        qh = q[:, h * HEAD_DIM:(h + 1) * HEAD_DIM]
        qx += [qh, z] if h // HEADS_PER_KV == 0 else [z, qh]
    pad = jnp.zeros((D_MODEL, LANES - N_GATES), w_in.dtype)
    return jnp.concatenate(qx + [kc, ks, kw, pool, gl, pad], axis=1).astype(BF16)


def _rope_tables(pos):
    half = ROT_DIM // 2
    inv = ROPE_THETA ** (-jnp.arange(half, dtype=F32) / half)
    ang = pos.astype(F32)[:, None] * inv[None, :]
    cos, sin = jnp.cos(ang), jnp.sin(ang)
    z = jnp.zeros_like(cos)
    rest0 = jnp.zeros((pos.shape[0], HEAD_DIM - ROT_DIM), F32)
    cs = jnp.concatenate([cos, cos, rest0 + 1.0], axis=1)
    sa = jnp.concatenate([-sin, z, rest0], axis=1)
    sb = jnp.concatenate([z, sin, rest0], axis=1)
    return tuple(jnp.tile(t, (1, LANES // HEAD_DIM)) for t in (cs, sa, sb))


def _rope(v, cs, sa, sb):
    n = v.shape[1] // LANES
    w = v.shape[1]
    cs, sa, sb = (jnp.tile(t, (1, n)) for t in (cs, sa, sb))
    return v * cs + pltpu.roll(v, w - ROT_DIM // 2, 1) * sa + pltpu.roll(v, ROT_DIM // 2, 1) * sb


def _in_proj_kernel(x_ref, g_ref, w_ref, cs_ref, sa_ref, sb_ref,
                    q_ref, qr_ref, cmp_ref, sel_ref, win_ref, selb_ref, winb_ref, gate_ref, pool_ref, *, feature_major):
    xb = _rms(x_ref[...], g_ref[...]).astype(BF16)
    cs, sa, sb = cs_ref[...], sa_ref[...], sb_ref[...]

    def put_state(ref, v):
        if feature_major:
            ref[0] = jnp.concatenate([v[:, c * LANES:(c + 1) * LANES].T for c in range(KV_W // LANES)], axis=0)
        else:
            ref[...] = v

    q = _dot(xb, w_ref[:, X_Q:X_CMP])
    q_ref[...] = (q * Q_SCALE).astype(BF16)
    qr_ref[...] = (_rope(q, cs, sa, sb) * Q_SCALE).astype(BF16)
    put_state(cmp_ref, _dot(xb, w_ref[:, X_CMP:X_SEL]))
    for lo, f_ref, b_ref in ((X_SEL, sel_ref, selb_ref), (X_WIN, win_ref, winb_ref)):
        kv = _dot(xb, w_ref[:, lo:lo + KV_W])
        kv = jnp.concatenate([_rope(kv[:, :LANES], cs, sa, sb), kv[:, LANES:]], axis=1)
        put_state(f_ref, kv)
        b_ref[...] = kv.astype(BF16)
    pool_ref[...] = _dot(xb, w_ref[:, X_POOL:X_GATE])
    gate_ref[...] = jax.nn.sigmoid(_dot(xb, w_ref[:, X_GATE:X_IN]))


def _in_proj(x, g1, w_in_p, pos_rows, tm, n_seq=None):
    n = x.shape[0]
    cs, sa, sb = _rope_tables(pos_rows)
    row = lambda w: pl.BlockSpec((tm, w), lambda i: (i, 0))
    sds = lambda w, dt: jax.ShapeDtypeStruct((n, w), dt)
    if n_seq is None:
        st_spec, st_shape = row(KV_W), sds(KV_W, F32)
    else:
        nt = n // n_seq // tm
        st_spec = pl.BlockSpec((1, KV_W, tm), lambda i: (i // nt, 0, i % nt))
        st_shape = jax.ShapeDtypeStruct((n_seq, KV_W, n // n_seq), F32)
    return pl.pallas_call(
        functools.partial(_in_proj_kernel, feature_major=n_seq is not None),
        grid=(n // tm,),
        in_specs=[row(D_MODEL), _const_spec((1, D_MODEL)), _const_spec((D_MODEL, X_IN)),
                  row(LANES), row(LANES), row(LANES)],
        out_specs=[row(QW), row(QW), st_spec, st_spec, st_spec, row(KV_W), row(KV_W), row(LANES), row(POOL_W)],
        out_shape=[sds(QW, BF16), sds(QW, BF16), st_shape, st_shape, st_shape,
                   sds(KV_W, BF16), sds(KV_W, BF16), sds(LANES, F32), sds(POOL_W, F32)],
        compiler_params=_params("parallel"),
        name="in_proj",
    )(x, g1, w_in_p, cs, sa, sb)


def _mem_kv_kernel(m_ref, g_ref, w_ref, o_ref, ob_ref):
    kv = _dot(_rms(m_ref[...], g_ref[...]).astype(BF16), w_ref[...])
    o_ref[...] = kv
    ob_ref[...] = kv.astype(BF16)


def _mem_kv(mem, g_mem, w_mkv_b, tm):
    n = mem.shape[0]
    row = lambda w: pl.BlockSpec((tm, w), lambda i: (i, 0))
    return pl.pallas_call(
        _mem_kv_kernel,
        grid=(n // tm,),
        in_specs=[row(D_MODEL), _const_spec((1, D_MODEL)), _const_spec((D_MODEL, 2 * MEM_W))],
        out_specs=[row(2 * MEM_W), row(2 * MEM_W)],
        out_shape=[jax.ShapeDtypeStruct((n, 2 * MEM_W), F32), jax.ShapeDtypeStruct((n, 2 * MEM_W), BF16)],
        compiler_params=_params("parallel"),
        name="mem_kv",
    )(mem, g_mem, w_mkv_b)


def _compress_weights(pe, w1, w2):
    w1r = w1.reshape(2, CMP_R, CMP_STRIDE, HEAD_DIM, CMP_HID)
    eye_c = jnp.eye(2, dtype=w1.dtype)
    eye_g = jnp.eye(N_KV_HEADS, dtype=w1.dtype)
    w1big = jnp.einsum('crsdh,cx,gy->scgdrxyh', w1r, eye_c, eye_g).reshape(CHUNK_W, CMP_R * KV_W)
    pe_term = jnp.einsum('cld,cldh->ch', pe, w1.reshape(2, CMP_LEN, HEAD_DIM, CMP_HID))
    bias = jnp.broadcast_to(pe_term[:, None, :], (2, N_KV_HEADS, CMP_HID)).reshape(1, KV_W)
    w2big = jnp.einsum('chd,cx,gy->cghxyd', w2, eye_c, eye_g).reshape(KV_W, KV_W)
    return w1big.astype(BF16), bias, w2big.astype(BF16)


def _compress_kernel(*refs, n_in, ppr, has_pt):
    if has_pt:
        refs = refs[1:]
    x_refs = refs[:n_in]
    w1_ref, b_ref, w2_ref, o_ref, carry_ref = refs[n_in:]

    @pl.when(pl.program_id(1) == 0)
    def _():
        carry_ref[...] = jnp.zeros_like(carry_ref)

    r_i = lax.broadcasted_iota(jnp.int32, (PAGE_SIZE, PAGE_SIZE), 0)
    t_i = lax.broadcasted_iota(jnp.int32, (PAGE_SIZE, PAGE_SIZE), 1)
    src = CMP_STRIDE * (r_i & (CHUNKS_PER_PAGE - 1)) + lax.shift_right_logical(r_i, 3)
    pick = jnp.where(t_i == src, 1.0, 0.0).astype(BF16)
    pages = [_dot_nt(pick, r[0, :, j * PAGE_SIZE:(j + 1) * PAGE_SIZE].astype(BF16))
             for r in x_refs for j in range(ppr)]
    cols = [jnp.concatenate([pg[s * CHUNKS_PER_PAGE:(s + 1) * CHUNKS_PER_PAGE, :] for pg in pages], axis=0)
            for s in range(CMP_STRIDE)]
    x = jnp.concatenate(cols, axis=1)
    u = _dot(x.astype(BF16), w1_ref[...])
    u0, u1 = u[:, :KV_W], u[:, KV_W:]
    rows = u0.shape[0]
    row = lax.broadcasted_iota(jnp.int32, u0.shape, 0)
    prev = jnp.where(row == 0, carry_ref[0:1, :], pltpu.roll(u0, 1, 0))
    carry_ref[...] = jnp.broadcast_to(u0[rows - 1:rows, :], carry_ref.shape)
    h = prev + u1 + b_ref[...]
    o_ref[0] = _dot(jax.nn.gelu(h).astype(BF16), w2_ref[...]).astype(BF16)


def _compress_call(n_seq, n_pages, n_in, ppr, has_pt, x_spec_fn):
    P = n_in * ppr
    assert n_pages % P == 0
    grid = (n_seq, n_pages // P)
    in_specs = [x_spec_fn(k) for k in range(n_in)] + [
        _const_spec((CHUNK_W, CMP_R * KV_W)), _const_spec((1, KV_W)), _const_spec((KV_W, KV_W))]
    out_spec = pl.BlockSpec((1, P * CHUNKS_PER_PAGE, KV_W), (lambda b, i, *_: (b, i, 0)))
    kern = functools.partial(_compress_kernel, n_in=n_in, ppr=ppr, has_pt=has_pt)
    common = dict(out_shape=jax.ShapeDtypeStruct((n_seq, n_pages * CHUNKS_PER_PAGE, KV_W), BF16),
                  compiler_params=_params("parallel", "arbitrary"), name="compress")
    scratch = [pltpu.VMEM((8, KV_W), F32)]
    if has_pt:
        return pl.pallas_call(kern, grid_spec=pltpu.PrefetchScalarGridSpec(
            num_scalar_prefetch=1, grid=grid, in_specs=in_specs, out_specs=out_spec, scratch_shapes=scratch), **common)
    return pl.pallas_call(kern, grid=grid, in_specs=in_specs, out_specs=out_spec, scratch_shapes=scratch, **common)


def _compress_seq(cmp_t, cw, pages_per_step):
    n_seq, _, seq = cmp_t.shape
    n_pages = seq // PAGE_SIZE
    P = min(pages_per_step, n_pages)
    spec = lambda k: pl.BlockSpec((1, KV_W, P * PAGE_SIZE), lambda b, i: (b, 0, i))
    return _compress_call(n_seq, n_pages, 1, P, False, spec)(cmp_t, *cw)


def _pages_t(cache):
    return cache.transpose(0, 2, 3, 4, 1).reshape(cache.shape[0], KV_W, cache.shape[1])


def _compress_paged(cache_t, page_table, cw, pages_per_step):
    n_seq, n_pages = page_table.shape
    P = min(pages_per_step, n_pages)
    spec = lambda k: pl.BlockSpec((1, KV_W, PAGE_SIZE), lambda b, i, pt: (pt[b, i * P + k], 0, 0))
    return _compress_call(n_seq, n_pages, P, 1, True, spec)(page_table, *([cache_t] * P), *cw)


def _overlap_t(nch, n_sel_pad):
    m = jnp.arange(nch)[None, :]
    j = jnp.arange(n_sel_pad)[:, None]
    lo = (m - 1) * CMP_STRIDE
    ov = jnp.maximum(jnp.minimum(lo + CMP_LEN, (j + 1) * SEL_BLOCK) - jnp.maximum(lo, j * SEL_BLOCK), 0)
    return jnp.where(m >= 1, ov.astype(F32) / CMP_LEN, 0.0).astype(BF16)


def _heads_to_rows(x):
    return jnp.concatenate([x[:, h * LANES:(h + 1) * LANES] for h in range(N_HEADS)], axis=0)


def _topk_bias(imp, pos, n_blk, k_top, axis):
    j = lax.broadcasted_iota(jnp.int32, imp.shape, axis)
    cur = lax.shift_right_logical(pos, 6)
    forced = (j == 0) | (j == cur) | (j == cur - 1)
    score = jnp.where(j * SEL_BLOCK <= pos, imp + jnp.where(forced, FORCE_BONUS, 0.0), -1e9)
    score = jnp.where(j < n_blk, score, -jnp.inf)
    bias = jnp.full(imp.shape, NEG, F32)
    for _ in range(k_top):
        mx = jnp.max(score, axis=axis, keepdims=True)
        idx = jnp.min(jnp.where(score == mx, j, LANES), axis=axis, keepdims=True)
        hit = j == idx
        bias = jnp.where(hit, 0.0, bias)
        score = jnp.where(hit, -jnp.inf, score)
    return bias


def _split_dot(dot, ps):
    hi = ps.astype(BF16)
    lo = (ps - hi.astype(F32)).astype(BF16)
    return dot(hi) + dot(lo)


ONES_ROWS = 16
VT_ROWS = LANES + ONES_ROWS


def _ones_rows(n):
    return jnp.where(lax.broadcasted_iota(jnp.int32, (ONES_ROWS, n), 0) == 0, 1.0, 0.0).astype(BF16)


def _nsa_prompt_kernel(q_ref, qr_ref, g_ref, ckv_ref, ksel_ref, kwin_ref, ovt_ref, a_ref,
                       kx_ref, vts_ref, vtw_ref, lc_ref, *, tk, n_sel, k_top):
    i = pl.program_id(1)
    q0 = pl.multiple_of(i * TQ, TQ)
    nc = ckv_ref.shape[1]

    @pl.when(i == 0)
    def _():
        kx_ref[...] = jnp.zeros_like(kx_ref)
        vts_ref[...] = jnp.zeros_like(vts_ref)
        vtw_ref[...] = jnp.zeros_like(vtw_ref)
        lc_ref[0:LANES, :] = ckv_ref[0, :, LANES:2 * LANES].astype(F32).T.astype(BF16)
        lc_ref[LANES:2 * LANES, :] = ovt_ref[...]
        lc_ref[2 * LANES:, :] = _ones_rows(nc)

    key_t = q0 + lax.broadcasted_iota(jnp.int32, (TQ, 1), 0)
    onehot = jnp.where(lax.shift_right_logical(key_t, 6) == lax.broadcasted_iota(jnp.int32, (TQ, LANES), 1), 1.0, 0.0)
    kx_ref[pl.ds(q0, TQ), :] = jnp.concatenate([ksel_ref[pl.ds(q0, TQ), 0:LANES], onehot.astype(BF16)], axis=1)
    for k_ref, vt_ref in ((ksel_ref, vts_ref), (kwin_ref, vtw_ref)):
        v = k_ref[pl.ds(q0, TQ), LANES:2 * LANES].astype(F32)
        vt_ref[:, pl.ds(q0, TQ)] = jnp.concatenate([v.T.astype(BF16), _ones_rows(TQ)], axis=0)

    nl = N_HEADS * TQ
    lane = lax.broadcasted_iota(jnp.int32, (1, nl), 1)
    pos = q0 + (lane & (TQ - 1))
    qc = _heads_to_rows(q_ref)
    qr = _heads_to_rows(qr_ref)

    def weights(s, mask):
        s = jnp.where(mask, s, NEG)
        return jnp.where(mask, jnp.exp2(s - jnp.max(s, axis=0, keepdims=True)), 0.0).astype(BF16)

    m_idx = lax.broadcasted_iota(jnp.int32, (nc, 1), 0)
    c_end = jnp.where(m_idx >= 1, m_idx * CMP_STRIDE + (CMP_STRIDE - 1), 2 ** 30)
    acc = _dot(lc_ref[...], weights(_dot_nt(ckv_ref[0, :, 0:LANES], qc), c_end <= pos))
    rden = 1.0 / jnp.maximum(acc[2 * LANES:2 * LANES + 1, :], 1e-30)
    oc_t = acc[0:LANES, :] * rden
    imp_h = acc[LANES:2 * LANES, :] * rden

    pos_t = q0 + lax.broadcasted_iota(jnp.int32, (1, TQ), 1)
    biases = []
    for g in range(N_KV_HEADS):
        imp_t = sum(imp_h[:, (g * HEADS_PER_KV + hh) * TQ:(g * HEADS_PER_KV + hh + 1) * TQ] for hh in range(HEADS_PER_KV))
        biases.append(_topk_bias(imp_t, pos_t, n_sel, k_top, 0).T.astype(BF16))
    bias_rows = jnp.concatenate([biases[h // HEADS_PER_KV] for h in range(N_HEADS)], axis=0)
    qs = jnp.concatenate([qr, bias_rows], axis=1)

    def sel_chunk(c0, masked, carry):
        m, acc = carry
        s = _dot_nt(kx_ref[pl.ds(c0, tk), :], qs)
        if masked:
            s = jnp.where(c0 + lax.broadcasted_iota(jnp.int32, (tk, 1), 0) <= pos, s, NEG)
        m_new = jnp.maximum(m, jnp.max(s, axis=0, keepdims=True))
        p = jnp.exp2(s - m_new).astype(BF16)
        acc = jnp.exp2(m - m_new) * acc + _dot(vts_ref[:, pl.ds(c0, tk)], p)
        return m_new, acc

    n_full = q0 // tk
    carry = (jnp.full((1, nl), NEG, F32), jnp.zeros((VT_ROWS, nl), F32))
    one = lambda c, cr: sel_chunk(pl.multiple_of(c * tk, tk), False, cr)
    carry = lax.fori_loop(0, n_full // 2, lambda c, cr: one(2 * c + 1, one(2 * c, cr)), carry)
    carry = lax.fori_loop(n_full // 2 * 2, n_full, one, carry)
    _, acc = sel_chunk(pl.multiple_of(n_full * tk, tk), True, carry)
    os_t = acc[0:LANES, :] / jnp.maximum(acc[LANES:LANES + 1, :], 1e-30)

    wk = WINDOW + TQ
    w0 = pl.multiple_of(jnp.maximum(q0 - WINDOW, 0), TQ)
    s = _dot_nt(kwin_ref[pl.ds(w0, wk), 0:LANES], qr)
    kp = w0 + lax.broadcasted_iota(jnp.int32, (wk, 1), 0)
    s = jnp.concatenate([jnp.where((kp[:TQ] <= pos) & (kp[:TQ] >= pos - WINDOW), s[:TQ], NEG),
                         jnp.where(kp[TQ:] <= pos, s[TQ:], NEG)], axis=0)
    acc = _dot(vtw_ref[:, pl.ds(w0, wk)], jnp.exp2(s - jnp.max(s, axis=0, keepdims=True)).astype(BF16))
    ow_t = acc[0:LANES, :] / jnp.maximum(acc[LANES:LANES + 1, :], 1e-30)

    g_t = g_ref[...].T
    outs = []
    for h in range(N_HEADS):
        g = h // HEADS_PER_KV
        blk_h = lambda o: o[g * HEAD_DIM:(g + 1) * HEAD_DIM, h * TQ:(h + 1) * TQ]
        gate = lambda j: g_t[3 * h + j:3 * h + j + 1, :]
        outs.append(gate(0) * blk_h(oc_t) + gate(1) * blk_h(os_t) + gate(2) * blk_h(ow_t))
    a_ref[...] = jnp.concatenate(outs, axis=0).T


def _nsa_prompt(q, qr, gates, ckv, ksel, kwin, n_seq):
    n = q.shape[0]
    seq = n // n_seq
    nch = ckv.shape[1]
    n_sel = seq // SEL_BLOCK
    assert seq % TQ == 0 and seq >= WINDOW + TQ and n_sel <= LANES and seq % SEL_BLOCK == 0
    tk = min(512, seq)
    assert seq % tk == 0
    nt = seq // TQ
    tile = lambda w: pl.BlockSpec((TQ, w), lambda b, i: (b * nt + i, 0))
    whole = lambda: pl.BlockSpec((seq, KV_W), lambda b, i: (b, 0), pipeline_mode=pl.Buffered(1))
    kern = functools.partial(_nsa_prompt_kernel, tk=tk, n_sel=n_sel, k_top=min(SEL_TOPN, n_sel))
    return pl.pallas_call(
        kern,
        grid=(n_seq, nt),
        in_specs=[tile(QW), tile(QW), tile(LANES),
                  pl.BlockSpec((1, nch, KV_W), lambda b, i: (b, 0, 0)),
                  whole(), whole(), _const_spec((LANES, nch))],
        out_specs=tile(ATTN_W),
        out_shape=jax.ShapeDtypeStruct((n, ATTN_W), F32),
        scratch_shapes=[pltpu.VMEM((seq, 2 * LANES), BF16), pltpu.VMEM((VT_ROWS, seq), BF16),
                        pltpu.VMEM((VT_ROWS, seq), BF16), pltpu.VMEM((2 * LANES + ONES_ROWS, nch), BF16)],
        compiler_params=_params("arbitrary", "arbitrary"),
        name="nsa_prompt",
    )(q, qr, gates, ckv, ksel, kwin, _overlap_t(nch, LANES))


def _softmax_rows2(s_a, mask_a, s_b, mask_b):
    s_a = jnp.where(mask_a, s_a, NEG)
    s_b = jnp.where(mask_b, s_b, NEG)
    mx = jnp.maximum(jnp.max(s_a, axis=1, keepdims=True), jnp.max(s_b, axis=1, keepdims=True))
    e_a = jnp.where(mask_a, jnp.exp2(s_a - mx), 0.0)
    e_b = jnp.where(mask_b, jnp.exp2(s_b - mx), 0.0)
    den = jnp.sum(e_a, axis=1, keepdims=True) + jnp.sum(e_b, axis=1, keepdims=True)
    return e_a, e_b, jnp.maximum(den, 1e-30)


def _pad_rows(x, rows):
    return jnp.concatenate([x, jnp.zeros((rows - x.shape[0], x.shape[1]), x.dtype)], axis=0)


def _nsa_sample_kernel(*refs, n_pg, t_new, past, n_cb, k_cache):
    pg_refs = refs[1:1 + n_pg]
    (q_ref, qr_ref, g_ref, ckv_ref, seln_ref, cwin_ref, winn_ref, ovl_ref,
     a_ref, swin_ref, qs_ref, oc_ref, ow_ref, m_ref, l_ref, acc_ref) = refs[1 + n_pg:]
    j = pl.program_id(1)
    nr = N_HEADS * t_new
    row_t = lax.rem(lax.broadcasted_iota(jnp.int32, (nr, 1), 0), t_new)
    pos = past + row_t
    lane = lax.broadcasted_iota(jnp.int32, (1, LANES), 1)

    @pl.when(j == 0)
    def _():
        qc = _heads_to_rows(q_ref[0].astype(F32)).astype(BF16)
        qr = _heads_to_rows(qr_ref[0].astype(F32)).astype(BF16)
        nc = ckv_ref.shape[1]
        m_idx = lax.broadcasted_iota(jnp.int32, (1, nc), 1)
        cmask = (m_idx >= 1) & (m_idx * CMP_STRIDE + (CMP_STRIDE - 1) <= pos)
        s = jnp.where(cmask, _dot_nt(qc, ckv_ref[0, :, 0:LANES]), NEG)
        e = jnp.where(cmask, jnp.exp2(s - jnp.max(s, axis=1, keepdims=True)), 0.0)
        p_c = e / jnp.maximum(jnp.sum(e, axis=1, keepdims=True), 1e-30)
        oc_ref[...] = _dot(p_c.astype(BF16), ckv_ref[0, :, LANES:2 * LANES])
        imps = []
        for g in range(N_KV_HEADS):
            ps = sum(p_c[(g * HEADS_PER_KV + hh) * t_new:(g * HEADS_PER_KV + hh + 1) * t_new, :]
                     for hh in range(HEADS_PER_KV))
            imps.append(_split_dot(lambda v: _dot(v, ovl_ref[...]), ps))
        imp_t = _pad_rows(jnp.concatenate(imps, axis=0), LANES).T
        bias = _topk_bias(imp_t, past + lax.rem(lane, t_new), n_cb, k_cache, 0).T
        biases = [bias[g * t_new:(g + 1) * t_new, :] for g in range(N_KV_HEADS)]
        bias_rows = jnp.concatenate([biases[h // HEADS_PER_KV] for h in range(N_HEADS)], axis=0)
        qs_ref[...] = jnp.concatenate([qr, bias_rows.astype(BF16)], axis=1)
        cw = jnp.concatenate([cwin_ref[0, c * LANES:(c + 1) * LANES, :].T for c in range(KV_W // LANES)], axis=1)
        wn = _pad_rows(winn_ref[0], LANES)
        r_idx = lax.broadcasted_iota(jnp.int32, (1, WINDOW), 1)
        e_a, e_b, den = _softmax_rows2(
            _dot_nt(qr, cw[:, 0:LANES].astype(BF16)), r_idx >= row_t,
            _dot_nt(qr, wn[:, 0:LANES].astype(BF16)), lane <= row_t)
        ow_ref[...] = (_dot(e_a.astype(BF16), cw[:, LANES:].astype(BF16))
                       + _dot(e_b.astype(BF16), wn[:, LANES:].astype(BF16))) / den
        swin_ref[0, 0:WINDOW - t_new, :] = cw[t_new:, :]
        swin_ref[0, WINDOW - t_new:, :] = winn_ref[0]
        m_ref[...] = jnp.full_like(m_ref, NEG)
        l_ref[...] = jnp.zeros_like(l_ref)
        acc_ref[...] = jnp.zeros_like(acc_ref)

    def online(s, pv):
        m = m_ref[:, 0:1]
        m_new = jnp.maximum(m, jnp.max(s, axis=1, keepdims=True))
        alpha = jnp.exp2(m - m_new)
        p = jnp.exp2(s - m_new)
        l_ref[...] = jnp.broadcast_to(alpha * l_ref[:, 0:1] + jnp.sum(p, axis=1, keepdims=True), l_ref.shape)
        acc_ref[...] = alpha * acc_ref[...] + pv(p.astype(BF16))
        m_ref[...] = jnp.broadcast_to(m_new, m_ref.shape)

    cat = lambda lo: jnp.concatenate([r[0, lo:lo + LANES, :] for r in pg_refs], axis=1).astype(BF16)
    k_t, v_t = cat(0), cat(LANES)
    nk = k_t.shape[1]
    blk = lax.shift_right_logical(j * nk + lax.broadcasted_iota(jnp.int32, (1, nk), 1), 6)
    onehot_t = jnp.where(blk == lax.broadcasted_iota(jnp.int32, (LANES, 1), 0), 1.0, 0.0).astype(BF16)
    online(_dot(qs_ref[...], jnp.concatenate([k_t, onehot_t], axis=0)), lambda p: _dot_nt(p, v_t))

    @pl.when(j == pl.num_programs(1) - 1)
    def _():
        sn = _pad_rows(seln_ref[0], LANES)
        s = _dot_nt(qs_ref[:, 0:LANES], sn[:, 0:LANES].astype(BF16))
        online(jnp.where(lane <= row_t, s, NEG), lambda p: _dot(p, sn[:, LANES:].astype(BF16)))
        os = acc_ref[...] / jnp.maximum(l_ref[:, 0:1], 1e-30)
        gts = g_ref[0]
        outs = []
        for h in range(N_HEADS):
            g = h // HEADS_PER_KV
            blk_h = lambda o: o[h * t_new:(h + 1) * t_new, g * HEAD_DIM:(g + 1) * HEAD_DIM]
            gate = lambda k: gts[:, 3 * h + k:3 * h + k + 1]
            outs.append(gate(0) * blk_h(oc_ref[...]) + gate(1) * blk_h(os) + gate(2) * blk_h(ow_ref[...]))
        a_ref[0] = jnp.concatenate(outs, axis=1)


def _nsa_sample(q, qr, gates, ckv, cache_sel, page_table, sel_new, cache_win, win_new, pages_per_step):
    n_seq, t_new = q.shape[:2]
    n_pages = page_table.shape[1]
    past = n_pages * PAGE_SIZE
    n_cb = past // SEL_BLOCK
    assert cache_win.shape[2] == WINDOW and n_cb <= LANES and t_new <= SEL_BLOCK and t_new % 8 == 0
    k_cache = min(SEL_TOPN, n_cb + 1) - 1
    assert k_cache >= 2
    P = min(pages_per_step, n_pages)
    assert n_pages % P == 0
    nch = ckv.shape[1]
    nr = N_HEADS * t_new
    per_seq = lambda r, w: pl.BlockSpec((1, r, w), lambda b, j, pt: (b, 0, 0))
    page = lambda k: pl.BlockSpec((1, KV_W, PAGE_SIZE), lambda b, j, pt: (pt[b, j * P + k], 0, 0))
    kern = functools.partial(_nsa_sample_kernel, n_pg=P, t_new=t_new, past=past, n_cb=n_cb, k_cache=k_cache)
    return pl.pallas_call(
        kern,
        grid_spec=pltpu.PrefetchScalarGridSpec(
            num_scalar_prefetch=1,
            grid=(n_seq, n_pages // P),
            in_specs=[page(k) for k in range(P)] + [
                per_seq(t_new, QW), per_seq(t_new, QW), per_seq(t_new, LANES), per_seq(nch, KV_W),
                per_seq(t_new, KV_W), per_seq(KV_W, WINDOW), per_seq(t_new, KV_W),
                pl.BlockSpec((nch, LANES), lambda b, j, pt: (0, 0), pipeline_mode=pl.Buffered(1))],
            out_specs=[per_seq(t_new, ATTN_W), per_seq(WINDOW, KV_W)],
            scratch_shapes=[pltpu.VMEM((nr, 2 * LANES), BF16)] + [pltpu.VMEM((nr, LANES), F32)] * 5),
        out_shape=[jax.ShapeDtypeStruct((n_seq, t_new, ATTN_W), F32),
                   jax.ShapeDtypeStruct((n_seq, WINDOW, KV_W), F32)],
        compiler_params=_params("arbitrary", "arbitrary"),
        name="nsa_sample",
    )(page_table, *([cache_sel] * P), q, qr, gates, ckv, sel_new, cache_win, win_new, _overlap_t(nch, LANES).T)


def _pool_windows(ext_ref, n_rows, pos, w_ref, lead):
    outs = []
    for gi, w in enumerate(POOL_WINDOWS):
        lanes = slice(gi * POOL_CH, (gi + 1) * POOL_CH)
        tot = None
        for k in range(w):
            v = ext_ref[lead + (slice(HALO - k, HALO - k + n_rows), lanes)]
            tot = v if tot is None else tot + v
        cnt = jnp.minimum(pos + 1, w).astype(F32)
        d = tot / cnt - ext_ref[lead + (slice(HALO, HALO + n_rows), lanes)]
        outs.append((d, w_ref[gi]))
    return outs


def _pool_prompt_kernel(x_ref, h_ref, w_ref, o_ref, ext_ref, *, tm):
    i = pl.program_id(1)
    ext_ref[0:HALO, :] = jnp.where(i == 0, 0.0, h_ref[...])
    ext_ref[HALO:, :] = x_ref[...]
    pos = i * tm + lax.broadcasted_iota(jnp.int32, (tm, 1), 0)
    res = [_dot(d.astype(BF16), w) for d, w in _pool_windows(ext_ref, tm, pos, w_ref, ())]
    o_ref[...] = jnp.concatenate(res, axis=1)


def _pool_prompt(pool_in, w_pool_b, n_seq, tm):
    n = pool_in.shape[0]
    nt = n // n_seq // tm
    hb = tm // HALO
    return pl.pallas_call(
        functools.partial(_pool_prompt_kernel, tm=tm),
        grid=(n_seq, nt),
        in_specs=[pl.BlockSpec((tm, POOL_W), lambda b, i: (b * nt + i, 0)),
                  pl.BlockSpec((HALO, POOL_W), lambda b, i: (jnp.maximum((b * nt + i) * hb - 1, 0), 0)),
                  _const_spec((N_POOL_GROUPS, POOL_CH, POOL_CH))],
        out_specs=pl.BlockSpec((tm, POOL_W), lambda b, i: (b * nt + i, 0)),
        out_shape=jax.ShapeDtypeStruct((n, POOL_W), F32),
        scratch_shapes=[pltpu.VMEM((HALO + tm, POOL_W), F32)],
        compiler_params=_params("parallel", "arbitrary"),
        name="pool_prompt",
    )(pool_in, pool_in, w_pool_b)


def _pool_sample_kernel(x_ref, st_ref, w_ref, o_ref, ns_ref, ext_ref, *, t_new, past):
    nb = x_ref.shape[0]
    ext_ref[:, HALO - POOL_HIST:HALO, :] = st_ref[...]
    ext_ref[:, HALO:, :] = x_ref[...]
    pos = past + lax.broadcasted_iota(jnp.int32, (1, t_new, 1), 1)
    res = [_dot(d.reshape(nb * t_new, POOL_CH).astype(BF16), w)
           for d, w in _pool_windows(ext_ref, t_new, pos, w_ref, (slice(None),))]
    o_ref[...] = jnp.concatenate(res, axis=1)
    ns_ref[...] = ext_ref[:, HALO + t_new - POOL_HIST:HALO + t_new, :]


def _pool_sample(pool_in, state, w_pool_b, past, nb):
    n_seq, t_new, _ = pool_in.shape
    assert t_new % 8 == 0 and n_seq % nb == 0 and past >= POOL_HIST
    return pl.pallas_call(
        functools.partial(_pool_sample_kernel, t_new=t_new, past=past),
        grid=(n_seq // nb,),
        in_specs=[pl.BlockSpec((nb, t_new, POOL_W), lambda i: (i, 0, 0)),
                  pl.BlockSpec((nb, POOL_HIST, POOL_W), lambda i: (i, 0, 0)),
                  _const_spec((N_POOL_GROUPS, POOL_CH, POOL_CH))],
        out_specs=[pl.BlockSpec((nb * t_new, POOL_W), lambda i: (i, 0)),
                   pl.BlockSpec((nb, POOL_HIST, POOL_W), lambda i: (i, 0, 0))],
        out_shape=[jax.ShapeDtypeStruct((n_seq * t_new, POOL_W), F32),
                   jax.ShapeDtypeStruct((n_seq, POOL_HIST, POOL_W), F32)],
        scratch_shapes=[pltpu.VMEM((nb, HALO + t_new, POOL_W), F32)],
        compiler_params=_params("parallel"),
        name="pool_sample",
    )(pool_in, state, w_pool_b)


def _mix_kernel(x_ref, a_ref, m_ref, ga_ref, gp_ref, wo_ref, g2_ref, wq_ref, x1_ref, qm_ref):
    mix = jnp.concatenate([_rms(a_ref[...], ga_ref[...]), _rms(m_ref[...], gp_ref[...])], axis=1)
    x1 = x_ref[...] + _dot(mix.astype(BF16), wo_ref[...])
    x1_ref[...] = x1
    qm_ref[...] = _dot(_rms(x1, g2_ref[...]).astype(BF16), wq_ref[...]).astype(BF16)


def _mix(x, a, m, ga, gp, w_out_b, g2, w_mq_b, tm):
    n = x.shape[0]
    row = lambda w: pl.BlockSpec((tm, w), lambda i: (i, 0))
    return pl.pallas_call(
        _mix_kernel,
        grid=(n // tm,),
        in_specs=[row(D_MODEL), row(ATTN_W), row(POOL_W), _const_spec((1, ATTN_W)), _const_spec((1, POOL_W)),
                  _const_spec((D_MODEL, D_MODEL)), _const_spec((1, D_MODEL)), _const_spec((D_MODEL, MEM_W))],
        out_specs=[row(D_MODEL), row(MEM_W)],
        out_shape=[jax.ShapeDtypeStruct((n, D_MODEL), F32), jax.ShapeDtypeStruct((n, MEM_W), BF16)],
        compiler_params=_params("parallel"),
        name="mix_out",
    )(x, a, m, ga, gp, w_out_b, g2, w_mq_b)


def _cross_kernel(q_ref, kv_ref, o_ref, *, n_mem_blk, rows, split_heads):
    for i in range(n_mem_blk):
        outs = []
        for h in range(MEM_HEADS):
            q = q_ref[i * rows:(i + 1) * rows, h * MEM_HEAD_DIM:(h + 1) * MEM_HEAD_DIM]
            if split_heads:
                k = kv_ref[i, pl.ds(h, N_MEM, stride=2 * MEM_HEADS), :].astype(BF16)
                v = kv_ref[i, pl.ds(MEM_HEADS + h, N_MEM, stride=2 * MEM_HEADS), :].astype(BF16)
            else:
                k = kv_ref[i, :, h * MEM_HEAD_DIM:(h + 1) * MEM_HEAD_DIM].astype(BF16)
                v = kv_ref[i, :, MEM_W + h * MEM_HEAD_DIM:MEM_W + (h + 1) * MEM_HEAD_DIM].astype(BF16)
            s = _dot_nt(q, k) * (MEM_HEAD_DIM ** -0.5)
            e = jnp.exp(s - jnp.max(s, axis=-1, keepdims=True))
            o = _dot(e.astype(BF16), v) / jnp.sum(e, axis=-1, keepdims=True)
            outs.append(o)
        o_ref[i * rows:(i + 1) * rows, :] = jnp.concatenate(outs, axis=1).astype(BF16)


def _cross(qm, mem_kv, rows, n_mem_blk, tiles_per_mem):
    n = qm.shape[0]
    tm = rows * n_mem_blk
    tail = mem_kv.shape[1:]
    return pl.pallas_call(
        functools.partial(_cross_kernel, n_mem_blk=n_mem_blk, rows=rows, split_heads=tail[-1] == MEM_HEAD_DIM),
        grid=(n // tm,),
        in_specs=[pl.BlockSpec((tm, MEM_W), lambda i: (i, 0)),
                  pl.BlockSpec((n_mem_blk,) + tail, lambda i: (i // tiles_per_mem,) + (0,) * len(tail))],
        out_specs=pl.BlockSpec((tm, MEM_W), lambda i: (i, 0)),
        out_shape=jax.ShapeDtypeStruct((n, MEM_W), BF16),
        compiler_params=_params("parallel"),
        name="cross_attn",
    )(qm, mem_kv)


def _ffn_kernel(x1_ref, o_ref, wmo_ref, g3_ref, wup_ref, wdn_ref, gf_ref, y_ref):
    x2 = x1_ref[...] + _dot(o_ref[...], wmo_ref[...])
    xn = _rms(x2, g3_ref[...]).astype(BF16)
    acc = x2
    for c in range(D_FF // FF_CHUNK):
        h = jnp.maximum(_dot(xn, wup_ref[:, c * FF_CHUNK:(c + 1) * FF_CHUNK]), 0.0)
        acc = acc + _dot((h * h).astype(BF16), wdn_ref[c * FF_CHUNK:(c + 1) * FF_CHUNK, :])
    y_ref[...] = _rms(acc, gf_ref[...])


def _ffn(x1, o, w_mo_b, g3, w_up_b, w_down_b, g_final, tm):
    n = x1.shape[0]
    row = lambda w: pl.BlockSpec((tm, w), lambda i: (i, 0))
    return pl.pallas_call(
        _ffn_kernel,
        grid=(n // tm,),
        in_specs=[row(D_MODEL), row(MEM_W), _const_spec((MEM_W, D_MODEL)), _const_spec((1, D_MODEL)),
                  _const_spec((D_MODEL, D_FF)), _const_spec((D_FF, D_MODEL)), _const_spec((1, D_MODEL))],
        out_specs=row(D_MODEL),
        out_shape=jax.ShapeDtypeStruct((n, D_MODEL), F32),
        compiler_params=_params("parallel"),
        name="ffn",
    )(x1, o, w_mo_b, g3, w_up_b, w_down_b, g_final)


def _tile(n, pref):
    t = min(n, pref)
    assert n % t == 0
    return t


def kernel(x_prompt, x_sample, cache_cmp_kv, cache_sel_kv, cache_win_kv, state_pool, cache_mem_kv, page_table, mem_prompt, w_in, pe_cmp, w_cmp1, w_cmp2, w_pool, g_attn_out, pool_scale, w_out, g_norm1, g_norm2, g_mem, w_mq, w_mkv, w_mo, g_norm3, w_up, w_down, g_final):
    assert w_in.shape[0] == 1, "single-layer trunk"
    B, S, _ = x_prompt.shape
    DB, T, _ = x_sample.shape
    past = page_table.shape[1] * PAGE_SIZE
    kv5 = (2, N_KV_HEADS, HEAD_DIM)
    r2 = lambda v: v.reshape(1, -1)

    w_in_p = _permute_w_in(w_in[0])
    w_out_b, w_mq_b, w_mkv_b, w_mo_b = (w[0].astype(BF16) for w in (w_out, w_mq, w_mkv, w_mo))
    w_up_b, w_down_b, w_pool_b = w_up[0].astype(BF16), w_down[0].astype(BF16), w_pool[0].astype(BF16)
    g1, g2, g3, gm, ga, gp, gf = (r2(v) for v in (g_norm1[0], g_norm2[0], g_norm3[0], g_mem[0], g_attn_out[0],
                                                  pool_scale[0], g_final))
    cw = _compress_weights(pe_cmp[0], w_cmp1[0], w_cmp2[0])

    np_tok = B * S
    tm_p = _tile(S, 512)
    xp = x_prompt.reshape(np_tok, D_MODEL)
    q, qr, cmp_t, sel_t, win_t, sel_b, win_b, gates, pool_in = _in_proj(
        xp, g1, w_in_p, jnp.tile(jnp.arange(S), B), tm_p, n_seq=B)
    mem_f, mem_b = _mem_kv(mem_prompt.reshape(B * N_MEM, D_MODEL), gm, w_mkv_b, _tile(B * N_MEM, 256))
    ckv = _compress_seq(cmp_t, cw, 64)
    a = _nsa_prompt(q, qr, gates, ckv, sel_b, win_b, B)
    m = _pool_prompt(pool_in, w_pool_b, B, tm_p)
    x1, qm = _mix(xp, a, m, ga, gp, w_out_b, g2, w_mq_b, tm_p)
    o = _cross(qm, mem_b.reshape(B, N_MEM, 2 * MEM_W), tm_p, 1, S // tm_p)
    y_p = _ffn(x1, o, w_mo_b, g3, w_up_b, w_down_b, gf, tm_p)
    rows_first = lambda t: t.reshape((B,) + kv5 + (t.shape[-1],)).transpose(0, 4, 1, 2, 3)[None]
    p_state = (rows_first(cmp_t), rows_first(sel_t), rows_first(win_t[:, :, -min(WINDOW, S):]),
               pool_in.reshape(B, S, POOL_W)[None, :, -POOL_HIST:],
               mem_f.reshape(1, B, N_MEM, 2, MEM_HEADS, MEM_HEAD_DIM))

    ns_tok = DB * T
    tm_s = _tile(ns_tok, 512)
    xs = x_sample.reshape(ns_tok, D_MODEL)
    q, qr, cmp_kv, sel_kv, win_kv, _, _, gates, pool_in = _in_proj(
        xs, g1, w_in_p, jnp.tile(past + jnp.arange(T), DB), tm_s)
    r3 = lambda v: v.reshape(DB, T, v.shape[-1])
    ckv = _compress_paged(_pages_t(cache_cmp_kv[0]), page_table, cw, 64)
    a, s_win = _nsa_sample(r3(q), r3(qr), r3(gates), ckv, _pages_t(cache_sel_kv[0]), page_table,
                           r3(sel_kv), _pages_t(cache_win_kv[0]), r3(win_kv), 32)
    m, s_pool = _pool_sample(r3(pool_in), state_pool[0], w_pool_b, past, _tile(DB, 16))
    x1, qm = _mix(xs, a.reshape(ns_tok, ATTN_W), m, ga, gp, w_out_b, g2, w_mq_b, tm_s)
    o = _cross(qm, cache_mem_kv[0].reshape(DB, N_MEM * 2 * MEM_HEADS, MEM_HEAD_DIM), T, _tile(DB, 4), 1)
    y_s = _ffn(x1, o, w_mo_b, g3, w_up_b, w_down_b, gf, tm_s)
    s_state = (cmp_kv.reshape((1, DB, T) + kv5), sel_kv.reshape((1, DB, T) + kv5),
               s_win.reshape((1, DB, WINDOW) + kv5), s_pool[None])

    return (y_p.reshape(B, S, D_MODEL), y_s.reshape(DB, T, D_MODEL)) + p_state + s_state
```

```python
import functools

import jax
import jax.numpy as jnp
from jax import lax
from jax.experimental import pallas as pl
from jax.experimental.pallas import tpu as pltpu

D_MODEL = 1024
PAGE_SIZE = 128
ATTN_W = 512
POOL_W = 512
HEAD_DIM = 64
N_HEADS = 8
N_KV_HEADS = 2
HEADS_PER_KV = N_HEADS // N_KV_HEADS
SCALE = HEAD_DIM ** -0.5
LOG2E = 1.4426950408889634
Q_SCALE = SCALE * LOG2E
ROT_DIM = HEAD_DIM // 4
ROPE_THETA = 500000.0
CMP_LEN = 32
CMP_STRIDE = 16
CMP_HID = HEAD_DIM
SEL_BLOCK = 64
SEL_TOPN = 16
WINDOW = 512
FORCE_BONUS = 1000.0
POOL_WINDOWS = (2, 4, 8, 16)
N_POOL_GROUPS = len(POOL_WINDOWS)
POOL_CH = POOL_W // N_POOL_GROUPS
POOL_HIST = max(POOL_WINDOWS) - 1
N_MEM = 256
MEM_HEADS = 4
MEM_HEAD_DIM = 128
MEM_W = MEM_HEADS * MEM_HEAD_DIM
D_FF = 4 * D_MODEL
EPS = 1e-6
KV_W = 2 * N_KV_HEADS * HEAD_DIM
N_GATES = 3 * N_HEADS
LANES = 128
VMEM_LIMIT = 56 * 1024 * 1024
QW = N_HEADS * LANES
TQ = 128
NEG = -1e30
CHUNK_W = CMP_STRIDE * KV_W
CMP_R = CMP_LEN // CMP_STRIDE
CHUNKS_PER_PAGE = PAGE_SIZE // CMP_STRIDE
HALO = 16
FF_CHUNK = 1024

F32 = jnp.float32
BF16 = jnp.bfloat16


def _rms(x, g):
    return x * lax.rsqrt(jnp.mean(x * x, axis=-1, keepdims=True) + EPS) * g


def _dot(a, b):
    return jnp.dot(a, b, preferred_element_type=F32)


def _dot_nt(a, b):
    return lax.dot_general(a, b, (((1,), (1,)), ((), ())), preferred_element_type=F32)


def _const_spec(shape):
    n = len(shape)
    return pl.BlockSpec(shape, lambda *_: (0,) * n, pipeline_mode=pl.Buffered(1))


def _params(*sem):
    return pltpu.CompilerParams(dimension_semantics=sem, vmem_limit_bytes=VMEM_LIMIT)


X_Q, X_CMP, X_SEL, X_WIN, X_POOL, X_GATE = 0, 1024, 1280, 1536, 1792, 2304
X_IN = X_GATE + LANES


def _permute_w_in(w_in):
    q, kc, ks, kw, gl, pool = jnp.split(w_in, (512, 768, 1024, 1280, 1280 + N_GATES), axis=1)
    z = jnp.zeros((D_MODEL, HEAD_DIM), w_in.dtype)
    qx = []
    for h in range(N_HEADS):
        qh = q[:, h * HEAD_DIM:(h + 1) * HEAD_DIM]
        qx += [qh, z] if h // HEADS_PER_KV == 0 else [z, qh]
    pad = jnp.zeros((D_MODEL, LANES - N_GATES), w_in.dtype)
    return jnp.concatenate(qx + [kc, ks, kw, pool, gl, pad], axis=1).astype(BF16)


def _rope_tables(pos):
    half = ROT_DIM // 2
    inv = ROPE_THETA ** (-jnp.arange(half, dtype=F32) / half)
    ang = pos.astype(F32)[:, None] * inv[None, :]
    cos, sin = jnp.cos(ang), jnp.sin(ang)
    z = jnp.zeros_like(cos)
    rest0 = jnp.zeros((pos.shape[0], HEAD_DIM - ROT_DIM), F32)
    cs = jnp.concatenate([cos, cos, rest0 + 1.0], axis=1)
    sa = jnp.concatenate([-sin, z, rest0], axis=1)
    sb = jnp.concatenate([z, sin, rest0], axis=1)
    return tuple(jnp.tile(t, (1, LANES // HEAD_DIM)) for t in (cs, sa, sb))


def _rope(v, cs, sa, sb):
    n = v.shape[1] // LANES
    w = v.shape[1]
    cs, sa, sb = (jnp.tile(t, (1, n)) for t in (cs, sa, sb))
    return v * cs + pltpu.roll(v, w - ROT_DIM // 2, 1) * sa + pltpu.roll(v, ROT_DIM // 2, 1) * sb


def _in_proj_kernel(x_ref, g_ref, w_ref, cs_ref, sa_ref, sb_ref,
                    q_ref, qr_ref, cmp_ref, sel_ref, win_ref, selb_ref, winb_ref, gate_ref, pool_ref, *, feature_major):
    xb = _rms(x_ref[...], g_ref[...]).astype(BF16)
    cs, sa, sb = cs_ref[...], sa_ref[...], sb_ref[...]

    def put_state(ref, v):
        if feature_major:
            ref[0] = jnp.concatenate([v[:, c * LANES:(c + 1) * LANES].T for c in range(KV_W // LANES)], axis=0)
        else:
            ref[...] = v

    q = _dot(xb, w_ref[:, X_Q:X_CMP])
    q_ref[...] = (q * Q_SCALE).astype(BF16)
    qr_ref[...] = (_rope(q, cs, sa, sb) * Q_SCALE).astype(BF16)
    put_state(cmp_ref, _dot(xb, w_ref[:, X_CMP:X_SEL]))
    for lo, f_ref, b_ref in ((X_SEL, sel_ref, selb_ref), (X_WIN, win_ref, winb_ref)):
        kv = _dot(xb, w_ref[:, lo:lo + KV_W])
        kv = jnp.concatenate([_rope(kv[:, :LANES], cs, sa, sb), kv[:, LANES:]], axis=1)
        put_state(f_ref, kv)
        b_ref[...] = kv.astype(BF16)
    pool_ref[...] = _dot(xb, w_ref[:, X_POOL:X_GATE])
    gate_ref[...] = jax.nn.sigmoid(_dot(xb, w_ref[:, X_GATE:X_IN]))


def _in_proj(x, g1, w_in_p, pos_rows, tm, n_seq=None):
    n = x.shape[0]
    cs, sa, sb = _rope_tables(pos_rows)
    row = lambda w: pl.BlockSpec((tm, w), lambda i: (i, 0))
    tab = pl.BlockSpec((tm, LANES), lambda i: (i % (pos_rows.shape[0] // tm), 0))
    sds = lambda w, dt: jax.ShapeDtypeStruct((n, w), dt)
    if n_seq is None:
        st_spec, st_shape = row(KV_W), sds(KV_W, F32)
    else:
        nt = n // n_seq // tm
        st_spec = pl.BlockSpec((1, KV_W, tm), lambda i: (i // nt, 0, i % nt))
        st_shape = jax.ShapeDtypeStruct((n_seq, KV_W, n // n_seq), F32)
    return pl.pallas_call(
        functools.partial(_in_proj_kernel, feature_major=n_seq is not None),
        grid=(n // tm,),
        in_specs=[row(D_MODEL), _const_spec((1, D_MODEL)), _const_spec((D_MODEL, X_IN)),
                  tab, tab, tab],
        out_specs=[row(QW), row(QW), st_spec, st_spec, st_spec, row(KV_W), row(KV_W), row(LANES), row(POOL_W)],
        out_shape=[sds(QW, BF16), sds(QW, BF16), st_shape, st_shape, st_shape,
                   sds(KV_W, BF16), sds(KV_W, BF16), sds(LANES, F32), sds(POOL_W, F32)],
        compiler_params=_params("parallel"),
        name="in_proj",
    )(x, g1, w_in_p, cs, sa, sb)


def _mem_kv_kernel(m_ref, g_ref, w_ref, o_ref, ob_ref):
    kv = _dot(_rms(m_ref[...], g_ref[...]).astype(BF16), w_ref[...])
    o_ref[...] = kv
    ob_ref[...] = kv.astype(BF16)


def _mem_kv(mem, g_mem, w_mkv_b, tm):
    n = mem.shape[0]
    row = lambda w: pl.BlockSpec((tm, w), lambda i: (i, 0))
    return pl.pallas_call(
        _mem_kv_kernel,
        grid=(n // tm,),
        in_specs=[row(D_MODEL), _const_spec((1, D_MODEL)), _const_spec((D_MODEL, 2 * MEM_W))],
        out_specs=[row(2 * MEM_W), row(2 * MEM_W)],
        out_shape=[jax.ShapeDtypeStruct((n, 2 * MEM_W), F32), jax.ShapeDtypeStruct((n, 2 * MEM_W), BF16)],
        compiler_params=_params("parallel"),
        name="mem_kv",
    )(mem, g_mem, w_mkv_b)


def _compress_weights(pe, w1, w2):
    w1r = w1.reshape(2, CMP_R, CMP_STRIDE, HEAD_DIM, CMP_HID)
    eye_c = jnp.eye(2, dtype=w1.dtype)
    eye_g = jnp.eye(N_KV_HEADS, dtype=w1.dtype)
    w1big = jnp.einsum('crsdh,cx,gy->scgdrxyh', w1r, eye_c, eye_g).reshape(CHUNK_W, CMP_R * KV_W)
    pe_term = jnp.einsum('cld,cldh->ch', pe, w1.reshape(2, CMP_LEN, HEAD_DIM, CMP_HID))
    bias = jnp.broadcast_to(pe_term[:, None, :], (2, N_KV_HEADS, CMP_HID)).reshape(1, KV_W)
    w2big = jnp.einsum('chd,cx,gy->cghxyd', w2, eye_c, eye_g).reshape(KV_W, KV_W)
    return w1big.astype(BF16), bias, w2big.astype(BF16)


def _compress_kernel(*refs, n_in, ppr, has_pt):
    if has_pt:
        refs = refs[1:]
    x_refs = refs[:n_in]
    w1_ref, b_ref, w2_ref, o_ref, carry_ref = refs[n_in:]

    @pl.when(pl.program_id(1) == 0)
    def _():
        carry_ref[...] = jnp.zeros_like(carry_ref)

    r_i = lax.broadcasted_iota(jnp.int32, (PAGE_SIZE, PAGE_SIZE), 0)
    t_i = lax.broadcasted_iota(jnp.int32, (PAGE_SIZE, PAGE_SIZE), 1)
    src = CMP_STRIDE * (r_i & (CHUNKS_PER_PAGE - 1)) + lax.shift_right_logical(r_i, 3)
    pick = jnp.where(t_i == src, 1.0, 0.0).astype(BF16)
    pages = [_dot_nt(pick, r[0, :, j * PAGE_SIZE:(j + 1) * PAGE_SIZE].astype(BF16))
             for r in x_refs for j in range(ppr)]
    cols = [jnp.concatenate([pg[s * CHUNKS_PER_PAGE:(s + 1) * CHUNKS_PER_PAGE, :] for pg in pages], axis=0)
            for s in range(CMP_STRIDE)]
    x = jnp.concatenate(cols, axis=1)
    u = _dot(x.astype(BF16), w1_ref[...])
    u0, u1 = u[:, :KV_W], u[:, KV_W:]
    rows = u0.shape[0]
    row = lax.broadcasted_iota(jnp.int32, u0.shape, 0)
    prev = jnp.where(row == 0, carry_ref[0:1, :], pltpu.roll(u0, 1, 0))
    carry_ref[...] = jnp.broadcast_to(u0[rows - 1:rows, :], carry_ref.shape)
    h = prev + u1 + b_ref[...]
    o_ref[0] = _dot(jax.nn.gelu(h).astype(BF16), w2_ref[...]).astype(BF16)


def _compress_call(n_seq, n_pages, n_in, ppr, has_pt, x_spec_fn):
    P = n_in * ppr
    assert n_pages % P == 0
    grid = (n_seq, n_pages // P)
    in_specs = [x_spec_fn(k) for k in range(n_in)] + [
        _const_spec((CHUNK_W, CMP_R * KV_W)), _const_spec((1, KV_W)), _const_spec((KV_W, KV_W))]
    out_spec = pl.BlockSpec((1, P * CHUNKS_PER_PAGE, KV_W), (lambda b, i, *_: (b, i, 0)))
    kern = functools.partial(_compress_kernel, n_in=n_in, ppr=ppr, has_pt=has_pt)
    common = dict(out_shape=jax.ShapeDtypeStruct((n_seq, n_pages * CHUNKS_PER_PAGE, KV_W), BF16),
                  compiler_params=_params("parallel", "arbitrary"), name="compress")
    scratch = [pltpu.VMEM((8, KV_W), F32)]
    if has_pt:
        return pl.pallas_call(kern, grid_spec=pltpu.PrefetchScalarGridSpec(
            num_scalar_prefetch=1, grid=grid, in_specs=in_specs, out_specs=out_spec, scratch_shapes=scratch), **common)
    return pl.pallas_call(kern, grid=grid, in_specs=in_specs, out_specs=out_spec, scratch_shapes=scratch, **common)


def _compress_seq(cmp_t, cw, pages_per_step):
    n_seq, _, seq = cmp_t.shape
    n_pages = seq // PAGE_SIZE
    P = min(pages_per_step, n_pages)
    spec = lambda k: pl.BlockSpec((1, KV_W, P * PAGE_SIZE), lambda b, i: (b, 0, i))
    return _compress_call(n_seq, n_pages, 1, P, False, spec)(cmp_t, *cw)


def _pages_t(cache):
    return cache.transpose(0, 2, 3, 4, 1).reshape(cache.shape[0], KV_W, cache.shape[1])


def _compress_paged(cache_t, page_table, cw, pages_per_step):
    n_seq, n_pages = page_table.shape
    P = min(pages_per_step, n_pages)
    spec = lambda k: pl.BlockSpec((1, KV_W, PAGE_SIZE), lambda b, i, pt: (pt[b, i * P + k], 0, 0))
    return _compress_call(n_seq, n_pages, P, 1, True, spec)(page_table, *([cache_t] * P), *cw)


def _overlap_t(nch, n_sel_pad):
    m = jnp.arange(nch)[None, :]
    j = jnp.arange(n_sel_pad)[:, None]
    lo = (m - 1) * CMP_STRIDE
    ov = jnp.maximum(jnp.minimum(lo + CMP_LEN, (j + 1) * SEL_BLOCK) - jnp.maximum(lo, j * SEL_BLOCK), 0)
    return jnp.where(m >= 1, ov.astype(F32) / CMP_LEN, 0.0).astype(BF16)


def _heads_to_rows(x):
    return jnp.concatenate([x[:, h * LANES:(h + 1) * LANES] for h in range(N_HEADS)], axis=0)


def _topk_bias(imp, pos, n_blk, k_top, axis):
    j = lax.broadcasted_iota(jnp.int32, imp.shape, axis)
    cur = lax.shift_right_logical(pos, 6)
    forced = (j == 0) | (j == cur) | (j == cur - 1)
    score = jnp.where(j * SEL_BLOCK <= pos, imp + jnp.where(forced, FORCE_BONUS, 0.0), -1e9)
    score = jnp.where(j < n_blk, score, -jnp.inf)
    bias = jnp.full(imp.shape, NEG, F32)
    for _ in range(k_top):
        mx = jnp.max(score, axis=axis, keepdims=True)
        idx = jnp.min(jnp.where(score == mx, j, LANES), axis=axis, keepdims=True)
        hit = j == idx
        bias = jnp.where(hit, 0.0, bias)
        score = jnp.where(hit, -jnp.inf, score)
    return bias


def _split_dot(dot, ps):
    hi = ps.astype(BF16)
    lo = (ps - hi.astype(F32)).astype(BF16)
    return dot(hi) + dot(lo)


ONES_ROWS = 16
VT_ROWS = LANES + ONES_ROWS


def _ones_rows(n):
    return jnp.where(lax.broadcasted_iota(jnp.int32, (ONES_ROWS, n), 0) == 0, 1.0, 0.0).astype(BF16)


def _nsa_prompt_kernel(q_ref, qr_ref, g_ref, ckv_ref, ksel_ref, kwin_ref, ovt_ref, a_ref,
                       kx_ref, vts_ref, vtw_ref, lc_ref, *, tk, n_sel, k_top):
    i = pl.program_id(1)
    q0 = pl.multiple_of(i * TQ, TQ)
    nc = ckv_ref.shape[1]

    @pl.when(i == 0)
    def _():
        kx_ref[...] = jnp.zeros_like(kx_ref)
        vts_ref[...] = jnp.zeros_like(vts_ref)
        vtw_ref[...] = jnp.zeros_like(vtw_ref)
        lc_ref[0:LANES, :] = ckv_ref[0, :, LANES:2 * LANES].astype(F32).T.astype(BF16)
        lc_ref[LANES:2 * LANES, :] = ovt_ref[...]
        lc_ref[2 * LANES:, :] = _ones_rows(nc)

    key_t = q0 + lax.broadcasted_iota(jnp.int32, (TQ, 1), 0)
    onehot = jnp.where(lax.shift_right_logical(key_t, 6) == lax.broadcasted_iota(jnp.int32, (TQ, LANES), 1), 1.0, 0.0)
    kx_ref[pl.ds(q0, TQ), :] = jnp.concatenate([ksel_ref[pl.ds(q0, TQ), 0:LANES], onehot.astype(BF16)], axis=1)
    for k_ref, vt_ref in ((ksel_ref, vts_ref), (kwin_ref, vtw_ref)):
        v = k_ref[pl.ds(q0, TQ), LANES:2 * LANES].astype(F32)
        vt_ref[:, pl.ds(q0, TQ)] = jnp.concatenate([v.T.astype(BF16), _ones_rows(TQ)], axis=0)

    nl = N_HEADS * TQ
    lane = lax.broadcasted_iota(jnp.int32, (1, nl), 1)
    pos = q0 + (lane & (TQ - 1))
    qc = _heads_to_rows(q_ref)
    qr = _heads_to_rows(qr_ref)

    def weights(s, mask):
        s = jnp.where(mask, s, NEG)
        return jnp.where(mask, jnp.exp2(s - jnp.max(s, axis=0, keepdims=True)), 0.0).astype(BF16)

    m_idx = lax.broadcasted_iota(jnp.int32, (nc, 1), 0)
    c_end = jnp.where(m_idx >= 1, m_idx * CMP_STRIDE + (CMP_STRIDE - 1), 2 ** 30)
    acc = _dot(lc_ref[...], weights(_dot_nt(ckv_ref[0, :, 0:LANES], qc), c_end <= pos))
    rden = 1.0 / jnp.maximum(acc[2 * LANES:2 * LANES + 1, :], 1e-30)
    oc_t = acc[0:LANES, :] * rden
    imp_h = acc[LANES:2 * LANES, :] * rden

    imp_t = jnp.concatenate(
        [sum(imp_h[:, (g * HEADS_PER_KV + hh) * TQ:(g * HEADS_PER_KV + hh + 1) * TQ] for hh in range(HEADS_PER_KV))
         for g in range(N_KV_HEADS)], axis=1)
    bias_t = _topk_bias(imp_t, pos[:, :N_KV_HEADS * TQ], n_sel, k_top, 0)
    biases = [bias_t[:, g * TQ:(g + 1) * TQ].T.astype(BF16) for g in range(N_KV_HEADS)]
    bias_rows = jnp.concatenate([biases[h // HEADS_PER_KV] for h in range(N_HEADS)], axis=0)
    qs = jnp.concatenate([qr, bias_rows], axis=1)

    def sel_chunk(c0, masked, carry):
        m, acc = carry
        s = _dot_nt(kx_ref[pl.ds(c0, tk), :], qs)
        if masked:
            s = jnp.where(c0 + lax.broadcasted_iota(jnp.int32, (tk, 1), 0) <= pos, s, NEG)
        m_new = jnp.maximum(m, jnp.max(s, axis=0, keepdims=True))
        p = jnp.exp2(s - m_new).astype(BF16)
        acc = jnp.exp2(m - m_new) * acc + _dot(vts_ref[:, pl.ds(c0, tk)], p)
        return m_new, acc

    n_full = q0 // tk
    carry = (jnp.full((1, nl), NEG, F32), jnp.zeros((VT_ROWS, nl), F32))
    one = lambda c, cr: sel_chunk(pl.multiple_of(c * tk, tk), False, cr)
    carry = lax.fori_loop(0, n_full // 2, lambda c, cr: one(2 * c + 1, one(2 * c, cr)), carry)
    carry = lax.fori_loop(n_full // 2 * 2, n_full, one, carry)
    _, acc = sel_chunk(pl.multiple_of(n_full * tk, tk), True, carry)
    os_t = acc[0:LANES, :] / jnp.maximum(acc[LANES:LANES + 1, :], 1e-30)

    wk = WINDOW + TQ
    w0 = pl.multiple_of(jnp.maximum(q0 - WINDOW, 0), TQ)
    s = _dot_nt(kwin_ref[pl.ds(w0, wk), 0:LANES], qr)
    kp = w0 + lax.broadcasted_iota(jnp.int32, (wk, 1), 0)
    s = jnp.concatenate([jnp.where((kp[:TQ] <= pos) & (kp[:TQ] >= pos - WINDOW), s[:TQ], NEG),
                         jnp.where(kp[TQ:] <= pos, s[TQ:], NEG)], axis=0)
    acc = _dot(vtw_ref[:, pl.ds(w0, wk)], jnp.exp2(s - jnp.max(s, axis=0, keepdims=True)).astype(BF16))
    ow_t = acc[0:LANES, :] / jnp.maximum(acc[LANES:LANES + 1, :], 1e-30)

    g_t = g_ref[...].T
    outs = []
    for h in range(N_HEADS):
        g = h // HEADS_PER_KV
        blk_h = lambda o: o[g * HEAD_DIM:(g + 1) * HEAD_DIM, h * TQ:(h + 1) * TQ]
        gate = lambda j: g_t[3 * h + j:3 * h + j + 1, :]
        outs.append(gate(0) * blk_h(oc_t) + gate(1) * blk_h(os_t) + gate(2) * blk_h(ow_t))
    a_ref[...] = jnp.concatenate(outs, axis=0).T


def _nsa_prompt(q, qr, gates, ckv, ksel, kwin, n_seq):
    n = q.shape[0]
    seq = n // n_seq
    nch = ckv.shape[1]
    n_sel = seq // SEL_BLOCK
    assert seq % TQ == 0 and seq >= WINDOW + TQ and n_sel <= LANES and seq % SEL_BLOCK == 0
    tk = min(512, seq)
    assert seq % tk == 0
    nt = seq // TQ
    tile = lambda w: pl.BlockSpec((TQ, w), lambda b, i: (b * nt + i, 0))
    whole = lambda: pl.BlockSpec((seq, KV_W), lambda b, i: (b, 0), pipeline_mode=pl.Buffered(1))
    kern = functools.partial(_nsa_prompt_kernel, tk=tk, n_sel=n_sel, k_top=min(SEL_TOPN, n_sel))
    return pl.pallas_call(
        kern,
        grid=(n_seq, nt),
        in_specs=[tile(QW), tile(QW), tile(LANES),
                  pl.BlockSpec((1, nch, KV_W), lambda b, i: (b, 0, 0)),
                  whole(), whole(), _const_spec((LANES, nch))],
        out_specs=tile(ATTN_W),
        out_shape=jax.ShapeDtypeStruct((n, ATTN_W), F32),
        scratch_shapes=[pltpu.VMEM((seq, 2 * LANES), BF16), pltpu.VMEM((VT_ROWS, seq), BF16),
                        pltpu.VMEM((VT_ROWS, seq), BF16), pltpu.VMEM((2 * LANES + ONES_ROWS, nch), BF16)],
        compiler_params=_params("arbitrary", "arbitrary"),
        name="nsa_prompt",
    )(q, qr, gates, ckv, ksel, kwin, _overlap_t(nch, LANES))


def _softmax_rows2(s_a, mask_a, s_b, mask_b):
    s_a = jnp.where(mask_a, s_a, NEG)
    s_b = jnp.where(mask_b, s_b, NEG)
    mx = jnp.maximum(jnp.max(s_a, axis=1, keepdims=True), jnp.max(s_b, axis=1, keepdims=True))
    e_a = jnp.where(mask_a, jnp.exp2(s_a - mx), 0.0)
    e_b = jnp.where(mask_b, jnp.exp2(s_b - mx), 0.0)
    den = jnp.sum(e_a, axis=1, keepdims=True) + jnp.sum(e_b, axis=1, keepdims=True)
    return e_a, e_b, jnp.maximum(den, 1e-30)


def _pad_rows(x, rows):
    return jnp.concatenate([x, jnp.zeros((rows - x.shape[0], x.shape[1]), x.dtype)], axis=0)


def _nsa_sample_kernel(*refs, n_seq, n_pg, t_new, past, n_cb, k_cache):
    pg_refs = refs[1:1 + n_seq * n_pg]
    (q_ref, qr_ref, g_ref, ckv_ref, seln_ref, cwin_ref, winn_ref, ovl_ref,
     a_ref, swin_ref, qs_ref, oc_ref, ow_ref, m_ref, l_ref, acc_ref) = refs[1 + n_seq * n_pg:]
    j = pl.program_id(1)
    nr = N_HEADS * t_new
    row_t = lax.rem(lax.broadcasted_iota(jnp.int32, (nr, 1), 0), t_new)
    pos = past + row_t
    lane = lax.broadcasted_iota(jnp.int32, (1, LANES), 1)

    def first(b):
        qc = _heads_to_rows(q_ref[b].astype(F32)).astype(BF16)
        qr = _heads_to_rows(qr_ref[b].astype(F32)).astype(BF16)
        nc = ckv_ref.shape[1]
        m_idx = lax.broadcasted_iota(jnp.int32, (1, nc), 1)
        cmask = (m_idx >= 1) & (m_idx * CMP_STRIDE + (CMP_STRIDE - 1) <= pos)
        s = jnp.where(cmask, _dot_nt(qc, ckv_ref[b, :, 0:LANES]), NEG)
        e = jnp.where(cmask, jnp.exp2(s - jnp.max(s, axis=1, keepdims=True)), 0.0)
        p_c = e / jnp.maximum(jnp.sum(e, axis=1, keepdims=True), 1e-30)
        oc_ref[b] = _dot(p_c.astype(BF16), ckv_ref[b, :, LANES:2 * LANES])
        imps = []
        for g in range(N_KV_HEADS):
            ps = sum(p_c[(g * HEADS_PER_KV + hh) * t_new:(g * HEADS_PER_KV + hh + 1) * t_new, :]
                     for hh in range(HEADS_PER_KV))
            imps.append(_split_dot(lambda v: _dot(v, ovl_ref[...]), ps))
        imp_t = _pad_rows(jnp.concatenate(imps, axis=0), LANES).T
        bias = _topk_bias(imp_t, past + lax.rem(lane, t_new), n_cb, k_cache, 0).T
        biases = [bias[g * t_new:(g + 1) * t_new, :] for g in range(N_KV_HEADS)]
        bias_rows = jnp.concatenate([biases[h // HEADS_PER_KV] for h in range(N_HEADS)], axis=0)
        qs_ref[b] = jnp.concatenate([qr, bias_rows.astype(BF16)], axis=1)
        cw = jnp.concatenate([cwin_ref[b, c * LANES:(c + 1) * LANES, :].T for c in range(KV_W // LANES)], axis=1)
        wn = _pad_rows(winn_ref[b], LANES)
        r_idx = lax.broadcasted_iota(jnp.int32, (1, WINDOW), 1)
        e_a, e_b, den = _softmax_rows2(
            _dot_nt(qr, cw[:, 0:LANES].astype(BF16)), r_idx >= row_t,
            _dot_nt(qr, wn[:, 0:LANES].astype(BF16)), lane <= row_t)
        ow_ref[b] = (_dot(e_a.astype(BF16), cw[:, LANES:].astype(BF16))
                     + _dot(e_b.astype(BF16), wn[:, LANES:].astype(BF16))) / den
        swin_ref[b, 0:WINDOW - t_new, :] = cw[t_new:, :]
        swin_ref[b, WINDOW - t_new:, :] = winn_ref[b]
        m_ref[b] = jnp.full(m_ref.shape[1:], NEG, F32)
        l_ref[b] = jnp.zeros(l_ref.shape[1:], F32)
        acc_ref[b] = jnp.zeros(acc_ref.shape[1:], F32)

    def online(b, s, pv):
        m = m_ref[b, :, 0:1]
        m_new = jnp.maximum(m, jnp.max(s, axis=1, keepdims=True))
        alpha = jnp.exp2(m - m_new)
        p = jnp.exp2(s - m_new)
        l_ref[b] = jnp.broadcast_to(alpha * l_ref[b, :, 0:1] + jnp.sum(p, axis=1, keepdims=True), l_ref.shape[1:])
        acc_ref[b] = alpha * acc_ref[b] + pv(p.astype(BF16))
        m_ref[b] = jnp.broadcast_to(m_new, m_ref.shape[1:])

    def pages(b):
        mine = pg_refs[b * n_pg:(b + 1) * n_pg]
        cat = lambda lo: jnp.concatenate([r[0, lo:lo + LANES, :] for r in mine], axis=1).astype(BF16)
        k_t, v_t = cat(0), cat(LANES)
        nk = k_t.shape[1]
        blk = lax.shift_right_logical(j * nk + lax.broadcasted_iota(jnp.int32, (1, nk), 1), 6)
        onehot_t = jnp.where(blk == lax.broadcasted_iota(jnp.int32, (LANES, 1), 0), 1.0, 0.0).astype(BF16)
        online(b, _dot(qs_ref[b], jnp.concatenate([k_t, onehot_t], axis=0)), lambda p: _dot_nt(p, v_t))

    def last(b):
        sn = _pad_rows(seln_ref[b], LANES)
        s = _dot_nt(qs_ref[b, :, 0:LANES], sn[:, 0:LANES].astype(BF16))
        online(b, jnp.where(lane <= row_t, s, NEG), lambda p: _dot(p, sn[:, LANES:].astype(BF16)))
        os = acc_ref[b] / jnp.maximum(l_ref[b, :, 0:1], 1e-30)
        gts = g_ref[b]
        outs = []
        for h in range(N_HEADS):
            g = h // HEADS_PER_KV
            blk_h = lambda o: o[h * t_new:(h + 1) * t_new, g * HEAD_DIM:(g + 1) * HEAD_DIM]
            gate = lambda k: gts[:, 3 * h + k:3 * h + k + 1]
            outs.append(gate(0) * blk_h(oc_ref[b]) + gate(1) * blk_h(os) + gate(2) * blk_h(ow_ref[b]))
        a_ref[b] = jnp.concatenate(outs, axis=1)

    @pl.when(j == 0)
    def _():
        for b in range(n_seq):
            first(b)

    for b in range(n_seq):
        pages(b)

    @pl.when(j == pl.num_programs(1) - 1)
    def _():
        for b in range(n_seq):
            last(b)


def _nsa_sample(q, qr, gates, ckv, cache_sel, page_table, sel_new, cache_win, win_new, pages_per_step, seqs_per_step):
    n_all, t_new = q.shape[:2]
    n_pages = page_table.shape[1]
    past = n_pages * PAGE_SIZE
    n_cb = past // SEL_BLOCK
    assert cache_win.shape[2] == WINDOW and n_cb <= LANES and t_new <= SEL_BLOCK and t_new % 8 == 0
    k_cache = min(SEL_TOPN, n_cb + 1) - 1
    assert k_cache >= 2
    P = min(pages_per_step, n_pages)
    NS = min(seqs_per_step, n_all)
    assert n_pages % P == 0 and n_all % NS == 0
    nch = ckv.shape[1]
    nr = N_HEADS * t_new
    per_seq = lambda r, w: pl.BlockSpec((NS, r, w), lambda b, j, pt: (b, 0, 0))
    page = lambda s, k: pl.BlockSpec((1, KV_W, PAGE_SIZE), lambda b, j, pt: (pt[b * NS + s, j * P + k], 0, 0))
    kern = functools.partial(_nsa_sample_kernel, n_seq=NS, n_pg=P, t_new=t_new, past=past, n_cb=n_cb, k_cache=k_cache)
    return pl.pallas_call(
        kern,
        grid_spec=pltpu.PrefetchScalarGridSpec(
            num_scalar_prefetch=1,
            grid=(n_all // NS, n_pages // P),
            in_specs=[page(s, k) for s in range(NS) for k in range(P)] + [
                per_seq(t_new, QW), per_seq(t_new, QW), per_seq(t_new, LANES), per_seq(nch, KV_W),
                per_seq(t_new, KV_W), per_seq(KV_W, WINDOW), per_seq(t_new, KV_W),
                pl.BlockSpec((nch, LANES), lambda b, j, pt: (0, 0), pipeline_mode=pl.Buffered(1))],
            out_specs=[per_seq(t_new, ATTN_W), per_seq(WINDOW, KV_W)],
            scratch_shapes=[pltpu.VMEM((NS, nr, 2 * LANES), BF16)] + [pltpu.VMEM((NS, nr, LANES), F32)] * 5),
        out_shape=[jax.ShapeDtypeStruct((n_all, t_new, ATTN_W), F32),
                   jax.ShapeDtypeStruct((n_all, WINDOW, KV_W), F32)],
        compiler_params=_params("arbitrary", "arbitrary"),
        name="nsa_sample",
    )(page_table, *([cache_sel] * (NS * P)), q, qr, gates, ckv, sel_new, cache_win, win_new,
      _overlap_t(nch, LANES).T)


def _pool_windows(ext_ref, n_rows, pos, w_ref, lead):
    outs = []
    for gi, w in enumerate(POOL_WINDOWS):
        lanes = slice(gi * POOL_CH, (gi + 1) * POOL_CH)
        tot = None
        for k in range(w):
            v = ext_ref[lead + (slice(HALO - k, HALO - k + n_rows), lanes)]
            tot = v if tot is None else tot + v
        cnt = jnp.minimum(pos + 1, w).astype(F32)
        d = tot / cnt - ext_ref[lead + (slice(HALO, HALO + n_rows), lanes)]
        outs.append((d, w_ref[gi]))
    return outs


def _pool_prompt_kernel(x_ref, h_ref, w_ref, o_ref, ext_ref, *, tm):
    i = pl.program_id(1)
    ext_ref[0:HALO, :] = jnp.where(i == 0, 0.0, h_ref[...])
    ext_ref[HALO:, :] = x_ref[...]
    pos = i * tm + lax.broadcasted_iota(jnp.int32, (tm, 1), 0)
    res = [_dot(d.astype(BF16), w) for d, w in _pool_windows(ext_ref, tm, pos, w_ref, ())]
    o_ref[...] = jnp.concatenate(res, axis=1)


def _pool_prompt(pool_in, w_pool_b, n_seq, tm):
    n = pool_in.shape[0]
    nt = n // n_seq // tm
    hb = tm // HALO
    return pl.pallas_call(
        functools.partial(_pool_prompt_kernel, tm=tm),
        grid=(n_seq, nt),
        in_specs=[pl.BlockSpec((tm, POOL_W), lambda b, i: (b * nt + i, 0)),
                  pl.BlockSpec((HALO, POOL_W), lambda b, i: (jnp.maximum((b * nt + i) * hb - 1, 0), 0)),
                  _const_spec((N_POOL_GROUPS, POOL_CH, POOL_CH))],
        out_specs=pl.BlockSpec((tm, POOL_W), lambda b, i: (b * nt + i, 0)),
        out_shape=jax.ShapeDtypeStruct((n, POOL_W), F32),
        scratch_shapes=[pltpu.VMEM((HALO + tm, POOL_W), F32)],
        compiler_params=_params("parallel", "arbitrary"),
        name="pool_prompt",
    )(pool_in, pool_in, w_pool_b)


def _pool_sample_kernel(x_ref, st_ref, w_ref, o_ref, ns_ref, ext_ref, *, t_new, past):
    nb = x_ref.shape[0]
    ext_ref[:, HALO - POOL_HIST:HALO, :] = st_ref[...]
    ext_ref[:, HALO:, :] = x_ref[...]
    pos = past + lax.broadcasted_iota(jnp.int32, (1, t_new, 1), 1)
    res = [_dot(d.reshape(nb * t_new, POOL_CH).astype(BF16), w)
           for d, w in _pool_windows(ext_ref, t_new, pos, w_ref, (slice(None),))]
    o_ref[...] = jnp.concatenate(res, axis=1)
    ns_ref[...] = ext_ref[:, HALO + t_new - POOL_HIST:HALO + t_new, :]


def _pool_sample(pool_in, state, w_pool_b, past, nb):
    n_seq, t_new, _ = pool_in.shape
    assert t_new % 8 == 0 and n_seq % nb == 0 and past >= POOL_HIST
    return pl.pallas_call(
        functools.partial(_pool_sample_kernel, t_new=t_new, past=past),
        grid=(n_seq // nb,),
        in_specs=[pl.BlockSpec((nb, t_new, POOL_W), lambda i: (i, 0, 0)),
                  pl.BlockSpec((nb, POOL_HIST, POOL_W), lambda i: (i, 0, 0)),
                  _const_spec((N_POOL_GROUPS, POOL_CH, POOL_CH))],
        out_specs=[pl.BlockSpec((nb * t_new, POOL_W), lambda i: (i, 0)),
                   pl.BlockSpec((nb, POOL_HIST, POOL_W), lambda i: (i, 0, 0))],
        out_shape=[jax.ShapeDtypeStruct((n_seq * t_new, POOL_W), F32),
                   jax.ShapeDtypeStruct((n_seq, POOL_HIST, POOL_W), F32)],
        scratch_shapes=[pltpu.VMEM((nb, HALO + t_new, POOL_W), F32)],
        compiler_params=_params("parallel"),
        name="pool_sample",
    )(pool_in, state, w_pool_b)


def _mix_kernel(x_ref, a_ref, m_ref, ga_ref, gp_ref, wo_ref, g2_ref, wq_ref, x1_ref, qm_ref):
    mix = jnp.concatenate([_rms(a_ref[...], ga_ref[...]), _rms(m_ref[...], gp_ref[...])], axis=1)
    x1 = x_ref[...] + _dot(mix.astype(BF16), wo_ref[...])
    x1_ref[...] = x1
    qm_ref[...] = _dot(_rms(x1, g2_ref[...]).astype(BF16), wq_ref[...]).astype(BF16)


def _mix(x, a, m, ga, gp, w_out_b, g2, w_mq_b, tm):
    n = x.shape[0]
    row = lambda w: pl.BlockSpec((tm, w), lambda i: (i, 0))
    return pl.pallas_call(
        _mix_kernel,
        grid=(n // tm,),
        in_specs=[row(D_MODEL), row(ATTN_W), row(POOL_W), _const_spec((1, ATTN_W)), _const_spec((1, POOL_W)),
                  _const_spec((D_MODEL, D_MODEL)), _const_spec((1, D_MODEL)), _const_spec((D_MODEL, MEM_W))],
        out_specs=[row(D_MODEL), row(MEM_W)],
        out_shape=[jax.ShapeDtypeStruct((n, D_MODEL), F32), jax.ShapeDtypeStruct((n, MEM_W), BF16)],
        compiler_params=_params("parallel"),
        name="mix_out",
    )(x, a, m, ga, gp, w_out_b, g2, w_mq_b)


def _cross_kernel(q_ref, kv_ref, o_ref, *, n_mem_blk, rows, split_heads):
    for i in range(n_mem_blk):
        outs = []
        for h in range(MEM_HEADS):
            q = q_ref[i * rows:(i + 1) * rows, h * MEM_HEAD_DIM:(h + 1) * MEM_HEAD_DIM]
            if split_heads:
                k = kv_ref[i, pl.ds(h, N_MEM, stride=2 * MEM_HEADS), :].astype(BF16)
                v = kv_ref[i, pl.ds(MEM_HEADS + h, N_MEM, stride=2 * MEM_HEADS), :].astype(BF16)
            else:
                k = kv_ref[i, :, h * MEM_HEAD_DIM:(h + 1) * MEM_HEAD_DIM].astype(BF16)
                v = kv_ref[i, :, MEM_W + h * MEM_HEAD_DIM:MEM_W + (h + 1) * MEM_HEAD_DIM].astype(BF16)
            s = _dot_nt(q, k) * (MEM_HEAD_DIM ** -0.5)
            e = jnp.exp(s - jnp.max(s, axis=-1, keepdims=True))
            o = _dot(e.astype(BF16), v) / jnp.sum(e, axis=-1, keepdims=True)
            outs.append(o)
        o_ref[i * rows:(i + 1) * rows, :] = jnp.concatenate(outs, axis=1).astype(BF16)


def _cross(qm, mem_kv, rows, n_mem_blk, tiles_per_mem):
    n = qm.shape[0]
    tm = rows * n_mem_blk
    tail = mem_kv.shape[1:]
    return pl.pallas_call(
        functools.partial(_cross_kernel, n_mem_blk=n_mem_blk, rows=rows, split_heads=tail[-1] == MEM_HEAD_DIM),
        grid=(n // tm,),
        in_specs=[pl.BlockSpec((tm, MEM_W), lambda i: (i, 0)),
                  pl.BlockSpec((n_mem_blk,) + tail, lambda i: (i // tiles_per_mem,) + (0,) * len(tail))],
        out_specs=pl.BlockSpec((tm, MEM_W), lambda i: (i, 0)),
        out_shape=jax.ShapeDtypeStruct((n, MEM_W), BF16),
        compiler_params=_params("parallel"),
        name="cross_attn",
    )(qm, mem_kv)


def _ffn_kernel(x1_ref, o_ref, wmo_ref, g3_ref, wup_ref, wdn_ref, gf_ref, y_ref):
    x2 = x1_ref[...] + _dot(o_ref[...], wmo_ref[...])
    xn = _rms(x2, g3_ref[...]).astype(BF16)
    acc = x2
    for c in range(D_FF // FF_CHUNK):
        h = jnp.maximum(_dot(xn, wup_ref[:, c * FF_CHUNK:(c + 1) * FF_CHUNK]), 0.0)
        acc = acc + _dot((h * h).astype(BF16), wdn_ref[c * FF_CHUNK:(c + 1) * FF_CHUNK, :])
    y_ref[...] = _rms(acc, gf_ref[...])


def _ffn(x1, o, w_mo_b, g3, w_up_b, w_down_b, g_final, tm):
    n = x1.shape[0]
    row = lambda w: pl.BlockSpec((tm, w), lambda i: (i, 0))
    return pl.pallas_call(
        _ffn_kernel,
        grid=(n // tm,),
        in_specs=[row(D_MODEL), row(MEM_W), _const_spec((MEM_W, D_MODEL)), _const_spec((1, D_MODEL)),
                  _const_spec((D_MODEL, D_FF)), _const_spec((D_FF, D_MODEL)), _const_spec((1, D_MODEL))],
        out_specs=row(D_MODEL),
        out_shape=jax.ShapeDtypeStruct((n, D_MODEL), F32),
        compiler_params=_params("parallel"),
        name="ffn",
    )(x1, o, w_mo_b, g3, w_up_b, w_down_b, g_final)


def _tile(n, pref):
    t = min(n, pref)
    assert n % t == 0
    return t


def kernel(x_prompt, x_sample, cache_cmp_kv, cache_sel_kv, cache_win_kv, state_pool, cache_mem_kv, page_table, mem_prompt, w_in, pe_cmp, w_cmp1, w_cmp2, w_pool, g_attn_out, pool_scale, w_out, g_norm1, g_norm2, g_mem, w_mq, w_mkv, w_mo, g_norm3, w_up, w_down, g_final):
    assert w_in.shape[0] == 1, "single-layer trunk"
    B, S, _ = x_prompt.shape
    DB, T, _ = x_sample.shape
    past = page_table.shape[1] * PAGE_SIZE
    kv5 = (2, N_KV_HEADS, HEAD_DIM)
    r2 = lambda v: v.reshape(1, -1)

    w_in_p = _permute_w_in(w_in[0])
    w_out_b, w_mq_b, w_mkv_b, w_mo_b = (w[0].astype(BF16) for w in (w_out, w_mq, w_mkv, w_mo))
    w_up_b, w_down_b, w_pool_b = w_up[0].astype(BF16), w_down[0].astype(BF16), w_pool[0].astype(BF16)
    g1, g2, g3, gm, ga, gp, gf = (r2(v) for v in (g_norm1[0], g_norm2[0], g_norm3[0], g_mem[0], g_attn_out[0],
                                                  pool_scale[0], g_final))
    cw = _compress_weights(pe_cmp[0], w_cmp1[0], w_cmp2[0])

    np_tok = B * S
    tm_p = _tile(S, 512)
    xp = x_prompt.reshape(np_tok, D_MODEL)
    q, qr, cmp_t, sel_t, win_t, sel_b, win_b, gates, pool_in = _in_proj(
        xp, g1, w_in_p, jnp.arange(S), tm_p, n_seq=B)
    mem_f, mem_b = _mem_kv(mem_prompt.reshape(B * N_MEM, D_MODEL), gm, w_mkv_b, _tile(B * N_MEM, 256))
    ckv = _compress_seq(cmp_t, cw, 64)
    a = _nsa_prompt(q, qr, gates, ckv, sel_b, win_b, B)
    m = _pool_prompt(pool_in, w_pool_b, B, tm_p)
    x1, qm = _mix(xp, a, m, ga, gp, w_out_b, g2, w_mq_b, tm_p)
    o = _cross(qm, mem_b.reshape(B, N_MEM, 2 * MEM_W), tm_p, 1, S // tm_p)
    y_p = _ffn(x1, o, w_mo_b, g3, w_up_b, w_down_b, gf, tm_p)
    rows_first = lambda t: t.reshape((B,) + kv5 + (t.shape[-1],)).transpose(0, 4, 1, 2, 3)[None]
    p_state = (rows_first(cmp_t), rows_first(sel_t), rows_first(win_t[:, :, -min(WINDOW, S):]),
               pool_in.reshape(B, S, POOL_W)[None, :, -POOL_HIST:],
               mem_f.reshape(1, B, N_MEM, 2, MEM_HEADS, MEM_HEAD_DIM))

    ns_tok = DB * T
    tm_s = _tile(ns_tok, 512)
    xs = x_sample.reshape(ns_tok, D_MODEL)
    q, qr, cmp_kv, sel_kv, win_kv, _, _, gates, pool_in = _in_proj(
        xs, g1, w_in_p, jnp.tile(past + jnp.arange(T), DB), tm_s)
    r3 = lambda v: v.reshape(DB, T, v.shape[-1])
    ckv = _compress_paged(_pages_t(cache_cmp_kv[0]), page_table, cw, 64)
    a, s_win = _nsa_sample(r3(q), r3(qr), r3(gates), ckv, _pages_t(cache_sel_kv[0]), page_table,
                           r3(sel_kv), _pages_t(cache_win_kv[0]), r3(win_kv), 64, 2)
    m, s_pool = _pool_sample(r3(pool_in), state_pool[0], w_pool_b, past, _tile(DB, 16))
    x1, qm = _mix(xs, a.reshape(ns_tok, ATTN_W), m, ga, gp, w_out_b, g2, w_mq_b, tm_s)
    o = _cross(qm, cache_mem_kv[0].reshape(DB, N_MEM * 2 * MEM_HEADS, MEM_HEAD_DIM), T, _tile(DB, 4), 1)
    y_s = _ffn(x1, o, w_mo_b, g3, w_up_b, w_down_b, gf, tm_s)
    s_state = (cmp_kv.reshape((1, DB, T) + kv5), sel_kv.reshape((1, DB, T) + kv5),
               s_win.reshape((1, DB, WINDOW) + kv5), s_pool[None])

    return (y_p.reshape(B, S, D_MODEL), y_s.reshape(DB, T, D_MODEL)) + p_state + s_state
```

```python
import functools

import jax
import jax.numpy as jnp
from jax import lax
from jax.experimental import pallas as pl
from jax.experimental.pallas import tpu as pltpu

D_MODEL = 1024
PAGE_SIZE = 128
ATTN_W = 512
POOL_W = 512
HEAD_DIM = 64
N_HEADS = 8
N_KV_HEADS = 2
HEADS_PER_KV = N_HEADS // N_KV_HEADS
SCALE = HEAD_DIM ** -0.5
LOG2E = 1.4426950408889634
Q_SCALE = SCALE * LOG2E
ROT_DIM = HEAD_DIM // 4
ROPE_THETA = 500000.0
CMP_LEN = 32
CMP_STRIDE = 16
CMP_HID = HEAD_DIM
SEL_BLOCK = 64
SEL_TOPN = 16
WINDOW = 512
FORCE_BONUS = 1000.0
POOL_WINDOWS = (2, 4, 8, 16)
N_POOL_GROUPS = len(POOL_WINDOWS)
POOL_CH = POOL_W // N_POOL_GROUPS
POOL_HIST = max(POOL_WINDOWS) - 1
N_MEM = 256
MEM_HEADS = 4
MEM_HEAD_DIM = 128
MEM_W = MEM_HEADS * MEM_HEAD_DIM
D_FF = 4 * D_MODEL
EPS = 1e-6
KV_W = 2 * N_KV_HEADS * HEAD_DIM
N_GATES = 3 * N_HEADS
LANES = 128
VMEM_LIMIT = 56 * 1024 * 1024
QW = N_HEADS * LANES
TQ = 128
NEG = -1e30
CHUNK_W = CMP_STRIDE * KV_W
CMP_R = CMP_LEN // CMP_STRIDE
CHUNKS_PER_PAGE = PAGE_SIZE // CMP_STRIDE
HALO = 16
FF_CHUNK = 1024

F32 = jnp.float32
BF16 = jnp.bfloat16


def _rms(x, g):
    return x * lax.rsqrt(jnp.mean(x * x, axis=-1, keepdims=True) + EPS) * g


def _dot(a, b):
    return jnp.dot(a, b, preferred_element_type=F32)


def _dot_nt(a, b):
    return lax.dot_general(a, b, (((1,), (1,)), ((), ())), preferred_element_type=F32)


ONES_ROWS = 16
VT_ROWS = LANES + ONES_ROWS


def _ones_rows(n):
    return jnp.where(lax.broadcasted_iota(jnp.int32, (ONES_ROWS, n), 0) == 0, 1.0, 0.0).astype(BF16)


def _const_spec(shape):
    n = len(shape)
    return pl.BlockSpec(shape, lambda *_: (0,) * n, pipeline_mode=pl.Buffered(1))


def _params(*sem):
    return pltpu.CompilerParams(dimension_semantics=sem, vmem_limit_bytes=VMEM_LIMIT)


X_Q, X_CMP, X_SEL, X_WIN, X_POOL, X_GATE = 0, 1024, 1280, 1536, 1792, 2304
X_IN = X_GATE + LANES


def _permute_w_in(w_in):
    q, kc, ks, kw, gl, pool = jnp.split(w_in, (512, 768, 1024, 1280, 1280 + N_GATES), axis=1)
    z = jnp.zeros((D_MODEL, HEAD_DIM), w_in.dtype)
    qx = []
    for h in range(N_HEADS):
        qh = q[:, h * HEAD_DIM:(h + 1) * HEAD_DIM]
        qx += [qh, z] if h // HEADS_PER_KV == 0 else [z, qh]
    pad = jnp.zeros((D_MODEL, LANES - N_GATES), w_in.dtype)
    return jnp.concatenate(qx + [kc, ks, kw, pool, gl, pad], axis=1).astype(BF16)


def _rope_tables(pos):
    half = ROT_DIM // 2
    inv = ROPE_THETA ** (-jnp.arange(half, dtype=F32) / half)
    ang = pos.astype(F32)[:, None] * inv[None, :]
    cos, sin = jnp.cos(ang), jnp.sin(ang)
    z = jnp.zeros_like(cos)
    rest0 = jnp.zeros((pos.shape[0], HEAD_DIM - ROT_DIM), F32)
    cs = jnp.concatenate([cos, cos, rest0 + 1.0], axis=1)
    sa = jnp.concatenate([-sin, z, rest0], axis=1)
    sb = jnp.concatenate([z, sin, rest0], axis=1)
    return tuple(jnp.tile(t, (1, LANES // HEAD_DIM)) for t in (cs, sa, sb))


def _rope(v, cs, sa, sb):
    n = v.shape[1] // LANES
    w = v.shape[1]
    cs, sa, sb = (jnp.tile(t, (1, n)) for t in (cs, sa, sb))
    return v * cs + pltpu.roll(v, w - ROT_DIM // 2, 1) * sa + pltpu.roll(v, ROT_DIM // 2, 1) * sb


def _in_proj_kernel(x_ref, g_ref, w_ref, cs_ref, sa_ref, sb_ref,
                    q_ref, qr_ref, cmp_ref, sel_ref, win_ref, selb_ref, winb_ref, gate_ref, pool_ref, *, feature_major):
    xb = _rms(x_ref[...], g_ref[...]).astype(BF16)
    cs, sa, sb = cs_ref[...], sa_ref[...], sb_ref[...]

    def put_state(ref, v):
        if feature_major:
            ref[0] = jnp.concatenate([v[:, c * LANES:(c + 1) * LANES].T for c in range(KV_W // LANES)], axis=0)
        else:
            ref[...] = v

    q = _dot(xb, w_ref[:, X_Q:X_CMP])
    q_ref[...] = (q * Q_SCALE).astype(BF16)
    qr_ref[...] = (_rope(q, cs, sa, sb) * Q_SCALE).astype(BF16)
    put_state(cmp_ref, _dot(xb, w_ref[:, X_CMP:X_SEL]))
    for lo, f_ref, b_ref in ((X_SEL, sel_ref, selb_ref), (X_WIN, win_ref, winb_ref)):
        kv = _dot(xb, w_ref[:, lo:lo + KV_W])
        kv = jnp.concatenate([_rope(kv[:, :LANES], cs, sa, sb), kv[:, LANES:]], axis=1)
        put_state(f_ref, kv)
        b_ref[...] = kv.astype(BF16)
    pool_ref[...] = _dot(xb, w_ref[:, X_POOL:X_GATE])
    gate_ref[...] = jax.nn.sigmoid(_dot(xb, w_ref[:, X_GATE:X_IN]))


def _in_proj(x, g1, w_in_p, pos_rows, tm, n_seq=None):
    n = x.shape[0]
    cs, sa, sb = _rope_tables(pos_rows)
    row = lambda w: pl.BlockSpec((tm, w), lambda i: (i, 0))
    tab = pl.BlockSpec((tm, LANES), lambda i: (i % (pos_rows.shape[0] // tm), 0))
    sds = lambda w, dt: jax.ShapeDtypeStruct((n, w), dt)
    if n_seq is None:
        st_spec, st_shape = row(KV_W), sds(KV_W, F32)
    else:
        nt = n // n_seq // tm
        st_spec = pl.BlockSpec((1, KV_W, tm), lambda i: (i // nt, 0, i % nt))
        st_shape = jax.ShapeDtypeStruct((n_seq, KV_W, n // n_seq), F32)
    return pl.pallas_call(
        functools.partial(_in_proj_kernel, feature_major=n_seq is not None),
        grid=(n // tm,),
        in_specs=[row(D_MODEL), _const_spec((1, D_MODEL)), _const_spec((D_MODEL, X_IN)),
                  tab, tab, tab],
        out_specs=[row(QW), row(QW), st_spec, st_spec, st_spec, row(KV_W), row(KV_W), row(LANES), row(POOL_W)],
        out_shape=[sds(QW, BF16), sds(QW, BF16), st_shape, st_shape, st_shape,
                   sds(KV_W, BF16), sds(KV_W, BF16), sds(LANES, F32), sds(POOL_W, F32)],
        compiler_params=_params("parallel"),
        name="in_proj",
    )(x, g1, w_in_p, cs, sa, sb)


def _mem_kv_kernel(m_ref, g_ref, w_ref, o_ref, kb_ref, vt_ref):
    kv = _dot(_rms(m_ref[...], g_ref[...]).astype(BF16), w_ref[...])
    o_ref[...] = kv
    kb_ref[...] = kv[:, :MEM_W].astype(BF16)
    for h in range(MEM_HEADS):
        v = kv[:, MEM_W + h * MEM_HEAD_DIM:MEM_W + (h + 1) * MEM_HEAD_DIM]
        vt_ref[0, h] = jnp.concatenate([v.T.astype(BF16), _ones_rows(N_MEM)], axis=0)


def _mem_kv(mem, g_mem, w_mkv_b):
    n = mem.shape[0]
    row = lambda w: pl.BlockSpec((N_MEM, w), lambda i: (i, 0))
    return pl.pallas_call(
        _mem_kv_kernel,
        grid=(n // N_MEM,),
        in_specs=[row(D_MODEL), _const_spec((1, D_MODEL)), _const_spec((D_MODEL, 2 * MEM_W))],
        out_specs=[row(2 * MEM_W), row(MEM_W),
                   pl.BlockSpec((1, MEM_HEADS, VT_ROWS, N_MEM), lambda i: (i, 0, 0, 0))],
        out_shape=[jax.ShapeDtypeStruct((n, 2 * MEM_W), F32), jax.ShapeDtypeStruct((n, MEM_W), BF16),
                   jax.ShapeDtypeStruct((n // N_MEM, MEM_HEADS, VT_ROWS, N_MEM), BF16)],
        compiler_params=_params("parallel"),
        name="mem_kv",
    )(mem, g_mem, w_mkv_b)


def _compress_weights(pe, w1, w2):
    w1r = w1.reshape(2, CMP_R, CMP_STRIDE, HEAD_DIM, CMP_HID)
    eye_c = jnp.eye(2, dtype=w1.dtype)
    eye_g = jnp.eye(N_KV_HEADS, dtype=w1.dtype)
    w1big = jnp.einsum('crsdh,cx,gy->scgdrxyh', w1r, eye_c, eye_g).reshape(CHUNK_W, CMP_R * KV_W)
    pe_term = jnp.einsum('cld,cldh->ch', pe, w1.reshape(2, CMP_LEN, HEAD_DIM, CMP_HID))
    bias = jnp.broadcast_to(pe_term[:, None, :], (2, N_KV_HEADS, CMP_HID)).reshape(1, KV_W)
    w2big = jnp.einsum('chd,cx,gy->cghxyd', w2, eye_c, eye_g).reshape(KV_W, KV_W)
    return w1big.astype(BF16), bias, w2big.astype(BF16)


def _compress_kernel(*refs, n_in, ppr, has_pt):
    if has_pt:
        refs = refs[1:]
    x_refs = refs[:n_in]
    w1_ref, b_ref, w2_ref, o_ref, carry_ref = refs[n_in:]

    @pl.when(pl.program_id(1) == 0)
    def _():
        carry_ref[...] = jnp.zeros_like(carry_ref)

    r_i = lax.broadcasted_iota(jnp.int32, (PAGE_SIZE, PAGE_SIZE), 0)
    t_i = lax.broadcasted_iota(jnp.int32, (PAGE_SIZE, PAGE_SIZE), 1)
    src = CMP_STRIDE * (r_i & (CHUNKS_PER_PAGE - 1)) + lax.shift_right_logical(r_i, 3)
    pick = jnp.where(t_i == src, 1.0, 0.0).astype(BF16)
    pages = [_dot_nt(pick, r[0, :, j * PAGE_SIZE:(j + 1) * PAGE_SIZE].astype(BF16))
             for r in x_refs for j in range(ppr)]
    cols = [jnp.concatenate([pg[s * CHUNKS_PER_PAGE:(s + 1) * CHUNKS_PER_PAGE, :] for pg in pages], axis=0)
            for s in range(CMP_STRIDE)]
    x = jnp.concatenate(cols, axis=1)
    u = _dot(x.astype(BF16), w1_ref[...])
    u0, u1 = u[:, :KV_W], u[:, KV_W:]
    rows = u0.shape[0]
    row = lax.broadcasted_iota(jnp.int32, u0.shape, 0)
    prev = jnp.where(row == 0, carry_ref[0:1, :], pltpu.roll(u0, 1, 0))
    carry_ref[...] = jnp.broadcast_to(u0[rows - 1:rows, :], carry_ref.shape)
    h = prev + u1 + b_ref[...]
    o_ref[0] = _dot(jax.nn.gelu(h).astype(BF16), w2_ref[...]).astype(BF16)


def _compress_call(n_seq, n_pages, n_in, ppr, has_pt, x_spec_fn):
    P = n_in * ppr
    assert n_pages % P == 0
    grid = (n_seq, n_pages // P)
    in_specs = [x_spec_fn(k) for k in range(n_in)] + [
        _const_spec((CHUNK_W, CMP_R * KV_W)), _const_spec((1, KV_W)), _const_spec((KV_W, KV_W))]
    out_spec = pl.BlockSpec((1, P * CHUNKS_PER_PAGE, KV_W), (lambda b, i, *_: (b, i, 0)))
    kern = functools.partial(_compress_kernel, n_in=n_in, ppr=ppr, has_pt=has_pt)
    common = dict(out_shape=jax.ShapeDtypeStruct((n_seq, n_pages * CHUNKS_PER_PAGE, KV_W), BF16),
                  compiler_params=_params("parallel", "arbitrary"), name="compress")
    scratch = [pltpu.VMEM((8, KV_W), F32)]
    if has_pt:
        return pl.pallas_call(kern, grid_spec=pltpu.PrefetchScalarGridSpec(
            num_scalar_prefetch=1, grid=grid, in_specs=in_specs, out_specs=out_spec, scratch_shapes=scratch), **common)
    return pl.pallas_call(kern, grid=grid, in_specs=in_specs, out_specs=out_spec, scratch_shapes=scratch, **common)


def _compress_seq(cmp_t, cw, pages_per_step):
    n_seq, _, seq = cmp_t.shape
    n_pages = seq // PAGE_SIZE
    P = min(pages_per_step, n_pages)
    spec = lambda k: pl.BlockSpec((1, KV_W, P * PAGE_SIZE), lambda b, i: (b, 0, i))
    return _compress_call(n_seq, n_pages, 1, P, False, spec)(cmp_t, *cw)


def _pages_t(cache):
    return cache.transpose(0, 2, 3, 4, 1).reshape(cache.shape[0], KV_W, cache.shape[1])


def _compress_paged(cache_t, page_table, cw, pages_per_step):
    n_seq, n_pages = page_table.shape
    P = min(pages_per_step, n_pages)
    spec = lambda k: pl.BlockSpec((1, KV_W, PAGE_SIZE), lambda b, i, pt: (pt[b, i * P + k], 0, 0))
    return _compress_call(n_seq, n_pages, P, 1, True, spec)(page_table, *([cache_t] * P), *cw)


def _overlap_t(nch, n_sel_pad):
    m = jnp.arange(nch)[None, :]
    j = jnp.arange(n_sel_pad)[:, None]
    lo = (m - 1) * CMP_STRIDE
    ov = jnp.maximum(jnp.minimum(lo + CMP_LEN, (j + 1) * SEL_BLOCK) - jnp.maximum(lo, j * SEL_BLOCK), 0)
    return jnp.where(m >= 1, ov.astype(F32) / CMP_LEN, 0.0).astype(BF16)


def _heads_to_rows(x):
    return jnp.concatenate([x[:, h * LANES:(h + 1) * LANES] for h in range(N_HEADS)], axis=0)


def _topk_bias(imp, pos, n_blk, k_top, axis):
    j = lax.broadcasted_iota(jnp.int32, imp.shape, axis)
    cur = lax.shift_right_logical(pos, 6)
    forced = (j == 0) | (j == cur) | (j == cur - 1)
    score = jnp.where(j * SEL_BLOCK <= pos, imp + jnp.where(forced, FORCE_BONUS, 0.0), -1e9)
    score = jnp.where(j < n_blk, score, -jnp.inf)
    bias = jnp.full(imp.shape, NEG, F32)
    for _ in range(k_top):
        mx = jnp.max(score, axis=axis, keepdims=True)
        idx = jnp.min(jnp.where(score == mx, j, LANES), axis=axis, keepdims=True)
        hit = j == idx
        bias = jnp.where(hit, 0.0, bias)
        score = jnp.where(hit, -jnp.inf, score)
    return bias


def _split_dot(dot, ps):
    hi = ps.astype(BF16)
    lo = (ps - hi.astype(F32)).astype(BF16)
    return dot(hi) + dot(lo)


def _nsa_prompt_kernel(q_ref, qr_ref, g_ref, ckv_ref, ksel_ref, kwin_ref, ovt_ref, a_ref,
                       kx_ref, vts_ref, vtw_ref, lc_ref, *, tk, n_sel, k_top):
    i = pl.program_id(1)
    q0 = pl.multiple_of(i * TQ, TQ)
    nc = ckv_ref.shape[1]

    @pl.when(i == 0)
    def _():
        kx_ref[...] = jnp.zeros_like(kx_ref)
        vts_ref[...] = jnp.zeros_like(vts_ref)
        vtw_ref[...] = jnp.zeros_like(vtw_ref)
        lc_ref[0:LANES, :] = ckv_ref[0, :, LANES:2 * LANES].astype(F32).T.astype(BF16)
        lc_ref[LANES:2 * LANES, :] = ovt_ref[...]
        lc_ref[2 * LANES:, :] = _ones_rows(nc)

    key_t = q0 + lax.broadcasted_iota(jnp.int32, (TQ, 1), 0)
    onehot = jnp.where(lax.shift_right_logical(key_t, 6) == lax.broadcasted_iota(jnp.int32, (TQ, LANES), 1), 1.0, 0.0)
    kx_ref[pl.ds(q0, TQ), :] = jnp.concatenate([ksel_ref[pl.ds(q0, TQ), 0:LANES], onehot.astype(BF16)], axis=1)
    for k_ref, vt_ref in ((ksel_ref, vts_ref), (kwin_ref, vtw_ref)):
        v = k_ref[pl.ds(q0, TQ), LANES:2 * LANES].astype(F32)
        vt_ref[:, pl.ds(q0, TQ)] = jnp.concatenate([v.T.astype(BF16), _ones_rows(TQ)], axis=0)

    nl = N_HEADS * TQ
    lane = lax.broadcasted_iota(jnp.int32, (1, nl), 1)
    pos = q0 + (lane & (TQ - 1))
    qc = _heads_to_rows(q_ref)
    qr = _heads_to_rows(qr_ref)

    m_idx = lax.broadcasted_iota(jnp.int32, (nc, 1), 0)
    c_end = jnp.where(m_idx >= 1, m_idx * CMP_STRIDE + (CMP_STRIDE - 1), 2 ** 30)
    s = jnp.where(c_end <= pos, _dot_nt(ckv_ref[0, :, 0:LANES], qc), NEG)
    acc = _dot(lc_ref[...], jnp.exp2(s - jnp.max(s, axis=0, keepdims=True)).astype(BF16))
    rden = jnp.where(pos >= CMP_LEN - 1, 1.0 / jnp.maximum(acc[2 * LANES:2 * LANES + 1, :], 1e-30), 0.0)
    oc_t = acc[0:LANES, :] * rden
    imp_h = acc[LANES:2 * LANES, :] * rden

    imp_t = jnp.concatenate(
        [sum(imp_h[:, (g * HEADS_PER_KV + hh) * TQ:(g * HEADS_PER_KV + hh + 1) * TQ] for hh in range(HEADS_PER_KV))
         for g in range(N_KV_HEADS)], axis=1)
    bias_t = _topk_bias(imp_t, pos[:, :N_KV_HEADS * TQ], n_sel, k_top, 0)
    biases = [bias_t[:, g * TQ:(g + 1) * TQ].T.astype(BF16) for g in range(N_KV_HEADS)]
    bias_rows = jnp.concatenate([biases[h // HEADS_PER_KV] for h in range(N_HEADS)], axis=0)
    qs = jnp.concatenate([qr, bias_rows], axis=1)

    def sel_chunk(c0, masked, carry):
        m, acc = carry
        s = _dot_nt(kx_ref[pl.ds(c0, tk), :], qs)
        if masked:
            s = jnp.where(c0 + lax.broadcasted_iota(jnp.int32, (tk, 1), 0) <= pos, s, NEG)
        m_new = jnp.maximum(m, jnp.max(s, axis=0, keepdims=True))
        p = jnp.exp2(s - m_new).astype(BF16)
        acc = jnp.exp2(m - m_new) * acc + _dot(vts_ref[:, pl.ds(c0, tk)], p)
        return m_new, acc

    n_full = q0 // tk
    carry = (jnp.full((1, nl), NEG, F32), jnp.zeros((VT_ROWS, nl), F32))
    one = lambda c, cr: sel_chunk(pl.multiple_of(c * tk, tk), False, cr)
    carry = lax.fori_loop(0, n_full // 2, lambda c, cr: one(2 * c + 1, one(2 * c, cr)), carry)
    carry = lax.fori_loop(n_full // 2 * 2, n_full, one, carry)
    _, acc = sel_chunk(pl.multiple_of(n_full * tk, tk), True, carry)
    os_t = acc[0:LANES, :] / jnp.maximum(acc[LANES:LANES + 1, :], 1e-30)

    wk = WINDOW + TQ
    w0 = pl.multiple_of(jnp.maximum(q0 - WINDOW, 0), TQ)
    s = _dot_nt(kwin_ref[pl.ds(w0, wk), 0:LANES], qr)
    kp = w0 + lax.broadcasted_iota(jnp.int32, (wk, 1), 0)
    s = jnp.concatenate([jnp.where((kp[:TQ] <= pos) & (kp[:TQ] >= pos - WINDOW), s[:TQ], NEG),
                         jnp.where(kp[TQ:] <= pos, s[TQ:], NEG)], axis=0)
    acc = _dot(vtw_ref[:, pl.ds(w0, wk)], jnp.exp2(s - jnp.max(s, axis=0, keepdims=True)).astype(BF16))
    ow_t = acc[0:LANES, :] / jnp.maximum(acc[LANES:LANES + 1, :], 1e-30)

    g_t = g_ref[...].T
    outs = []
    for h in range(N_HEADS):
        g = h // HEADS_PER_KV
        blk_h = lambda o: o[g * HEAD_DIM:(g + 1) * HEAD_DIM, h * TQ:(h + 1) * TQ]
        gate = lambda j: g_t[3 * h + j:3 * h + j + 1, :]
        outs.append(gate(0) * blk_h(oc_t) + gate(1) * blk_h(os_t) + gate(2) * blk_h(ow_t))
    a_ref[...] = jnp.concatenate(outs, axis=0).T


def _nsa_prompt(q, qr, gates, ckv, ksel, kwin, n_seq):
    n = q.shape[0]
    seq = n // n_seq
    nch = ckv.shape[1]
    n_sel = seq // SEL_BLOCK
    assert seq % TQ == 0 and seq >= WINDOW + TQ and n_sel <= LANES and seq % SEL_BLOCK == 0
    tk = min(512, seq)
    assert seq % tk == 0
    nt = seq // TQ
    tile = lambda w: pl.BlockSpec((TQ, w), lambda b, i: (b * nt + i, 0))
    whole = lambda: pl.BlockSpec((seq, KV_W), lambda b, i: (b, 0), pipeline_mode=pl.Buffered(1))
    kern = functools.partial(_nsa_prompt_kernel, tk=tk, n_sel=n_sel, k_top=min(SEL_TOPN, n_sel))
    return pl.pallas_call(
        kern,
        grid=(n_seq, nt),
        in_specs=[tile(QW), tile(QW), tile(LANES),
                  pl.BlockSpec((1, nch, KV_W), lambda b, i: (b, 0, 0)),
                  whole(), whole(), _const_spec((LANES, nch))],
        out_specs=tile(ATTN_W),
        out_shape=jax.ShapeDtypeStruct((n, ATTN_W), F32),
        scratch_shapes=[pltpu.VMEM((seq, 2 * LANES), BF16), pltpu.VMEM((VT_ROWS, seq), BF16),
                        pltpu.VMEM((VT_ROWS, seq), BF16), pltpu.VMEM((2 * LANES + ONES_ROWS, nch), BF16)],
        compiler_params=_params("arbitrary", "arbitrary"),
        name="nsa_prompt",
    )(q, qr, gates, ckv, ksel, kwin, _overlap_t(nch, LANES))


def _softmax_rows2(s_a, mask_a, s_b, mask_b):
    s_a = jnp.where(mask_a, s_a, NEG)
    s_b = jnp.where(mask_b, s_b, NEG)
    mx = jnp.maximum(jnp.max(s_a, axis=1, keepdims=True), jnp.max(s_b, axis=1, keepdims=True))
    e_a = jnp.where(mask_a, jnp.exp2(s_a - mx), 0.0)
    e_b = jnp.where(mask_b, jnp.exp2(s_b - mx), 0.0)
    den = jnp.sum(e_a, axis=1, keepdims=True) + jnp.sum(e_b, axis=1, keepdims=True)
    return e_a, e_b, jnp.maximum(den, 1e-30)


def _pad_rows(x, rows):
    return jnp.concatenate([x, jnp.zeros((rows - x.shape[0], x.shape[1]), x.dtype)], axis=0)


def _nsa_sample_kernel(*refs, n_seq, n_pg, t_new, past, n_cb, k_cache):
    pg_refs = refs[1:1 + n_seq * n_pg]
    (q_ref, qr_ref, g_ref, ckv_ref, seln_ref, cwin_ref, winn_ref, ovl_ref,
     a_ref, swin_ref, qs_ref, oc_ref, ow_ref, m_ref, l_ref, acc_ref) = refs[1 + n_seq * n_pg:]
    j = pl.program_id(1)
    nr = N_HEADS * t_new
    row_t = lax.rem(lax.broadcasted_iota(jnp.int32, (nr, 1), 0), t_new)
    pos = past + row_t
    lane = lax.broadcasted_iota(jnp.int32, (1, LANES), 1)

    def first(b):
        qc = _heads_to_rows(q_ref[b].astype(F32)).astype(BF16)
        qr = _heads_to_rows(qr_ref[b].astype(F32)).astype(BF16)
        nc = ckv_ref.shape[1]
        m_idx = lax.broadcasted_iota(jnp.int32, (1, nc), 1)
        cmask = (m_idx >= 1) & (m_idx * CMP_STRIDE + (CMP_STRIDE - 1) <= pos)
        s = jnp.where(cmask, _dot_nt(qc, ckv_ref[b, :, 0:LANES]), NEG)
        e = jnp.where(cmask, jnp.exp2(s - jnp.max(s, axis=1, keepdims=True)), 0.0)
        p_c = e / jnp.maximum(jnp.sum(e, axis=1, keepdims=True), 1e-30)
        oc_ref[b] = _dot(p_c.astype(BF16), ckv_ref[b, :, LANES:2 * LANES])
        imps = []
        for g in range(N_KV_HEADS):
            ps = sum(p_c[(g * HEADS_PER_KV + hh) * t_new:(g * HEADS_PER_KV + hh + 1) * t_new, :]
                     for hh in range(HEADS_PER_KV))
            imps.append(_split_dot(lambda v: _dot(v, ovl_ref[...]), ps))
        imp_t = _pad_rows(jnp.concatenate(imps, axis=0), LANES).T
        bias = _topk_bias(imp_t, past + lax.rem(lane, t_new), n_cb, k_cache, 0).T
        biases = [bias[g * t_new:(g + 1) * t_new, :] for g in range(N_KV_HEADS)]
        bias_rows = jnp.concatenate([biases[h // HEADS_PER_KV] for h in range(N_HEADS)], axis=0)
        qs_ref[b] = jnp.concatenate([qr, bias_rows.astype(BF16)], axis=1)
        cw = jnp.concatenate([cwin_ref[b, c * LANES:(c + 1) * LANES, :].T for c in range(KV_W // LANES)], axis=1)
        wn = _pad_rows(winn_ref[b], LANES)
        r_idx = lax.broadcasted_iota(jnp.int32, (1, WINDOW), 1)
        e_a, e_b, den = _softmax_rows2(
            _dot_nt(qr, cw[:, 0:LANES].astype(BF16)), r_idx >= row_t,
            _dot_nt(qr, wn[:, 0:LANES].astype(BF16)), lane <= row_t)
        ow_ref[b] = (_dot(e_a.astype(BF16), cw[:, LANES:].astype(BF16))
                     + _dot(e_b.astype(BF16), wn[:, LANES:].astype(BF16))) / den
        shifted = pltpu.roll(cwin_ref[b], WINDOW - t_new, 1)
        fresh = pltpu.roll(wn.T, LANES - t_new, 1)
        tail = jnp.where(lane >= LANES - t_new, fresh, shifted[:, WINDOW - LANES:])
        swin_ref[b] = jnp.concatenate([shifted[:, :WINDOW - LANES], tail], axis=1)
        m_ref[b] = jnp.full(m_ref.shape[1:], NEG, F32)
        l_ref[b] = jnp.zeros(l_ref.shape[1:], F32)
        acc_ref[b] = jnp.zeros(acc_ref.shape[1:], F32)

    def online(b, s, pv):
        m = m_ref[b, :, 0:1]
        m_new = jnp.maximum(m, jnp.max(s, axis=1, keepdims=True))
        alpha = jnp.exp2(m - m_new)
        p = jnp.exp2(s - m_new)
        l_ref[b] = jnp.broadcast_to(alpha * l_ref[b, :, 0:1] + jnp.sum(p, axis=1, keepdims=True), l_ref.shape[1:])
        acc_ref[b] = alpha * acc_ref[b] + pv(p.astype(BF16))
        m_ref[b] = jnp.broadcast_to(m_new, m_ref.shape[1:])

    def pages(b):
        mine = pg_refs[b * n_pg:(b + 1) * n_pg]
        cat = lambda lo: jnp.concatenate([r[0, lo:lo + LANES, :] for r in mine], axis=1).astype(BF16)
        k_t, v_t = cat(0), cat(LANES)
        nk = k_t.shape[1]
        blk = lax.shift_right_logical(j * nk + lax.broadcasted_iota(jnp.int32, (1, nk), 1), 6)
        onehot_t = jnp.where(blk == lax.broadcasted_iota(jnp.int32, (LANES, 1), 0), 1.0, 0.0).astype(BF16)
        online(b, _dot(qs_ref[b], jnp.concatenate([k_t, onehot_t], axis=0)), lambda p: _dot_nt(p, v_t))

    def last(b):
        sn = _pad_rows(seln_ref[b], LANES)
        s = _dot_nt(qs_ref[b, :, 0:LANES], sn[:, 0:LANES].astype(BF16))
        online(b, jnp.where(lane <= row_t, s, NEG), lambda p: _dot(p, sn[:, LANES:].astype(BF16)))
        os = acc_ref[b] / jnp.maximum(l_ref[b, :, 0:1], 1e-30)
        gts = g_ref[b]
        outs = []
        for h in range(N_HEADS):
            g = h // HEADS_PER_KV
            blk_h = lambda o: o[h * t_new:(h + 1) * t_new, g * HEAD_DIM:(g + 1) * HEAD_DIM]
            gate = lambda k: gts[:, 3 * h + k:3 * h + k + 1]
            outs.append(gate(0) * blk_h(oc_ref[b]) + gate(1) * blk_h(os) + gate(2) * blk_h(ow_ref[b]))
        a_ref[b] = jnp.concatenate(outs, axis=1)

    @pl.when(j == 0)
    def _():
        for b in range(n_seq):
            first(b)

    for b in range(n_seq):
        pages(b)

    @pl.when(j == pl.num_programs(1) - 1)
    def _():
        for b in range(n_seq):
            last(b)


def _nsa_sample(q, qr, gates, ckv, cache_sel, page_table, sel_new, cache_win, win_new, pages_per_step, seqs_per_step):
    n_all, t_new = q.shape[:2]
    n_pages = page_table.shape[1]
    past = n_pages * PAGE_SIZE
    n_cb = past // SEL_BLOCK
    assert cache_win.shape[2] == WINDOW and n_cb <= LANES and t_new <= SEL_BLOCK and t_new % 8 == 0
    k_cache = min(SEL_TOPN, n_cb + 1) - 1
    assert k_cache >= 2
    P = min(pages_per_step, n_pages)
    NS = min(seqs_per_step, n_all)
    assert n_pages % P == 0 and n_all % NS == 0
    nch = ckv.shape[1]
    nr = N_HEADS * t_new
    per_seq = lambda r, w: pl.BlockSpec((NS, r, w), lambda b, j, pt: (b, 0, 0))
    page = lambda s, k: pl.BlockSpec((1, KV_W, PAGE_SIZE), lambda b, j, pt: (pt[b * NS + s, j * P + k], 0, 0))
    kern = functools.partial(_nsa_sample_kernel, n_seq=NS, n_pg=P, t_new=t_new, past=past, n_cb=n_cb, k_cache=k_cache)
    return pl.pallas_call(
        kern,
        grid_spec=pltpu.PrefetchScalarGridSpec(
            num_scalar_prefetch=1,
            grid=(n_all // NS, n_pages // P),
            in_specs=[page(s, k) for s in range(NS) for k in range(P)] + [
                per_seq(t_new, QW), per_seq(t_new, QW), per_seq(t_new, LANES), per_seq(nch, KV_W),
                per_seq(t_new, KV_W), per_seq(KV_W, WINDOW), per_seq(t_new, KV_W),
                pl.BlockSpec((nch, LANES), lambda b, j, pt: (0, 0), pipeline_mode=pl.Buffered(1))],
            out_specs=[per_seq(t_new, ATTN_W), per_seq(KV_W, WINDOW)],
            scratch_shapes=[pltpu.VMEM((NS, nr, 2 * LANES), BF16)] + [pltpu.VMEM((NS, nr, LANES), F32)] * 5),
        out_shape=[jax.ShapeDtypeStruct((n_all, t_new, ATTN_W), F32),
                   jax.ShapeDtypeStruct((n_all, KV_W, WINDOW), F32)],
        compiler_params=_params("arbitrary", "arbitrary"),
        name="nsa_sample",
    )(page_table, *([cache_sel] * (NS * P)), q, qr, gates, ckv, sel_new, cache_win, win_new,
      _overlap_t(nch, LANES).T)


def _pool_windows(ext_ref, n_rows, pos, w_ref, lead):
    outs = []
    for gi, w in enumerate(POOL_WINDOWS):
        lanes = slice(gi * POOL_CH, (gi + 1) * POOL_CH)
        tot = None
        for k in range(w):
            v = ext_ref[lead + (slice(HALO - k, HALO - k + n_rows), lanes)]
            tot = v if tot is None else tot + v
        cnt = jnp.minimum(pos + 1, w).astype(F32)
        d = tot / cnt - ext_ref[lead + (slice(HALO, HALO + n_rows), lanes)]
        outs.append((d, w_ref[gi]))
    return outs


def _pool_prompt_kernel(x_ref, h_ref, w_ref, o_ref, ext_ref, *, tm):
    i = pl.program_id(1)
    ext_ref[0:HALO, :] = jnp.where(i == 0, 0.0, h_ref[...])
    ext_ref[HALO:, :] = x_ref[...]
    pos = i * tm + lax.broadcasted_iota(jnp.int32, (tm, 1), 0)
    res = [_dot(d.astype(BF16), w) for d, w in _pool_windows(ext_ref, tm, pos, w_ref, ())]
    o_ref[...] = jnp.concatenate(res, axis=1)


def _pool_prompt(pool_in, w_pool_b, n_seq, tm):
    n = pool_in.shape[0]
    nt = n // n_seq // tm
    hb = tm // HALO
    return pl.pallas_call(
        functools.partial(_pool_prompt_kernel, tm=tm),
        grid=(n_seq, nt),
        in_specs=[pl.BlockSpec((tm, POOL_W), lambda b, i: (b * nt + i, 0)),
                  pl.BlockSpec((HALO, POOL_W), lambda b, i: (jnp.maximum((b * nt + i) * hb - 1, 0), 0)),
                  _const_spec((N_POOL_GROUPS, POOL_CH, POOL_CH))],
        out_specs=pl.BlockSpec((tm, POOL_W), lambda b, i: (b * nt + i, 0)),
        out_shape=jax.ShapeDtypeStruct((n, POOL_W), F32),
        scratch_shapes=[pltpu.VMEM((HALO + tm, POOL_W), F32)],
        compiler_params=_params("parallel", "arbitrary"),
        name="pool_prompt",
    )(pool_in, pool_in, w_pool_b)


def _pool_sample_kernel(x_ref, st_ref, w_ref, o_ref, ns_ref, ext_ref, *, t_new, past):
    nb = x_ref.shape[0]
    ext_ref[:, HALO - POOL_HIST:HALO, :] = st_ref[...]
    ext_ref[:, HALO:, :] = x_ref[...]
    pos = past + lax.broadcasted_iota(jnp.int32, (1, t_new, 1), 1)
    res = [_dot(d.reshape(nb * t_new, POOL_CH).astype(BF16), w)
           for d, w in _pool_windows(ext_ref, t_new, pos, w_ref, (slice(None),))]
    o_ref[...] = jnp.concatenate(res, axis=1)
    ns_ref[...] = ext_ref[:, HALO + t_new - POOL_HIST:HALO + t_new, :]


def _pool_sample(pool_in, state, w_pool_b, past, nb):
    n_seq, t_new, _ = pool_in.shape
    assert t_new % 8 == 0 and n_seq % nb == 0 and past >= POOL_HIST
    return pl.pallas_call(
        functools.partial(_pool_sample_kernel, t_new=t_new, past=past),
        grid=(n_seq // nb,),
        in_specs=[pl.BlockSpec((nb, t_new, POOL_W), lambda i: (i, 0, 0)),
                  pl.BlockSpec((nb, POOL_HIST, POOL_W), lambda i: (i, 0, 0)),
                  _const_spec((N_POOL_GROUPS, POOL_CH, POOL_CH))],
        out_specs=[pl.BlockSpec((nb * t_new, POOL_W), lambda i: (i, 0)),
                   pl.BlockSpec((nb, POOL_HIST, POOL_W), lambda i: (i, 0, 0))],
        out_shape=[jax.ShapeDtypeStruct((n_seq * t_new, POOL_W), F32),
                   jax.ShapeDtypeStruct((n_seq, POOL_HIST, POOL_W), F32)],
        scratch_shapes=[pltpu.VMEM((nb, HALO + t_new, POOL_W), F32)],
        compiler_params=_params("parallel"),
        name="pool_sample",
    )(pool_in, state, w_pool_b)


def _mix_kernel(x_ref, a_ref, m_ref, ga_ref, gp_ref, wo_ref, g2_ref, wq_ref, x1_ref, qm_ref):
    mix = jnp.concatenate([_rms(a_ref[...], ga_ref[...]), _rms(m_ref[...], gp_ref[...])], axis=1)
    x1 = x_ref[...] + _dot(mix.astype(BF16), wo_ref[...])
    x1_ref[...] = x1
    qm_ref[...] = _dot(_rms(x1, g2_ref[...]).astype(BF16), wq_ref[...]).astype(BF16)


def _mix(x, a, m, ga, gp, w_out_b, g2, w_mq_b, tm):
    n = x.shape[0]
    row = lambda w: pl.BlockSpec((tm, w), lambda i: (i, 0))
    return pl.pallas_call(
        _mix_kernel,
        grid=(n // tm,),
        in_specs=[row(D_MODEL), row(ATTN_W), row(POOL_W), _const_spec((1, ATTN_W)), _const_spec((1, POOL_W)),
                  _const_spec((D_MODEL, D_MODEL)), _const_spec((1, D_MODEL)), _const_spec((D_MODEL, MEM_W))],
        out_specs=[row(D_MODEL), row(MEM_W)],
        out_shape=[jax.ShapeDtypeStruct((n, D_MODEL), F32), jax.ShapeDtypeStruct((n, MEM_W), BF16)],
        compiler_params=_params("parallel"),
        name="mix_out",
    )(x, a, m, ga, gp, w_out_b, g2, w_mq_b)


def _cross_kernel(q_ref, kv_ref, o_ref, *, n_mem_blk, rows):
    for i in range(n_mem_blk):
        outs = []
        for h in range(MEM_HEADS):
            q = q_ref[i * rows:(i + 1) * rows, h * MEM_HEAD_DIM:(h + 1) * MEM_HEAD_DIM]
            k = kv_ref[i, pl.ds(h, N_MEM, stride=2 * MEM_HEADS), :].astype(BF16)
            v = kv_ref[i, pl.ds(MEM_HEADS + h, N_MEM, stride=2 * MEM_HEADS), :].astype(BF16)
            s = _dot_nt(q, k) * (MEM_HEAD_DIM ** -0.5)
            e = jnp.exp(s - jnp.max(s, axis=-1, keepdims=True))
            o = _dot(e.astype(BF16), v) / jnp.sum(e, axis=-1, keepdims=True)
            outs.append(o)
        o_ref[i * rows:(i + 1) * rows, :] = jnp.concatenate(outs, axis=1).astype(BF16)


def _cross(qm, mem_rows, rows, n_mem_blk):
    n = qm.shape[0]
    tm = rows * n_mem_blk
    return pl.pallas_call(
        functools.partial(_cross_kernel, n_mem_blk=n_mem_blk, rows=rows),
        grid=(n // tm,),
        in_specs=[pl.BlockSpec((tm, MEM_W), lambda i: (i, 0)),
                  pl.BlockSpec((n_mem_blk,) + mem_rows.shape[1:], lambda i: (i, 0, 0))],
        out_specs=pl.BlockSpec((tm, MEM_W), lambda i: (i, 0)),
        out_shape=jax.ShapeDtypeStruct((n, MEM_W), BF16),
        compiler_params=_params("parallel"),
        name="cross_attn",
    )(qm, mem_rows)


def _cross_shared_kernel(q_ref, k_ref, vt_ref, o_ref):
    outs = []
    for h in range(MEM_HEADS):
        lanes = slice(h * MEM_HEAD_DIM, (h + 1) * MEM_HEAD_DIM)
        s = _dot_nt(k_ref[:, lanes], q_ref[:, lanes]) * (MEM_HEAD_DIM ** -0.5)
        e = jnp.exp(s - jnp.max(s, axis=0, keepdims=True)).astype(BF16)
        acc = _dot(vt_ref[0, h], e)
        outs.append((acc[0:MEM_HEAD_DIM, :] / acc[MEM_HEAD_DIM:MEM_HEAD_DIM + 1, :]).T)
    o_ref[...] = jnp.concatenate(outs, axis=1).astype(BF16)


def _cross_shared(qm, mem_k, mem_vt, tm):
    n = qm.shape[0]
    tiles_per_mem = n // mem_vt.shape[0] // tm
    return pl.pallas_call(
        _cross_shared_kernel,
        grid=(n // tm,),
        in_specs=[pl.BlockSpec((tm, MEM_W), lambda i: (i, 0)),
                  pl.BlockSpec((N_MEM, MEM_W), lambda i: (i // tiles_per_mem, 0)),
                  pl.BlockSpec((1, MEM_HEADS, VT_ROWS, N_MEM), lambda i: (i // tiles_per_mem, 0, 0, 0))],
        out_specs=pl.BlockSpec((tm, MEM_W), lambda i: (i, 0)),
        out_shape=jax.ShapeDtypeStruct((n, MEM_W), BF16),
        compiler_params=_params("parallel"),
        name="cross_attn_shared",
    )(qm, mem_k, mem_vt)


def _ffn_kernel(x1_ref, o_ref, wmo_ref, g3_ref, wup_ref, wdn_ref, gf_ref, y_ref):
    x2 = x1_ref[...] + _dot(o_ref[...], wmo_ref[...])
    xn = _rms(x2, g3_ref[...]).astype(BF16)
    acc = x2
    for c in range(D_FF // FF_CHUNK):
        h = jnp.maximum(_dot(xn, wup_ref[:, c * FF_CHUNK:(c + 1) * FF_CHUNK]), 0.0)
        acc = acc + _dot((h * h).astype(BF16), wdn_ref[c * FF_CHUNK:(c + 1) * FF_CHUNK, :])
    y_ref[...] = _rms(acc, gf_ref[...])


def _ffn(x1, o, w_mo_b, g3, w_up_b, w_down_b, g_final, tm):
    n = x1.shape[0]
    row = lambda w: pl.BlockSpec((tm, w), lambda i: (i, 0))
    return pl.pallas_call(
        _ffn_kernel,
        grid=(n // tm,),
        in_specs=[row(D_MODEL), row(MEM_W), _const_spec((MEM_W, D_MODEL)), _const_spec((1, D_MODEL)),
                  _const_spec((D_MODEL, D_FF)), _const_spec((D_FF, D_MODEL)), _const_spec((1, D_MODEL))],
        out_specs=row(D_MODEL),
        out_shape=jax.ShapeDtypeStruct((n, D_MODEL), F32),
        compiler_params=_params("parallel"),
        name="ffn",
    )(x1, o, w_mo_b, g3, w_up_b, w_down_b, g_final)


def _tile(n, pref):
    t = min(n, pref)
    assert n % t == 0
    return t


def kernel(x_prompt, x_sample, cache_cmp_kv, cache_sel_kv, cache_win_kv, state_pool, cache_mem_kv, page_table, mem_prompt, w_in, pe_cmp, w_cmp1, w_cmp2, w_pool, g_attn_out, pool_scale, w_out, g_norm1, g_norm2, g_mem, w_mq, w_mkv, w_mo, g_norm3, w_up, w_down, g_final):
    assert w_in.shape[0] == 1, "single-layer trunk"
    B, S, _ = x_prompt.shape
    DB, T, _ = x_sample.shape
    past = page_table.shape[1] * PAGE_SIZE
    kv5 = (2, N_KV_HEADS, HEAD_DIM)
    r2 = lambda v: v.reshape(1, -1)

    w_in_p = _permute_w_in(w_in[0])
    w_out_b, w_mq_b, w_mkv_b, w_mo_b = (w[0].astype(BF16) for w in (w_out, w_mq, w_mkv, w_mo))
    w_up_b, w_down_b, w_pool_b = w_up[0].astype(BF16), w_down[0].astype(BF16), w_pool[0].astype(BF16)
    g1, g2, g3, gm, ga, gp, gf = (r2(v) for v in (g_norm1[0], g_norm2[0], g_norm3[0], g_mem[0], g_attn_out[0],
                                                  pool_scale[0], g_final))
    cw = _compress_weights(pe_cmp[0], w_cmp1[0], w_cmp2[0])

    np_tok = B * S
    tm_p = _tile(S, 512)
    xp = x_prompt.reshape(np_tok, D_MODEL)
    q, qr, cmp_t, sel_t, win_t, sel_b, win_b, gates, pool_in = _in_proj(
        xp, g1, w_in_p, jnp.arange(S), tm_p, n_seq=B)
    mem_f, mem_k, mem_vt = _mem_kv(mem_prompt.reshape(B * N_MEM, D_MODEL), gm, w_mkv_b)
    ckv = _compress_seq(cmp_t, cw, 64)
    a = _nsa_prompt(q, qr, gates, ckv, sel_b, win_b, B)
    m = _pool_prompt(pool_in, w_pool_b, B, tm_p)
    x1, qm = _mix(xp, a, m, ga, gp, w_out_b, g2, w_mq_b, tm_p)
    o = _cross_shared(qm, mem_k, mem_vt, tm_p)
    y_p = _ffn(x1, o, w_mo_b, g3, w_up_b, w_down_b, gf, tm_p)
    rows_first = lambda t: t.reshape(t.shape[:1] + kv5 + t.shape[-1:]).transpose(0, 4, 1, 2, 3)[None]
    p_state = (rows_first(cmp_t), rows_first(sel_t), rows_first(win_t[:, :, -min(WINDOW, S):]),
               pool_in.reshape(B, S, POOL_W)[None, :, -POOL_HIST:],
               mem_f.reshape(1, B, N_MEM, 2, MEM_HEADS, MEM_HEAD_DIM))

    ns_tok = DB * T
    tm_s = _tile(ns_tok, 512)
    xs = x_sample.reshape(ns_tok, D_MODEL)
    q, qr, cmp_kv, sel_kv, win_kv, _, _, gates, pool_in = _in_proj(
        xs, g1, w_in_p, jnp.tile(past + jnp.arange(T), DB), tm_s)
    r3 = lambda v: v.reshape(DB, T, v.shape[-1])
    ckv = _compress_paged(_pages_t(cache_cmp_kv[0]), page_table, cw, 64)
    a, s_win = _nsa_sample(r3(q), r3(qr), r3(gates), ckv, _pages_t(cache_sel_kv[0]), page_table,
                           r3(sel_kv), _pages_t(cache_win_kv[0]), r3(win_kv), 64, 2)
    m, s_pool = _pool_sample(r3(pool_in), state_pool[0], w_pool_b, past, _tile(DB, 16))
    x1, qm = _mix(xs, a.reshape(ns_tok, ATTN_W), m, ga, gp, w_out_b, g2, w_mq_b, tm_s)
    o = _cross(qm, cache_mem_kv[0].reshape(DB, N_MEM * 2 * MEM_HEADS, MEM_HEAD_DIM), T, _tile(DB, 4))
    y_s = _ffn(x1, o, w_mo_b, g3, w_up_b, w_down_b, gf, tm_s)
    s_state = (cmp_kv.reshape((1, DB, T) + kv5), sel_kv.reshape((1, DB, T) + kv5),
               rows_first(s_win), s_pool[None])

    return (y_p.reshape(B, S, D_MODEL), y_s.reshape(DB, T, D_MODEL)) + p_state + s_state
```

```python
import functools

import jax
import jax.numpy as jnp
from jax import lax
from jax.experimental import pallas as pl
from jax.experimental.pallas import tpu as pltpu

D_MODEL = 1024
PAGE_SIZE = 128
ATTN_W = 512
POOL_W = 512
HEAD_DIM = 64
N_HEADS = 8
N_KV_HEADS = 2
HEADS_PER_KV = N_HEADS // N_KV_HEADS
SCALE = HEAD_DIM ** -0.5
LOG2E = 1.4426950408889634
Q_SCALE = SCALE * LOG2E
ROT_DIM = HEAD_DIM // 4
ROPE_THETA = 500000.0
CMP_LEN = 32
CMP_STRIDE = 16
CMP_HID = HEAD_DIM
SEL_BLOCK = 64
SEL_TOPN = 16
WINDOW = 512
FORCE_BONUS = 1000.0
POOL_WINDOWS = (2, 4, 8, 16)
N_POOL_GROUPS = len(POOL_WINDOWS)
POOL_CH = POOL_W // N_POOL_GROUPS
POOL_HIST = max(POOL_WINDOWS) - 1
N_MEM = 256
MEM_HEADS = 4
MEM_HEAD_DIM = 128
MEM_W = MEM_HEADS * MEM_HEAD_DIM
D_FF = 4 * D_MODEL
EPS = 1e-6
KV_W = 2 * N_KV_HEADS * HEAD_DIM
N_GATES = 3 * N_HEADS
LANES = 128
VMEM_LIMIT = 56 * 1024 * 1024
QW = N_HEADS * LANES
TQ = 128
NEG = -1e30
CHUNK_W = CMP_STRIDE * KV_W
CMP_R = CMP_LEN // CMP_STRIDE
CHUNKS_PER_PAGE = PAGE_SIZE // CMP_STRIDE
HALO = 16
FF_CHUNK = 1024

F32 = jnp.float32
BF16 = jnp.bfloat16


def _rms(x, g):
    return x * lax.rsqrt(jnp.mean(x * x, axis=-1, keepdims=True) + EPS) * g


def _dot(a, b):
    return jnp.dot(a, b, preferred_element_type=F32)


def _dot_nt(a, b):
    return lax.dot_general(a, b, (((1,), (1,)), ((), ())), preferred_element_type=F32)


ONES_ROWS = 16
VT_ROWS = LANES + ONES_ROWS


def _ones_rows(n):
    return jnp.where(lax.broadcasted_iota(jnp.int32, (ONES_ROWS, n), 0) == 0, 1.0, 0.0).astype(BF16)


def _const_spec(shape):
    n = len(shape)
    return pl.BlockSpec(shape, lambda *_: (0,) * n, pipeline_mode=pl.Buffered(1))


def _params(*sem):
    return pltpu.CompilerParams(dimension_semantics=sem, vmem_limit_bytes=VMEM_LIMIT)


X_Q, X_CMP, X_SEL, X_WIN, X_POOL, X_GATE = 0, 1024, 1280, 1536, 1792, 2304
X_IN = X_GATE + LANES


def _permute_w_in(w_in):
    q, kc, ks, kw, gl, pool = jnp.split(w_in, (512, 768, 1024, 1280, 1280 + N_GATES), axis=1)
    z = jnp.zeros((D_MODEL, HEAD_DIM), w_in.dtype)
    qx = []
    for h in range(N_HEADS):
        qh = q[:, h * HEAD_DIM:(h + 1) * HEAD_DIM]
        qx += [qh, z] if h // HEADS_PER_KV == 0 else [z, qh]
    pad = jnp.zeros((D_MODEL, LANES - N_GATES), w_in.dtype)
    return jnp.concatenate(qx + [kc, ks, kw, pool, gl, pad], axis=1).astype(BF16)


def _rope_tables(pos):
    half = ROT_DIM // 2
    inv = ROPE_THETA ** (-jnp.arange(half, dtype=F32) / half)
    ang = pos.astype(F32)[:, None] * inv[None, :]
    cos, sin = jnp.cos(ang), jnp.sin(ang)
    z = jnp.zeros_like(cos)
    rest0 = jnp.zeros((pos.shape[0], HEAD_DIM - ROT_DIM), F32)
    cs = jnp.concatenate([cos, cos, rest0 + 1.0], axis=1)
    sa = jnp.concatenate([-sin, z, rest0], axis=1)
    sb = jnp.concatenate([z, sin, rest0], axis=1)
    return tuple(jnp.tile(t, (1, LANES // HEAD_DIM)) for t in (cs, sa, sb))


def _rope(v, cs, sa, sb):
    n = v.shape[1] // LANES
    w = v.shape[1]
    cs, sa, sb = (jnp.tile(t, (1, n)) for t in (cs, sa, sb))
    return v * cs + pltpu.roll(v, w - ROT_DIM // 2, 1) * sa + pltpu.roll(v, ROT_DIM // 2, 1) * sb


def _in_proj_kernel(x_ref, g_ref, w_ref, cs_ref, sa_ref, sb_ref,
                    q_ref, qr_ref, cmp_ref, sel_ref, win_ref, selb_ref, winb_ref, gate_ref, pool_ref, *, feature_major):
    xb = _rms(x_ref[...], g_ref[...]).astype(BF16)
    cs, sa, sb = cs_ref[...], sa_ref[...], sb_ref[...]

    def put_state(ref, v):
        if feature_major:
            ref[0] = jnp.concatenate([v[:, c * LANES:(c + 1) * LANES].T for c in range(KV_W // LANES)], axis=0)
        else:
            ref[...] = v

    q = _dot(xb, w_ref[:, X_Q:X_CMP])
    q_ref[...] = (q * Q_SCALE).astype(BF16)
    qr_ref[...] = (_rope(q, cs, sa, sb) * Q_SCALE).astype(BF16)
    put_state(cmp_ref, _dot(xb, w_ref[:, X_CMP:X_SEL]))
    for lo, f_ref, b_ref in ((X_SEL, sel_ref, selb_ref), (X_WIN, win_ref, winb_ref)):
        kv = _dot(xb, w_ref[:, lo:lo + KV_W])
        kv = jnp.concatenate([_rope(kv[:, :LANES], cs, sa, sb), kv[:, LANES:]], axis=1)
        put_state(f_ref, kv)
        b_ref[...] = kv.astype(BF16)
    pool_ref[...] = _dot(xb, w_ref[:, X_POOL:X_GATE])
    gate_ref[...] = jax.nn.sigmoid(_dot(xb, w_ref[:, X_GATE:X_IN]))


def _in_proj(x, g1, w_in_p, pos_rows, tm, n_seq=None):
    n = x.shape[0]
    cs, sa, sb = _rope_tables(pos_rows)
    row = lambda w: pl.BlockSpec((tm, w), lambda i: (i, 0))
    tab = pl.BlockSpec((tm, LANES), lambda i: (i % (pos_rows.shape[0] // tm), 0))
    sds = lambda w, dt: jax.ShapeDtypeStruct((n, w), dt)
    if n_seq is None:
        st_spec, st_shape = row(KV_W), sds(KV_W, F32)
    else:
        nt = n // n_seq // tm
        st_spec = pl.BlockSpec((1, KV_W, tm), lambda i: (i // nt, 0, i % nt))
        st_shape = jax.ShapeDtypeStruct((n_seq, KV_W, n // n_seq), F32)
    return pl.pallas_call(
        functools.partial(_in_proj_kernel, feature_major=n_seq is not None),
        grid=(n // tm,),
        in_specs=[row(D_MODEL), _const_spec((1, D_MODEL)), _const_spec((D_MODEL, X_IN)),
                  tab, tab, tab],
        out_specs=[row(QW), row(QW), st_spec, st_spec, st_spec, row(KV_W), row(KV_W), row(LANES), row(POOL_W)],
        out_shape=[sds(QW, BF16), sds(QW, BF16), st_shape, st_shape, st_shape,
                   sds(KV_W, BF16), sds(KV_W, BF16), sds(LANES, F32), sds(POOL_W, F32)],
        compiler_params=_params("parallel"),
        name="in_proj",
    )(x, g1, w_in_p, cs, sa, sb)


def _mem_kv_kernel(m_ref, g_ref, w_ref, o_ref, kb_ref, vt_ref):
    kv = _dot(_rms(m_ref[...], g_ref[...]).astype(BF16), w_ref[...])
    o_ref[...] = kv
    kb_ref[...] = kv[:, :MEM_W].astype(BF16)
    for h in range(MEM_HEADS):
        v = kv[:, MEM_W + h * MEM_HEAD_DIM:MEM_W + (h + 1) * MEM_HEAD_DIM]
        vt_ref[0, h] = jnp.concatenate([v.T.astype(BF16), _ones_rows(N_MEM)], axis=0)


def _mem_kv(mem, g_mem, w_mkv_b):
    n = mem.shape[0]
    row = lambda w: pl.BlockSpec((N_MEM, w), lambda i: (i, 0))
    return pl.pallas_call(
        _mem_kv_kernel,
        grid=(n // N_MEM,),
        in_specs=[row(D_MODEL), _const_spec((1, D_MODEL)), _const_spec((D_MODEL, 2 * MEM_W))],
        out_specs=[row(2 * MEM_W), row(MEM_W),
                   pl.BlockSpec((1, MEM_HEADS, VT_ROWS, N_MEM), lambda i: (i, 0, 0, 0))],
        out_shape=[jax.ShapeDtypeStruct((n, 2 * MEM_W), F32), jax.ShapeDtypeStruct((n, MEM_W), BF16),
                   jax.ShapeDtypeStruct((n // N_MEM, MEM_HEADS, VT_ROWS, N_MEM), BF16)],
        compiler_params=_params("parallel"),
        name="mem_kv",
    )(mem, g_mem, w_mkv_b)


def _compress_weights(pe, w1, w2):
    w1r = w1.reshape(2, CMP_R, CMP_STRIDE, HEAD_DIM, CMP_HID)
    eye_g = jnp.eye(N_KV_HEADS, dtype=w1.dtype)
    w1c = jnp.einsum('crsdh,gy->csgdryh', w1r, eye_g).reshape(2, CHUNK_W // 2, CMP_R * LANES)
    pe_term = jnp.einsum('cld,cldh->ch', pe, w1.reshape(2, CMP_LEN, HEAD_DIM, CMP_HID))
    bias = jnp.broadcast_to(pe_term[:, None, :], (2, N_KV_HEADS, CMP_HID)).reshape(2, 1, LANES)
    w2c = jnp.einsum('chd,gy->cghyd', w2, eye_g).reshape(2, LANES, LANES)
    return w1c.astype(BF16), bias, w2c.astype(BF16)


def _compress_kernel(*refs, n_in, ppr, has_pt):
    if has_pt:
        refs = refs[1:]
    x_refs = refs[:n_in]
    w1_ref, b_ref, w2_ref, o_ref, carry_ref = refs[n_in:]

    @pl.when(pl.program_id(1) == 0)
    def _():
        carry_ref[...] = jnp.zeros_like(carry_ref)

    r_i = lax.broadcasted_iota(jnp.int32, (PAGE_SIZE, PAGE_SIZE), 0)
    t_i = lax.broadcasted_iota(jnp.int32, (PAGE_SIZE, PAGE_SIZE), 1)
    src = CMP_STRIDE * (r_i & (CHUNKS_PER_PAGE - 1)) + lax.shift_right_logical(r_i, 3)
    pick = jnp.where(t_i == src, 1.0, 0.0).astype(BF16)
    pages = [_dot_nt(pick, r[0, :, j * PAGE_SIZE:(j + 1) * PAGE_SIZE].astype(BF16))
             for r in x_refs for j in range(ppr)]
    rows = len(pages) * CHUNKS_PER_PAGE
    row = lax.broadcasted_iota(jnp.int32, (rows, LANES), 0)
    outs = []
    for c in range(KV_W // LANES):
        lanes = slice(c * LANES, (c + 1) * LANES)
        x = jnp.concatenate(
            [jnp.concatenate([pg[s * CHUNKS_PER_PAGE:(s + 1) * CHUNKS_PER_PAGE, lanes] for pg in pages], axis=0)
             for s in range(CMP_STRIDE)], axis=1)
        u = _dot(x.astype(BF16), w1_ref[c])
        u0, u1 = u[:, :LANES], u[:, LANES:]
        prev = jnp.where(row == 0, carry_ref[0:1, lanes], pltpu.roll(u0, 1, 0))
        carry_ref[:, lanes] = jnp.broadcast_to(u0[rows - 1:rows, :], (carry_ref.shape[0], LANES))
        h = prev + u1 + b_ref[c]
        outs.append(_dot(jax.nn.gelu(h).astype(BF16), w2_ref[c]))
    o_ref[0] = jnp.concatenate(outs, axis=1).astype(BF16)


def _compress_call(n_seq, n_pages, n_in, ppr, has_pt, x_spec_fn):
    P = n_in * ppr
    assert n_pages % P == 0
    grid = (n_seq, n_pages // P)
    in_specs = [x_spec_fn(k) for k in range(n_in)] + [
        _const_spec((2, CHUNK_W // 2, CMP_R * LANES)), _const_spec((2, 1, LANES)), _const_spec((2, LANES, LANES))]
    out_spec = pl.BlockSpec((1, P * CHUNKS_PER_PAGE, KV_W), (lambda b, i, *_: (b, i, 0)))
    kern = functools.partial(_compress_kernel, n_in=n_in, ppr=ppr, has_pt=has_pt)
    common = dict(out_shape=jax.ShapeDtypeStruct((n_seq, n_pages * CHUNKS_PER_PAGE, KV_W), BF16),
                  compiler_params=_params("parallel", "arbitrary"), name="compress")
    scratch = [pltpu.VMEM((8, KV_W), F32)]
    if has_pt:
        return pl.pallas_call(kern, grid_spec=pltpu.PrefetchScalarGridSpec(
            num_scalar_prefetch=1, grid=grid, in_specs=in_specs, out_specs=out_spec, scratch_shapes=scratch), **common)
    return pl.pallas_call(kern, grid=grid, in_specs=in_specs, out_specs=out_spec, scratch_shapes=scratch, **common)


def _compress_seq(cmp_t, cw, pages_per_step):
    n_seq, _, seq = cmp_t.shape
    n_pages = seq // PAGE_SIZE
    P = min(pages_per_step, n_pages)
    spec = lambda k: pl.BlockSpec((1, KV_W, P * PAGE_SIZE), lambda b, i: (b, 0, i))
    return _compress_call(n_seq, n_pages, 1, P, False, spec)(cmp_t, *cw)


def _pages_t(cache):
    return cache.transpose(0, 2, 3, 4, 1).reshape(cache.shape[0], KV_W, cache.shape[1])


def _compress_paged(cache_t, page_table, cw, pages_per_step):
    n_seq, n_pages = page_table.shape
    P = min(pages_per_step, n_pages)
    spec = lambda k: pl.BlockSpec((1, KV_W, PAGE_SIZE), lambda b, i, pt: (pt[b, i * P + k], 0, 0))
    return _compress_call(n_seq, n_pages, P, 1, True, spec)(page_table, *([cache_t] * P), *cw)


def _overlap_t(nch, n_sel_pad):
    m = jnp.arange(nch)[None, :]
    j = jnp.arange(n_sel_pad)[:, None]
    lo = (m - 1) * CMP_STRIDE
    ov = jnp.maximum(jnp.minimum(lo + CMP_LEN, (j + 1) * SEL_BLOCK) - jnp.maximum(lo, j * SEL_BLOCK), 0)
    return jnp.where(m >= 1, ov.astype(F32) / CMP_LEN, 0.0).astype(BF16)


def _heads_to_rows(x):
    return jnp.concatenate([x[:, h * LANES:(h + 1) * LANES] for h in range(N_HEADS)], axis=0)


def _topk_bias(imp, pos, n_blk, k_top, axis):
    j = lax.broadcasted_iota(jnp.int32, imp.shape, axis)
    cur = lax.shift_right_logical(pos, 6)
    forced = (j == 0) | (j == cur) | (j == cur - 1)
    score = jnp.where(j * SEL_BLOCK <= pos, imp + jnp.where(forced, FORCE_BONUS, 0.0), -1e9)
    score = jnp.where(j < n_blk, score, -jnp.inf)
    bias = jnp.full(imp.shape, NEG, F32)
    for _ in range(k_top):
        mx = jnp.max(score, axis=axis, keepdims=True)
        idx = jnp.min(jnp.where(score == mx, j, LANES), axis=axis, keepdims=True)
        hit = j == idx
        bias = jnp.where(hit, 0.0, bias)
        score = jnp.where(hit, -jnp.inf, score)
    return bias


def _split_dot(dot, ps):
    hi = ps.astype(BF16)
    lo = (ps - hi.astype(F32)).astype(BF16)
    return dot(hi) + dot(lo)


def _nsa_prompt_kernel(q_ref, qr_ref, g_ref, ckv_ref, ksel_ref, kwin_ref, ovt_ref, a_ref,
                       kx_ref, vts_ref, vtw_ref, lc_ref, *, tk, n_sel, k_top):
    i = pl.program_id(1)
    q0 = pl.multiple_of(i * TQ, TQ)
    nc = ckv_ref.shape[1]

    @pl.when(i == 0)
    def _():
        kx_ref[...] = jnp.zeros_like(kx_ref)
        vts_ref[...] = jnp.zeros_like(vts_ref)
        vtw_ref[...] = jnp.zeros_like(vtw_ref)
        lc_ref[0:LANES, :] = ckv_ref[0, :, LANES:2 * LANES].astype(F32).T.astype(BF16)
        lc_ref[LANES:2 * LANES, :] = ovt_ref[...]
        lc_ref[2 * LANES:, :] = _ones_rows(nc)

    key_t = q0 + lax.broadcasted_iota(jnp.int32, (TQ, 1), 0)
    onehot = jnp.where(lax.shift_right_logical(key_t, 6) == lax.broadcasted_iota(jnp.int32, (TQ, LANES), 1), 1.0, 0.0)
    kx_ref[pl.ds(q0, TQ), :] = jnp.concatenate([ksel_ref[pl.ds(q0, TQ), 0:LANES], onehot.astype(BF16)], axis=1)
    for k_ref, vt_ref in ((ksel_ref, vts_ref), (kwin_ref, vtw_ref)):
        v = k_ref[pl.ds(q0, TQ), LANES:2 * LANES].astype(F32)
        vt_ref[:, pl.ds(q0, TQ)] = jnp.concatenate([v.T.astype(BF16), _ones_rows(TQ)], axis=0)

    nl = N_HEADS * TQ
    lane = lax.broadcasted_iota(jnp.int32, (1, nl), 1)
    pos = q0 + (lane & (TQ - 1))
    qc = _heads_to_rows(q_ref)
    qr = _heads_to_rows(qr_ref)

    m_idx = lax.broadcasted_iota(jnp.int32, (nc, 1), 0)
    c_end = jnp.where(m_idx >= 1, m_idx * CMP_STRIDE + (CMP_STRIDE - 1), 2 ** 30)
    s = jnp.where(c_end <= pos, _dot_nt(ckv_ref[0, :, 0:LANES], qc), NEG)
    acc = _dot(lc_ref[...], jnp.exp2(s - jnp.max(s, axis=0, keepdims=True)).astype(BF16))
    rden = jnp.where(pos >= CMP_LEN - 1, 1.0 / jnp.maximum(acc[2 * LANES:2 * LANES + 1, :], 1e-30), 0.0)
    oc_t = acc[0:LANES, :] * rden
    imp_h = acc[LANES:2 * LANES, :] * rden

    wk = WINDOW + TQ
    w0 = pl.multiple_of(jnp.maximum(q0 - WINDOW, 0), TQ)
    sw = _dot_nt(kwin_ref[pl.ds(w0, wk), 0:LANES], qr)
    kp = w0 + lax.broadcasted_iota(jnp.int32, (wk, 1), 0)
    sw = jnp.concatenate([jnp.where((kp[:TQ] <= pos) & (kp[:TQ] >= pos - WINDOW), sw[:TQ], NEG),
                         jnp.where(kp[TQ:] <= pos, sw[TQ:], NEG)], axis=0)
    accw = _dot(vtw_ref[:, pl.ds(w0, wk)], jnp.exp2(sw - jnp.max(sw, axis=0, keepdims=True)).astype(BF16))
    ow_t = accw[0:LANES, :] / jnp.maximum(accw[LANES:LANES + 1, :], 1e-30)

    imp_t = jnp.concatenate(
        [sum(imp_h[:, (g * HEADS_PER_KV + hh) * TQ:(g * HEADS_PER_KV + hh + 1) * TQ] for hh in range(HEADS_PER_KV))
         for g in range(N_KV_HEADS)], axis=1)
    bias_t = _topk_bias(imp_t, pos[:, :N_KV_HEADS * TQ], n_sel, k_top, 0)
    biases = [bias_t[:, g * TQ:(g + 1) * TQ].T.astype(BF16) for g in range(N_KV_HEADS)]
    bias_rows = jnp.concatenate([biases[h // HEADS_PER_KV] for h in range(N_HEADS)], axis=0)
    qs = jnp.concatenate([qr, bias_rows], axis=1)

    def sel_chunk(c0, masked, carry):
        m, acc = carry
        s = _dot_nt(kx_ref[pl.ds(c0, tk), :], qs)
        if masked:
            s = jnp.where(c0 + lax.broadcasted_iota(jnp.int32, (tk, 1), 0) <= pos, s, NEG)
        m_new = jnp.maximum(m, jnp.max(s, axis=0, keepdims=True))
        p = jnp.exp2(s - m_new).astype(BF16)
        acc = jnp.exp2(m - m_new) * acc + _dot(vts_ref[:, pl.ds(c0, tk)], p)
        return m_new, acc

    n_full = q0 // tk
    carry = (jnp.full((1, nl), NEG, F32), jnp.zeros((VT_ROWS, nl), F32))
    one = lambda c, cr: sel_chunk(pl.multiple_of(c * tk, tk), False, cr)
    carry = lax.fori_loop(0, n_full // 2, lambda c, cr: one(2 * c + 1, one(2 * c, cr)), carry)
    carry = lax.fori_loop(n_full // 2 * 2, n_full, one, carry)
    _, acc = sel_chunk(pl.multiple_of(n_full * tk, tk), True, carry)
    os_t = acc[0:LANES, :] / jnp.maximum(acc[LANES:LANES + 1, :], 1e-30)

    g_t = g_ref[...].T
    outs = []
    for h in range(N_HEADS):
        g = h // HEADS_PER_KV
        blk_h = lambda o: o[g * HEAD_DIM:(g + 1) * HEAD_DIM, h * TQ:(h + 1) * TQ]
        gate = lambda j: g_t[3 * h + j:3 * h + j + 1, :]
        outs.append(gate(0) * blk_h(oc_t) + gate(1) * blk_h(os_t) + gate(2) * blk_h(ow_t))
    a_ref[...] = jnp.concatenate(outs, axis=0).T


def _nsa_prompt(q, qr, gates, ckv, ksel, kwin, n_seq):
    n = q.shape[0]
    seq = n // n_seq
    nch = ckv.shape[1]
    n_sel = seq // SEL_BLOCK
    assert seq % TQ == 0 and seq >= WINDOW + TQ and n_sel <= LANES and seq % SEL_BLOCK == 0
    tk = min(512, seq)
    assert seq % tk == 0
    nt = seq // TQ
    tile = lambda w: pl.BlockSpec((TQ, w), lambda b, i: (b * nt + i, 0))
    whole = lambda: pl.BlockSpec((seq, KV_W), lambda b, i: (b, 0), pipeline_mode=pl.Buffered(1))
    kern = functools.partial(_nsa_prompt_kernel, tk=tk, n_sel=n_sel, k_top=min(SEL_TOPN, n_sel))
    return pl.pallas_call(
        kern,
        grid=(n_seq, nt),
        in_specs=[tile(QW), tile(QW), tile(LANES),
                  pl.BlockSpec((1, nch, KV_W), lambda b, i: (b, 0, 0)),
                  whole(), whole(), _const_spec((LANES, nch))],
        out_specs=tile(ATTN_W),
        out_shape=jax.ShapeDtypeStruct((n, ATTN_W), F32),
        scratch_shapes=[pltpu.VMEM((seq, 2 * LANES), BF16), pltpu.VMEM((VT_ROWS, seq), BF16),
                        pltpu.VMEM((VT_ROWS, seq), BF16), pltpu.VMEM((2 * LANES + ONES_ROWS, nch), BF16)],
        compiler_params=_params("arbitrary", "arbitrary"),
        name="nsa_prompt",
    )(q, qr, gates, ckv, ksel, kwin, _overlap_t(nch, LANES))


def _softmax_rows2(s_a, mask_a, s_b, mask_b):
    s_a = jnp.where(mask_a, s_a, NEG)
    s_b = jnp.where(mask_b, s_b, NEG)
    mx = jnp.maximum(jnp.max(s_a, axis=1, keepdims=True), jnp.max(s_b, axis=1, keepdims=True))
    e_a = jnp.where(mask_a, jnp.exp2(s_a - mx), 0.0)
    e_b = jnp.where(mask_b, jnp.exp2(s_b - mx), 0.0)
    den = jnp.sum(e_a, axis=1, keepdims=True) + jnp.sum(e_b, axis=1, keepdims=True)
    return e_a, e_b, jnp.maximum(den, 1e-30)


def _pad_rows(x, rows):
    return jnp.concatenate([x, jnp.zeros((rows - x.shape[0], x.shape[1]), x.dtype)], axis=0)


def _nsa_sample_kernel(*refs, n_seq, n_pg, t_new, past, n_cb, k_cache, single_step):
    pg_refs = refs[1:1 + n_seq * n_pg]
    (q_ref, qr_ref, g_ref, ckv_ref, seln_ref, cwin_ref, winn_ref, ovl_ref,
     a_ref, swin_ref, qs_ref, oc_ref, ow_ref, m_ref, l_ref, acc_ref) = refs[1 + n_seq * n_pg:]
    j = pl.program_id(1)
    nr = N_HEADS * t_new
    row_t = lax.rem(lax.broadcasted_iota(jnp.int32, (nr, 1), 0), t_new)
    pos = past + row_t
    lane = lax.broadcasted_iota(jnp.int32, (1, LANES), 1)

    seqs = range(n_seq)

    def first():
        qc = [_heads_to_rows(q_ref[b].astype(F32)).astype(BF16) for b in seqs]
        qr = [_heads_to_rows(qr_ref[b].astype(F32)).astype(BF16) for b in seqs]
        nc = ckv_ref.shape[1]
        m_idx = lax.broadcasted_iota(jnp.int32, (1, nc), 1)
        cmask = (m_idx >= 1) & (m_idx * CMP_STRIDE + (CMP_STRIDE - 1) <= pos)
        sc = [jnp.where(cmask, _dot_nt(qc[b], ckv_ref[b, :, 0:LANES]), NEG) for b in seqs]
        ec = [jnp.where(cmask, jnp.exp2(s - jnp.max(s, axis=1, keepdims=True)), 0.0) for s in sc]
        p_c = [e / jnp.maximum(jnp.sum(e, axis=1, keepdims=True), 1e-30) for e in ec]
        for b in seqs:
            oc_ref[b] = _dot(p_c[b].astype(BF16), ckv_ref[b, :, LANES:2 * LANES])
        cw = [jnp.concatenate([cwin_ref[b, c * LANES:(c + 1) * LANES, :].T for c in range(KV_W // LANES)], axis=1)
              for b in seqs]
        wn = [_pad_rows(winn_ref[b], LANES) for b in seqs]
        r_idx = lax.broadcasted_iota(jnp.int32, (1, WINDOW), 1)
        sm = [_softmax_rows2(_dot_nt(qr[b], cw[b][:, 0:LANES].astype(BF16)), r_idx >= row_t,
                             _dot_nt(qr[b], wn[b][:, 0:LANES].astype(BF16)), lane <= row_t) for b in seqs]
        for b in seqs:
            e_a, e_b, den = sm[b]
            ow_ref[b] = (_dot(e_a.astype(BF16), cw[b][:, LANES:].astype(BF16))
                         + _dot(e_b.astype(BF16), wn[b][:, LANES:].astype(BF16))) / den
            shifted = pltpu.roll(cwin_ref[b], WINDOW - t_new, 1)
            fresh = pltpu.roll(wn[b].T, LANES - t_new, 1)
            tail = jnp.where(lane >= LANES - t_new, fresh, shifted[:, WINDOW - LANES:])
            swin_ref[b] = jnp.concatenate([shifted[:, :WINDOW - LANES], tail], axis=1)
        imps = [_split_dot(lambda v: _dot(v, ovl_ref[...]),
                           sum(p_c[b][(g * HEADS_PER_KV + hh) * t_new:(g * HEADS_PER_KV + hh + 1) * t_new, :]
                               for hh in range(HEADS_PER_KV)))
                for b in seqs for g in range(N_KV_HEADS)]
        imp_t = _pad_rows(jnp.concatenate(imps, axis=0), LANES).T
        bias = _topk_bias(imp_t, past + lax.rem(lane, t_new), n_cb, k_cache, 0).T
        for b in seqs:
            rows_b = [bias[(b * N_KV_HEADS + h // HEADS_PER_KV) * t_new:(b * N_KV_HEADS + h // HEADS_PER_KV + 1) * t_new, :]
                      for h in range(N_HEADS)]
            qs_ref[b] = jnp.concatenate([qr[b], jnp.concatenate(rows_b, axis=0).astype(BF16)], axis=1)
            m_ref[b] = jnp.full(m_ref.shape[1:], NEG, F32)
            l_ref[b] = jnp.zeros(l_ref.shape[1:], F32)
            acc_ref[b] = jnp.zeros(acc_ref.shape[1:], F32)

    def online(ss, pvs):
        m_new = [jnp.maximum(m_ref[b, :, 0:1], jnp.max(ss[b], axis=1, keepdims=True)) for b in seqs]
        ps = [jnp.exp2(ss[b] - m_new[b]) for b in seqs]
        pv = [pvs[b](ps[b].astype(BF16)) for b in seqs]
        for b in seqs:
            alpha = jnp.exp2(m_ref[b, :, 0:1] - m_new[b])
            l_ref[b] = jnp.broadcast_to(alpha * l_ref[b, :, 0:1] + jnp.sum(ps[b], axis=1, keepdims=True), l_ref.shape[1:])
            acc_ref[b] = alpha * acc_ref[b] + pv[b]
            m_ref[b] = jnp.broadcast_to(m_new[b], m_ref.shape[1:])

    def pages():
        cat = lambda b, lo: jnp.concatenate(
            [r[0, lo:lo + LANES, :] for r in pg_refs[b * n_pg:(b + 1) * n_pg]], axis=1).astype(BF16)
        k_t = [cat(b, 0) for b in seqs]
        v_t = [cat(b, LANES) for b in seqs]
        nk = k_t[0].shape[1]
        blk = lax.shift_right_logical(j * nk + lax.broadcasted_iota(jnp.int32, (1, nk), 1), 6)
        onehot_t = jnp.where(blk == lax.broadcasted_iota(jnp.int32, (LANES, 1), 0), 1.0, 0.0).astype(BF16)
        online([_dot(qs_ref[b], jnp.concatenate([k_t[b], onehot_t], axis=0)) for b in seqs],
               [functools.partial(_dot_nt, b=v_t[b]) for b in seqs])

    def last():
        sn = [_pad_rows(seln_ref[b], LANES) for b in seqs]
        online([jnp.where(lane <= row_t, _dot_nt(qs_ref[b, :, 0:LANES], sn[b][:, 0:LANES].astype(BF16)), NEG)
                for b in seqs],
               [functools.partial(_dot, b=sn[b][:, LANES:].astype(BF16)) for b in seqs])
        for b in seqs:
            os = acc_ref[b] / jnp.maximum(l_ref[b, :, 0:1], 1e-30)
            gts = g_ref[b]
            outs = []
            for h in range(N_HEADS):
                g = h // HEADS_PER_KV
                blk_h = lambda o: o[h * t_new:(h + 1) * t_new, g * HEAD_DIM:(g + 1) * HEAD_DIM]
                gate = lambda k: gts[:, 3 * h + k:3 * h + k + 1]
                outs.append(gate(0) * blk_h(oc_ref[b]) + gate(1) * blk_h(os) + gate(2) * blk_h(ow_ref[b]))
            a_ref[b] = jnp.concatenate(outs, axis=1)

    if single_step:
        first()
        pages()
        last()
    else:
        pl.when(j == 0)(first)
        pages()
        pl.when(j == pl.num_programs(1) - 1)(last)


def _nsa_sample(q, qr, gates, ckv, cache_sel, page_table, sel_new, cache_win, win_new, pages_per_step, seqs_per_step):
    n_all, t_new = q.shape[:2]
    n_pages = page_table.shape[1]
    past = n_pages * PAGE_SIZE
    n_cb = past // SEL_BLOCK
    assert cache_win.shape[2] == WINDOW and n_cb <= LANES and t_new <= SEL_BLOCK and t_new % 8 == 0
    k_cache = min(SEL_TOPN, n_cb + 1) - 1
    assert k_cache >= 2
    P = min(pages_per_step, n_pages)
    NS = min(seqs_per_step, n_all)
    assert n_pages % P == 0 and n_all % NS == 0
    nch = ckv.shape[1]
    nr = N_HEADS * t_new
    per_seq = lambda r, w: pl.BlockSpec((NS, r, w), lambda b, j, pt: (b, 0, 0))
    page = lambda s, k: pl.BlockSpec((1, KV_W, PAGE_SIZE), lambda b, j, pt: (pt[b * NS + s, j * P + k], 0, 0))
    kern = functools.partial(_nsa_sample_kernel, n_seq=NS, n_pg=P, t_new=t_new, past=past, n_cb=n_cb, k_cache=k_cache,
                             single_step=n_pages == P)
    return pl.pallas_call(
        kern,
        grid_spec=pltpu.PrefetchScalarGridSpec(
            num_scalar_prefetch=1,
            grid=(n_all // NS, n_pages // P),
            in_specs=[page(s, k) for s in range(NS) for k in range(P)] + [
                per_seq(t_new, QW), per_seq(t_new, QW), per_seq(t_new, LANES), per_seq(nch, KV_W),
                per_seq(t_new, KV_W), per_seq(KV_W, WINDOW), per_seq(t_new, KV_W),
                pl.BlockSpec((nch, LANES), lambda b, j, pt: (0, 0), pipeline_mode=pl.Buffered(1))],
            out_specs=[per_seq(t_new, ATTN_W), per_seq(KV_W, WINDOW)],
            scratch_shapes=[pltpu.VMEM((NS, nr, 2 * LANES), BF16)] + [pltpu.VMEM((NS, nr, LANES), F32)] * 5),
        out_shape=[jax.ShapeDtypeStruct((n_all, t_new, ATTN_W), F32),
                   jax.ShapeDtypeStruct((n_all, KV_W, WINDOW), F32)],
        compiler_params=_params("arbitrary", "arbitrary"),
        name="nsa_sample",
    )(page_table, *([cache_sel] * (NS * P)), q, qr, gates, ckv, sel_new, cache_win, win_new,
      _overlap_t(nch, LANES).T)


def _pool_windows(ext_ref, n_rows, pos, w_ref, lead):
    outs = []
    for gi, w in enumerate(POOL_WINDOWS):
        lanes = slice(gi * POOL_CH, (gi + 1) * POOL_CH)
        tot = None
        for k in range(w):
            v = ext_ref[lead + (slice(HALO - k, HALO - k + n_rows), lanes)]
            tot = v if tot is None else tot + v
        cnt = jnp.minimum(pos + 1, w).astype(F32)
        d = tot / cnt - ext_ref[lead + (slice(HALO, HALO + n_rows), lanes)]
        outs.append((d, w_ref[gi]))
    return outs


def _pool_prompt_kernel(x_ref, h_ref, w_ref, o_ref, ext_ref, *, tm):
    i = pl.program_id(1)
    ext_ref[0:HALO, :] = jnp.where(i == 0, 0.0, h_ref[...])
    ext_ref[HALO:, :] = x_ref[...]
    pos = i * tm + lax.broadcasted_iota(jnp.int32, (tm, 1), 0)
    res = [_dot(d.astype(BF16), w) for d, w in _pool_windows(ext_ref, tm, pos, w_ref, ())]
    o_ref[...] = jnp.concatenate(res, axis=1)


def _pool_prompt(pool_in, w_pool_b, n_seq, tm):
    n = pool_in.shape[0]
    nt = n // n_seq // tm
    hb = tm // HALO
    return pl.pallas_call(
        functools.partial(_pool_prompt_kernel, tm=tm),
        grid=(n_seq, nt),
        in_specs=[pl.BlockSpec((tm, POOL_W), lambda b, i: (b * nt + i, 0)),
                  pl.BlockSpec((HALO, POOL_W), lambda b, i: (jnp.maximum((b * nt + i) * hb - 1, 0), 0)),
                  _const_spec((N_POOL_GROUPS, POOL_CH, POOL_CH))],
        out_specs=pl.BlockSpec((tm, POOL_W), lambda b, i: (b * nt + i, 0)),
        out_shape=jax.ShapeDtypeStruct((n, POOL_W), F32),
        scratch_shapes=[pltpu.VMEM((HALO + tm, POOL_W), F32)],
        compiler_params=_params("parallel", "arbitrary"),
        name="pool_prompt",
    )(pool_in, pool_in, w_pool_b)


def _pool_sample_kernel(x_ref, st_ref, w_ref, o_ref, ns_ref, ext_ref, *, t_new, past):
    nb = x_ref.shape[0]
    ext_ref[:, HALO - POOL_HIST:HALO, :] = st_ref[...]
    ext_ref[:, HALO:, :] = x_ref[...]
    pos = past + lax.broadcasted_iota(jnp.int32, (1, t_new, 1), 1)
    res = [_dot(d.reshape(nb * t_new, POOL_CH).astype(BF16), w)
           for d, w in _pool_windows(ext_ref, t_new, pos, w_ref, (slice(None),))]
    o_ref[...] = jnp.concatenate(res, axis=1)
    ns_ref[...] = ext_ref[:, HALO + t_new - POOL_HIST:HALO + t_new, :]


def _pool_sample(pool_in, state, w_pool_b, past, nb):
    n_seq, t_new, _ = pool_in.shape
    assert t_new % 8 == 0 and n_seq % nb == 0 and past >= POOL_HIST
    return pl.pallas_call(
        functools.partial(_pool_sample_kernel, t_new=t_new, past=past),
        grid=(n_seq // nb,),
        in_specs=[pl.BlockSpec((nb, t_new, POOL_W), lambda i: (i, 0, 0)),
                  pl.BlockSpec((nb, POOL_HIST, POOL_W), lambda i: (i, 0, 0)),
                  _const_spec((N_POOL_GROUPS, POOL_CH, POOL_CH))],
        out_specs=[pl.BlockSpec((nb * t_new, POOL_W), lambda i: (i, 0)),
                   pl.BlockSpec((nb, POOL_HIST, POOL_W), lambda i: (i, 0, 0))],
        out_shape=[jax.ShapeDtypeStruct((n_seq * t_new, POOL_W), F32),
                   jax.ShapeDtypeStruct((n_seq, POOL_HIST, POOL_W), F32)],
        scratch_shapes=[pltpu.VMEM((nb, HALO + t_new, POOL_W), F32)],
        compiler_params=_params("parallel"),
        name="pool_sample",
    )(pool_in, state, w_pool_b)


def _mix_kernel(x_ref, a_ref, m_ref, ga_ref, gp_ref, wo_ref, g2_ref, wq_ref, x1_ref, qm_ref):
    mix = jnp.concatenate([_rms(a_ref[...], ga_ref[...]), _rms(m_ref[...], gp_ref[...])], axis=1)
    x1 = x_ref[...] + _dot(mix.astype(BF16), wo_ref[...])
    x1_ref[...] = x1
    qm_ref[...] = _dot(_rms(x1, g2_ref[...]).astype(BF16), wq_ref[...]).astype(BF16)


def _mix(x, a, m, ga, gp, w_out_b, g2, w_mq_b, tm):
    n = x.shape[0]
    row = lambda w: pl.BlockSpec((tm, w), lambda i: (i, 0))
    return pl.pallas_call(
        _mix_kernel,
        grid=(n // tm,),
        in_specs=[row(D_MODEL), row(ATTN_W), row(POOL_W), _const_spec((1, ATTN_W)), _const_spec((1, POOL_W)),
                  _const_spec((D_MODEL, D_MODEL)), _const_spec((1, D_MODEL)), _const_spec((D_MODEL, MEM_W))],
        out_specs=[row(D_MODEL), row(MEM_W)],
        out_shape=[jax.ShapeDtypeStruct((n, D_MODEL), F32), jax.ShapeDtypeStruct((n, MEM_W), BF16)],
        compiler_params=_params("parallel"),
        name="mix_out",
    )(x, a, m, ga, gp, w_out_b, g2, w_mq_b)


def _cross_kernel(q_ref, kv_ref, o_ref, *, n_mem_blk, rows):
    pairs = [(i, h) for i in range(n_mem_blk) for h in range(MEM_HEADS)]
    head = lambda off, i, h: kv_ref[i, pl.ds(off + h, N_MEM, stride=2 * MEM_HEADS), :].astype(BF16)
    scores = [_dot_nt(q_ref[i * rows:(i + 1) * rows, h * MEM_HEAD_DIM:(h + 1) * MEM_HEAD_DIM], head(0, i, h))
              * (MEM_HEAD_DIM ** -0.5) for i, h in pairs]
    es = [jnp.exp(s - jnp.max(s, axis=-1, keepdims=True)) for s in scores]
    outs = [_dot(e.astype(BF16), head(MEM_HEADS, i, h)) / jnp.sum(e, axis=-1, keepdims=True)
            for e, (i, h) in zip(es, pairs)]
    for i in range(n_mem_blk):
        o_ref[i * rows:(i + 1) * rows, :] = jnp.concatenate(
            outs[i * MEM_HEADS:(i + 1) * MEM_HEADS], axis=1).astype(BF16)


def _cross(qm, mem_rows, rows, n_mem_blk):
    n = qm.shape[0]
    tm = rows * n_mem_blk
    return pl.pallas_call(
        functools.partial(_cross_kernel, n_mem_blk=n_mem_blk, rows=rows),
        grid=(n // tm,),
        in_specs=[pl.BlockSpec((tm, MEM_W), lambda i: (i, 0)),
                  pl.BlockSpec((n_mem_blk,) + mem_rows.shape[1:], lambda i: (i, 0, 0))],
        out_specs=pl.BlockSpec((tm, MEM_W), lambda i: (i, 0)),
        out_shape=jax.ShapeDtypeStruct((n, MEM_W), BF16),
        compiler_params=_params("parallel"),
        name="cross_attn",
    )(qm, mem_rows)


def _cross_shared_kernel(q_ref, k_ref, vt_ref, o_ref):
    lanes = [slice(h * MEM_HEAD_DIM, (h + 1) * MEM_HEAD_DIM) for h in range(MEM_HEADS)]
    scores = [_dot_nt(k_ref[:, l], q_ref[:, l]) * (MEM_HEAD_DIM ** -0.5) for l in lanes]
    es = [jnp.exp(s - jnp.max(s, axis=0, keepdims=True)).astype(BF16) for s in scores]
    accs = [_dot(vt_ref[0, h], e) for h, e in enumerate(es)]
    outs = [(a[0:MEM_HEAD_DIM, :] / a[MEM_HEAD_DIM:MEM_HEAD_DIM + 1, :]).T for a in accs]
    o_ref[...] = jnp.concatenate(outs, axis=1).astype(BF16)


def _cross_shared(qm, mem_k, mem_vt, tm):
    n = qm.shape[0]
    tiles_per_mem = n // mem_vt.shape[0] // tm
    return pl.pallas_call(
        _cross_shared_kernel,
        grid=(n // tm,),
        in_specs=[pl.BlockSpec((tm, MEM_W), lambda i: (i, 0)),
                  pl.BlockSpec((N_MEM, MEM_W), lambda i: (i // tiles_per_mem, 0)),
                  pl.BlockSpec((1, MEM_HEADS, VT_ROWS, N_MEM), lambda i: (i // tiles_per_mem, 0, 0, 0))],
        out_specs=pl.BlockSpec((tm, MEM_W), lambda i: (i, 0)),
        out_shape=jax.ShapeDtypeStruct((n, MEM_W), BF16),
        compiler_params=_params("parallel"),
        name="cross_attn_shared",
    )(qm, mem_k, mem_vt)


def _ffn_kernel(x1_ref, o_ref, wmo_ref, g3_ref, wup_ref, wdn_ref, gf_ref, y_ref):
    x2 = x1_ref[...] + _dot(o_ref[...], wmo_ref[...])
    xn = _rms(x2, g3_ref[...]).astype(BF16)
    acc = x2
    for c in range(D_FF // FF_CHUNK):
        h = jnp.maximum(_dot(xn, wup_ref[:, c * FF_CHUNK:(c + 1) * FF_CHUNK]), 0.0)
        acc = acc + _dot((h * h).astype(BF16), wdn_ref[c * FF_CHUNK:(c + 1) * FF_CHUNK, :])
    y_ref[...] = _rms(acc, gf_ref[...])


def _ffn(x1, o, w_mo_b, g3, w_up_b, w_down_b, g_final, tm):
    n = x1.shape[0]
    row = lambda w: pl.BlockSpec((tm, w), lambda i: (i, 0))
    return pl.pallas_call(
        _ffn_kernel,
        grid=(n // tm,),
        in_specs=[row(D_MODEL), row(MEM_W), _const_spec((MEM_W, D_MODEL)), _const_spec((1, D_MODEL)),
                  _const_spec((D_MODEL, D_FF)), _const_spec((D_FF, D_MODEL)), _const_spec((1, D_MODEL))],
        out_specs=row(D_MODEL),
        out_shape=jax.ShapeDtypeStruct((n, D_MODEL), F32),
        compiler_params=_params("parallel"),
        name="ffn",
    )(x1, o, w_mo_b, g3, w_up_b, w_down_b, g_final)


def _tile(n, pref):
    t = min(n, pref)
    assert n % t == 0
    return t


def kernel(x_prompt, x_sample, cache_cmp_kv, cache_sel_kv, cache_win_kv, state_pool, cache_mem_kv, page_table, mem_prompt, w_in, pe_cmp, w_cmp1, w_cmp2, w_pool, g_attn_out, pool_scale, w_out, g_norm1, g_norm2, g_mem, w_mq, w_mkv, w_mo, g_norm3, w_up, w_down, g_final):
    assert w_in.shape[0] == 1, "single-layer trunk"
    B, S, _ = x_prompt.shape
    DB, T, _ = x_sample.shape
    past = page_table.shape[1] * PAGE_SIZE
    kv5 = (2, N_KV_HEADS, HEAD_DIM)
    r2 = lambda v: v.reshape(1, -1)

    w_in_p = _permute_w_in(w_in[0])
    w_out_b, w_mq_b, w_mkv_b, w_mo_b = (w[0].astype(BF16) for w in (w_out, w_mq, w_mkv, w_mo))
    w_up_b, w_down_b, w_pool_b = w_up[0].astype(BF16), w_down[0].astype(BF16), w_pool[0].astype(BF16)
    g1, g2, g3, gm, ga, gp, gf = (r2(v) for v in (g_norm1[0], g_norm2[0], g_norm3[0], g_mem[0], g_attn_out[0],
                                                  pool_scale[0], g_final))
    cw = _compress_weights(pe_cmp[0], w_cmp1[0], w_cmp2[0])

    np_tok = B * S
    tm_p = _tile(S, 512)
    xp = x_prompt.reshape(np_tok, D_MODEL)
    q, qr, cmp_t, sel_t, win_t, sel_b, win_b, gates, pool_in = _in_proj(
        xp, g1, w_in_p, jnp.arange(S), tm_p, n_seq=B)
    mem_f, mem_k, mem_vt = _mem_kv(mem_prompt.reshape(B * N_MEM, D_MODEL), gm, w_mkv_b)
    ckv = _compress_seq(cmp_t, cw, 64)
    a = _nsa_prompt(q, qr, gates, ckv, sel_b, win_b, B)
    m = _pool_prompt(pool_in, w_pool_b, B, tm_p)
    x1, qm = _mix(xp, a, m, ga, gp, w_out_b, g2, w_mq_b, tm_p)
    o = _cross_shared(qm, mem_k, mem_vt, tm_p)
    y_p = _ffn(x1, o, w_mo_b, g3, w_up_b, w_down_b, gf, tm_p)
    rows_first = lambda t: t.reshape(t.shape[:1] + kv5 + t.shape[-1:]).transpose(0, 4, 1, 2, 3)[None]
    p_state = (rows_first(cmp_t), rows_first(sel_t), rows_first(win_t[:, :, -min(WINDOW, S):]),
               pool_in.reshape(B, S, POOL_W)[None, :, -POOL_HIST:],
               mem_f.reshape(1, B, N_MEM, 2, MEM_HEADS, MEM_HEAD_DIM))

    ns_tok = DB * T
    tm_s = _tile(ns_tok, 512)
    xs = x_sample.reshape(ns_tok, D_MODEL)
    q, qr, cmp_kv, sel_kv, win_kv, _, _, gates, pool_in = _in_proj(
        xs, g1, w_in_p, jnp.tile(past + jnp.arange(T), DB), tm_s)
    r3 = lambda v: v.reshape(DB, T, v.shape[-1])
    ckv = _compress_paged(_pages_t(cache_cmp_kv[0]), page_table, cw, 64)
    a, s_win = _nsa_sample(r3(q), r3(qr), r3(gates), ckv, _pages_t(cache_sel_kv[0]), page_table,
                           r3(sel_kv), _pages_t(cache_win_kv[0]), r3(win_kv), 64, 2)
    m, s_pool = _pool_sample(r3(pool_in), state_pool[0], w_pool_b, past, _tile(DB, 16))
    x1, qm = _mix(xs, a.reshape(ns_tok, ATTN_W), m, ga, gp, w_out_b, g2, w_mq_b, tm_s)
    o = _cross(qm, cache_mem_kv[0].reshape(DB, N_MEM * 2 * MEM_HEADS, MEM_HEAD_DIM), T, _tile(DB, 4))
    y_s = _ffn(x1, o, w_mo_b, g3, w_up_b, w_down_b, gf, tm_s)
    s_state = (cmp_kv.reshape((1, DB, T) + kv5), sel_kv.reshape((1, DB, T) + kv5),
               rows_first(s_win), s_pool[None])

    return (y_p.reshape(B, S, D_MODEL), y_s.reshape(DB, T, D_MODEL)) + p_state + s_state
```

```python
import functools

import jax
import jax.numpy as jnp
from jax import lax
from jax.experimental import pallas as pl
from jax.experimental.pallas import tpu as pltpu

D_MODEL = 1024
PAGE_SIZE = 128
ATTN_W = 512
POOL_W = 512
HEAD_DIM = 64
N_HEADS = 8
N_KV_HEADS = 2
HEADS_PER_KV = N_HEADS // N_KV_HEADS
SCALE = HEAD_DIM ** -0.5
LOG2E = 1.4426950408889634
Q_SCALE = SCALE * LOG2E
ROT_DIM = HEAD_DIM // 4
ROPE_THETA = 500000.0
CMP_LEN = 32
CMP_STRIDE = 16
CMP_HID = HEAD_DIM
SEL_BLOCK = 64
SEL_TOPN = 16
WINDOW = 512
FORCE_BONUS = 1000.0
POOL_WINDOWS = (2, 4, 8, 16)
N_POOL_GROUPS = len(POOL_WINDOWS)
POOL_CH = POOL_W // N_POOL_GROUPS
POOL_HIST = max(POOL_WINDOWS) - 1
N_MEM = 256
MEM_HEADS = 4
MEM_HEAD_DIM = 128
MEM_W = MEM_HEADS * MEM_HEAD_DIM
D_FF = 4 * D_MODEL
EPS = 1e-6
KV_W = 2 * N_KV_HEADS * HEAD_DIM
N_GATES = 3 * N_HEADS
LANES = 128
VMEM_LIMIT = 56 * 1024 * 1024
QW = N_HEADS * LANES
TQ = 128
NEG = -1e30
CHUNK_W = CMP_STRIDE * KV_W
CMP_R = CMP_LEN // CMP_STRIDE
CHUNKS_PER_PAGE = PAGE_SIZE // CMP_STRIDE
HALO = 16
FF_CHUNK = 1024

F32 = jnp.float32
BF16 = jnp.bfloat16


def _rms(x, g):
    return x * lax.rsqrt(jnp.mean(x * x, axis=-1, keepdims=True) + EPS) * g


def _dot(a, b):
    return jnp.dot(a, b, preferred_element_type=F32)


def _dot_nt(a, b):
    return lax.dot_general(a, b, (((1,), (1,)), ((), ())), preferred_element_type=F32)


ONES_ROWS = 16
VT_ROWS = LANES + ONES_ROWS


def _ones_rows(n):
    return jnp.where(lax.broadcasted_iota(jnp.int32, (ONES_ROWS, n), 0) == 0, 1.0, 0.0).astype(BF16)


def _const_spec(shape):
    n = len(shape)
    return pl.BlockSpec(shape, lambda *_: (0,) * n, pipeline_mode=pl.Buffered(1))


def _params(*sem):
    return pltpu.CompilerParams(dimension_semantics=sem, vmem_limit_bytes=VMEM_LIMIT)


X_Q, X_CMP, X_SEL, X_WIN, X_POOL, X_GATE = 0, 1024, 1280, 1536, 1792, 2304
X_IN = X_GATE + LANES


def _permute_w_in(w_in):
    q, kc, ks, kw, gl, pool = jnp.split(w_in, (512, 768, 1024, 1280, 1280 + N_GATES), axis=1)
    z = jnp.zeros((D_MODEL, HEAD_DIM), w_in.dtype)
    qx = []
    for h in range(N_HEADS):
        qh = q[:, h * HEAD_DIM:(h + 1) * HEAD_DIM]
        qx += [qh, z] if h // HEADS_PER_KV == 0 else [z, qh]
    pad = jnp.zeros((D_MODEL, LANES - N_GATES), w_in.dtype)
    return jnp.concatenate(qx + [kc, ks, kw, pool, gl, pad], axis=1).astype(BF16)


def _rope_tables(pos):
    half = ROT_DIM // 2
    inv = ROPE_THETA ** (-jnp.arange(half, dtype=F32) / half)
    ang = pos.astype(F32)[:, None] * inv[None, :]
    cos, sin = jnp.cos(ang), jnp.sin(ang)
    z = jnp.zeros_like(cos)
    rest0 = jnp.zeros((pos.shape[0], HEAD_DIM - ROT_DIM), F32)
    cs = jnp.concatenate([cos, cos, rest0 + 1.0], axis=1)
    sa = jnp.concatenate([-sin, z, rest0], axis=1)
    sb = jnp.concatenate([z, sin, rest0], axis=1)
    return tuple(jnp.tile(t, (1, LANES // HEAD_DIM)) for t in (cs, sa, sb))


def _rope(v, cs, sa, sb):
    n = v.shape[1] // LANES
    w = v.shape[1]
    cs, sa, sb = (jnp.tile(t, (1, n)) for t in (cs, sa, sb))
    return v * cs + pltpu.roll(v, w - ROT_DIM // 2, 1) * sa + pltpu.roll(v, ROT_DIM // 2, 1) * sb


def _in_proj_kernel(x_ref, g_ref, w_ref, cs_ref, sa_ref, sb_ref,
                    q_ref, qr_ref, cmp_ref, sel_ref, win_ref, selb_ref, winb_ref, gate_ref, pool_ref, *, feature_major):
    xb = _rms(x_ref[...], g_ref[...]).astype(BF16)
    cs, sa, sb = cs_ref[...], sa_ref[...], sb_ref[...]

    def put_state(ref, v):
        if feature_major:
            ref[0] = jnp.concatenate([v[:, c * LANES:(c + 1) * LANES].T for c in range(KV_W // LANES)], axis=0)
        else:
            ref[...] = v

    q = _dot(xb, w_ref[:, X_Q:X_CMP])
    q_ref[...] = (q * Q_SCALE).astype(BF16)
    qr_ref[...] = (_rope(q, cs, sa, sb) * Q_SCALE).astype(BF16)
    put_state(cmp_ref, _dot(xb, w_ref[:, X_CMP:X_SEL]))
    for lo, f_ref, b_ref in ((X_SEL, sel_ref, selb_ref), (X_WIN, win_ref, winb_ref)):
        kv = _dot(xb, w_ref[:, lo:lo + KV_W])
        kv = jnp.concatenate([_rope(kv[:, :LANES], cs, sa, sb), kv[:, LANES:]], axis=1)
        put_state(f_ref, kv)
        b_ref[...] = kv.astype(BF16)
    pool_ref[...] = _dot(xb, w_ref[:, X_POOL:X_GATE])
    gate_ref[...] = jax.nn.sigmoid(_dot(xb, w_ref[:, X_GATE:X_IN]))


def _in_proj(x, g1, w_in_p, pos_rows, tm, n_seq=None):
    n = x.shape[0]
    cs, sa, sb = _rope_tables(pos_rows)
    row = lambda w: pl.BlockSpec((tm, w), lambda i: (i, 0))
    tab = pl.BlockSpec((tm, LANES), lambda i: (i % (pos_rows.shape[0] // tm), 0))
    sds = lambda w, dt: jax.ShapeDtypeStruct((n, w), dt)
    if n_seq is None:
        st_spec, st_shape = row(KV_W), sds(KV_W, F32)
    else:
        nt = n // n_seq // tm
        st_spec = pl.BlockSpec((1, KV_W, tm), lambda i: (i // nt, 0, i % nt))
        st_shape = jax.ShapeDtypeStruct((n_seq, KV_W, n // n_seq), F32)
    return pl.pallas_call(
        functools.partial(_in_proj_kernel, feature_major=n_seq is not None),
        grid=(n // tm,),
        in_specs=[row(D_MODEL), _const_spec((1, D_MODEL)), _const_spec((D_MODEL, X_IN)),
                  tab, tab, tab],
        out_specs=[row(QW), row(QW), st_spec, st_spec, st_spec, row(KV_W), row(KV_W), row(LANES), row(POOL_W)],
        out_shape=[sds(QW, BF16), sds(QW, BF16), st_shape, st_shape, st_shape,
                   sds(KV_W, BF16), sds(KV_W, BF16), sds(LANES, F32), sds(POOL_W, F32)],
        compiler_params=_params("parallel"),
        name="in_proj",
    )(x, g1, w_in_p, cs, sa, sb)


def _mem_kv_kernel(m_ref, g_ref, w_ref, o_ref, kb_ref, vt_ref):
    kv = _dot(_rms(m_ref[...], g_ref[...]).astype(BF16), w_ref[...])
    o_ref[...] = kv
    kb_ref[...] = kv[:, :MEM_W].astype(BF16)
    for h in range(MEM_HEADS):
        v = kv[:, MEM_W + h * MEM_HEAD_DIM:MEM_W + (h + 1) * MEM_HEAD_DIM]
        vt_ref[0, h] = jnp.concatenate([v.T.astype(BF16), _ones_rows(N_MEM)], axis=0)


def _mem_kv(mem, g_mem, w_mkv_b):
    n = mem.shape[0]
    row = lambda w: pl.BlockSpec((N_MEM, w), lambda i: (i, 0))
    return pl.pallas_call(
        _mem_kv_kernel,
        grid=(n // N_MEM,),
        in_specs=[row(D_MODEL), _const_spec((1, D_MODEL)), _const_spec((D_MODEL, 2 * MEM_W))],
        out_specs=[row(2 * MEM_W), row(MEM_W),
                   pl.BlockSpec((1, MEM_HEADS, VT_ROWS, N_MEM), lambda i: (i, 0, 0, 0))],
        out_shape=[jax.ShapeDtypeStruct((n, 2 * MEM_W), F32), jax.ShapeDtypeStruct((n, MEM_W), BF16),
                   jax.ShapeDtypeStruct((n // N_MEM, MEM_HEADS, VT_ROWS, N_MEM), BF16)],
        compiler_params=_params("parallel"),
        name="mem_kv",
    )(mem, g_mem, w_mkv_b)


def _compress_weights(pe, w1, w2):
    w1r = w1.reshape(2, CMP_R, CMP_STRIDE, HEAD_DIM, CMP_HID)
    eye_g = jnp.eye(N_KV_HEADS, dtype=w1.dtype)
    w1c = jnp.einsum('crsdh,gy->csgdryh', w1r, eye_g).reshape(2, CHUNK_W // 2, CMP_R * LANES)
    pe_term = jnp.einsum('cld,cldh->ch', pe, w1.reshape(2, CMP_LEN, HEAD_DIM, CMP_HID))
    bias = jnp.broadcast_to(pe_term[:, None, :], (2, N_KV_HEADS, CMP_HID)).reshape(2, 1, LANES)
    w2c = jnp.einsum('chd,gy->cghyd', w2, eye_g).reshape(2, LANES, LANES)
    return w1c.astype(BF16), bias, w2c.astype(BF16)


def _compress_kernel(*refs, n_in, ppr, has_pt):
    if has_pt:
        refs = refs[1:]
    x_refs = refs[:n_in]
    w1_ref, b_ref, w2_ref, o_ref, carry_ref = refs[n_in:]

    @pl.when(pl.program_id(1) == 0)
    def _():
        carry_ref[...] = jnp.zeros_like(carry_ref)

    r_i = lax.broadcasted_iota(jnp.int32, (PAGE_SIZE, PAGE_SIZE), 0)
    t_i = lax.broadcasted_iota(jnp.int32, (PAGE_SIZE, PAGE_SIZE), 1)
    src = CMP_STRIDE * (r_i & (CHUNKS_PER_PAGE - 1)) + lax.shift_right_logical(r_i, 3)
    pick = jnp.where(t_i == src, 1.0, 0.0).astype(BF16)
    pages = [_dot_nt(pick, r[0, :, j * PAGE_SIZE:(j + 1) * PAGE_SIZE].astype(BF16))
             for r in x_refs for j in range(ppr)]
    rows = len(pages) * CHUNKS_PER_PAGE
    row = lax.broadcasted_iota(jnp.int32, (rows, LANES), 0)
    outs = []
    for c in range(KV_W // LANES):
        lanes = slice(c * LANES, (c + 1) * LANES)
        x = jnp.concatenate(
            [jnp.concatenate([pg[s * CHUNKS_PER_PAGE:(s + 1) * CHUNKS_PER_PAGE, lanes] for pg in pages], axis=0)
             for s in range(CMP_STRIDE)], axis=1)
        u = _dot(x.astype(BF16), w1_ref[c])
        u0, u1 = u[:, :LANES], u[:, LANES:]
        prev = jnp.where(row == 0, carry_ref[0:1, lanes], pltpu.roll(u0, 1, 0))
        carry_ref[:, lanes] = jnp.broadcast_to(u0[rows - 1:rows, :], (carry_ref.shape[0], LANES))
        h = prev + u1 + b_ref[c]
        outs.append(_dot(jax.nn.gelu(h).astype(BF16), w2_ref[c]))
    o_ref[0] = jnp.concatenate(outs, axis=1).astype(BF16)


def _compress_call(n_seq, n_pages, n_in, ppr, has_pt, x_spec_fn):
    P = n_in * ppr
    assert n_pages % P == 0
    grid = (n_seq, n_pages // P)
    in_specs = [x_spec_fn(k) for k in range(n_in)] + [
        _const_spec((2, CHUNK_W // 2, CMP_R * LANES)), _const_spec((2, 1, LANES)), _const_spec((2, LANES, LANES))]
    out_spec = pl.BlockSpec((1, P * CHUNKS_PER_PAGE, KV_W), (lambda b, i, *_: (b, i, 0)))
    kern = functools.partial(_compress_kernel, n_in=n_in, ppr=ppr, has_pt=has_pt)
    common = dict(out_shape=jax.ShapeDtypeStruct((n_seq, n_pages * CHUNKS_PER_PAGE, KV_W), BF16),
                  compiler_params=_params("parallel", "arbitrary"), name="compress")
    scratch = [pltpu.VMEM((8, KV_W), F32)]
    if has_pt:
        return pl.pallas_call(kern, grid_spec=pltpu.PrefetchScalarGridSpec(
            num_scalar_prefetch=1, grid=grid, in_specs=in_specs, out_specs=out_spec, scratch_shapes=scratch), **common)
    return pl.pallas_call(kern, grid=grid, in_specs=in_specs, out_specs=out_spec, scratch_shapes=scratch, **common)


def _compress_seq(cmp_t, cw, pages_per_step):
    n_seq, _, seq = cmp_t.shape
    n_pages = seq // PAGE_SIZE
    P = min(pages_per_step, n_pages)
    spec = lambda k: pl.BlockSpec((1, KV_W, P * PAGE_SIZE), lambda b, i: (b, 0, i))
    return _compress_call(n_seq, n_pages, 1, P, False, spec)(cmp_t, *cw)


def _pages_t(cache):
    return cache.transpose(0, 2, 3, 4, 1).reshape(cache.shape[0], KV_W, cache.shape[1])


def _compress_paged(cache_t, page_table, cw, pages_per_step):
    n_seq, n_pages = page_table.shape
    P = min(pages_per_step, n_pages)
    spec = lambda k: pl.BlockSpec((1, KV_W, PAGE_SIZE), lambda b, i, pt: (pt[b, i * P + k], 0, 0))
    return _compress_call(n_seq, n_pages, P, 1, True, spec)(page_table, *([cache_t] * P), *cw)


def _overlap_t(nch, n_sel_pad):
    m = jnp.arange(nch)[None, :]
    j = jnp.arange(n_sel_pad)[:, None]
    lo = (m - 1) * CMP_STRIDE
    ov = jnp.maximum(jnp.minimum(lo + CMP_LEN, (j + 1) * SEL_BLOCK) - jnp.maximum(lo, j * SEL_BLOCK), 0)
    return jnp.where(m >= 1, ov.astype(F32) / CMP_LEN, 0.0).astype(BF16)


def _heads_to_rows(x):
    return jnp.concatenate([x[:, h * LANES:(h + 1) * LANES] for h in range(N_HEADS)], axis=0)


def _topk_bias(imp, pos, n_blk, k_top, axis):
    j = lax.broadcasted_iota(jnp.int32, imp.shape, axis)
    cur = lax.shift_right_logical(pos, 6)
    forced = (j == 0) | (j == cur) | (j == cur - 1)
    score = jnp.where(j * SEL_BLOCK <= pos, imp + jnp.where(forced, FORCE_BONUS, 0.0), -1e9)
    score = jnp.where(j < n_blk, score, -jnp.inf)
    bias = jnp.full(imp.shape, NEG, F32)
    for _ in range(k_top):
        mx = jnp.max(score, axis=axis, keepdims=True)
        idx = jnp.min(jnp.where(score == mx, j, LANES), axis=axis, keepdims=True)
        hit = j == idx
        bias = jnp.where(hit, 0.0, bias)
        score = jnp.where(hit, -jnp.inf, score)
    return bias


def _split_dot(dot, ps):
    hi = ps.astype(BF16)
    lo = (ps - hi.astype(F32)).astype(BF16)
    return dot(hi) + dot(lo)


def _nsa_prompt_kernel(q_ref, qr_ref, g_ref, ckv_ref, ksel_ref, kwin_ref, ovt_ref, a_ref,
                       kx_ref, vts_ref, vtw_ref, lc_ref, *, tk, n_sel, k_top):
    i = pl.program_id(1)
    q0 = pl.multiple_of(i * TQ, TQ)
    nc = ckv_ref.shape[1]

    @pl.when(i == 0)
    def _():
        kx_ref[...] = jnp.zeros_like(kx_ref)
        vts_ref[...] = jnp.zeros_like(vts_ref)
        vtw_ref[...] = jnp.zeros_like(vtw_ref)
        lc_ref[0:LANES, :] = ckv_ref[0, :, LANES:2 * LANES].astype(F32).T.astype(BF16)
        lc_ref[LANES:2 * LANES, :] = ovt_ref[...]
        lc_ref[2 * LANES:, :] = _ones_rows(nc)

    key_t = q0 + lax.broadcasted_iota(jnp.int32, (TQ, 1), 0)
    onehot = jnp.where(lax.shift_right_logical(key_t, 6) == lax.broadcasted_iota(jnp.int32, (TQ, LANES), 1), 1.0, 0.0)
    kx_ref[pl.ds(q0, TQ), :] = jnp.concatenate([ksel_ref[pl.ds(q0, TQ), 0:LANES], onehot.astype(BF16)], axis=1)
    for k_ref, vt_ref in ((ksel_ref, vts_ref), (kwin_ref, vtw_ref)):
        v = k_ref[pl.ds(q0, TQ), LANES:2 * LANES].astype(F32)
        vt_ref[:, pl.ds(q0, TQ)] = jnp.concatenate([v.T.astype(BF16), _ones_rows(TQ)], axis=0)

    nl = N_HEADS * TQ
    lane = lax.broadcasted_iota(jnp.int32, (1, nl), 1)
    pos = q0 + (lane & (TQ - 1))
    qc = _heads_to_rows(q_ref)
    qr = _heads_to_rows(qr_ref)

    m_idx = lax.broadcasted_iota(jnp.int32, (nc, 1), 0)
    c_end = jnp.where(m_idx >= 1, m_idx * CMP_STRIDE + (CMP_STRIDE - 1), 2 ** 30)
    s = jnp.where(c_end <= pos, _dot_nt(ckv_ref[0, :, 0:LANES], qc), NEG)
    acc = _dot(lc_ref[...], jnp.exp2(s - jnp.max(s, axis=0, keepdims=True)).astype(BF16))
    rden = jnp.where(pos >= CMP_LEN - 1, 1.0 / jnp.maximum(acc[2 * LANES:2 * LANES + 1, :], 1e-30), 0.0)
    oc_t = acc[0:LANES, :] * rden
    imp_h = acc[LANES:2 * LANES, :] * rden

    wk = WINDOW + TQ
    w0 = pl.multiple_of(jnp.maximum(q0 - WINDOW, 0), TQ)
    sw = _dot_nt(kwin_ref[pl.ds(w0, wk), 0:LANES], qr)
    kp = w0 + lax.broadcasted_iota(jnp.int32, (wk, 1), 0)
    sw = jnp.concatenate([jnp.where((kp[:TQ] <= pos) & (kp[:TQ] >= pos - WINDOW), sw[:TQ], NEG),
                         jnp.where(kp[TQ:] <= pos, sw[TQ:], NEG)], axis=0)
    accw = _dot(vtw_ref[:, pl.ds(w0, wk)], jnp.exp2(sw - jnp.max(sw, axis=0, keepdims=True)).astype(BF16))
    ow_t = accw[0:LANES, :] / jnp.maximum(accw[LANES:LANES + 1, :], 1e-30)

    imp_t = jnp.concatenate(
        [sum(imp_h[:, (g * HEADS_PER_KV + hh) * TQ:(g * HEADS_PER_KV + hh + 1) * TQ] for hh in range(HEADS_PER_KV))
         for g in range(N_KV_HEADS)], axis=1)
    bias_t = _topk_bias(imp_t, pos[:, :N_KV_HEADS * TQ], n_sel, k_top, 0)
    biases = [bias_t[:, g * TQ:(g + 1) * TQ].T.astype(BF16) for g in range(N_KV_HEADS)]
    bias_rows = jnp.concatenate([biases[h // HEADS_PER_KV] for h in range(N_HEADS)], axis=0)
    qs = jnp.concatenate([qr, bias_rows], axis=1)

    def sel_chunk(c0, n, masked, carry):
        m, acc = carry
        s = _dot_nt(kx_ref[pl.ds(c0, n), :], qs)
        if masked:
            s = jnp.where(c0 + lax.broadcasted_iota(jnp.int32, (n, 1), 0) <= pos, s, NEG)
        m_new = jnp.maximum(m, jnp.max(s, axis=0, keepdims=True))
        p = jnp.exp2(s - m_new).astype(BF16)
        acc = jnp.exp2(m - m_new) * acc + _dot(vts_ref[:, pl.ds(c0, n)], p)
        return m_new, acc

    tks = tk // 2
    n_big = q0 // tk
    n_small = q0 // tks
    carry = (jnp.full((1, nl), NEG, F32), jnp.zeros((VT_ROWS, nl), F32))
    big = lambda c, cr: sel_chunk(pl.multiple_of(c * tk, tk), tk, False, cr)
    small = lambda c, masked, cr: sel_chunk(pl.multiple_of(c * tks, tks), tks, masked, cr)
    carry = lax.fori_loop(0, n_big // 2, lambda c, cr: big(2 * c + 1, big(2 * c, cr)), carry)
    carry = lax.fori_loop(n_big // 2 * 2, n_big, big, carry)
    carry = lax.fori_loop(n_big * 2, n_small, lambda c, cr: small(c, False, cr), carry)
    _, acc = small(n_small, True, carry)
    os_t = acc[0:LANES, :] / jnp.maximum(acc[LANES:LANES + 1, :], 1e-30)

    g_t = g_ref[...].T
    outs = []
    for h in range(N_HEADS):
        g = h // HEADS_PER_KV
        blk_h = lambda o: o[g * HEAD_DIM:(g + 1) * HEAD_DIM, h * TQ:(h + 1) * TQ]
        gate = lambda j: g_t[3 * h + j:3 * h + j + 1, :]
        outs.append(gate(0) * blk_h(oc_t) + gate(1) * blk_h(os_t) + gate(2) * blk_h(ow_t))
    a_ref[...] = jnp.concatenate(outs, axis=0).T


def _nsa_prompt(q, qr, gates, ckv, ksel, kwin, n_seq):
    n = q.shape[0]
    seq = n // n_seq
    nch = ckv.shape[1]
    n_sel = seq // SEL_BLOCK
    assert seq % TQ == 0 and seq >= WINDOW + TQ and n_sel <= LANES and seq % SEL_BLOCK == 0
    tk = min(1024, seq)
    assert seq % tk == 0
    nt = seq // TQ
    tile = lambda w: pl.BlockSpec((TQ, w), lambda b, i: (b * nt + i, 0))
    whole = lambda: pl.BlockSpec((seq, KV_W), lambda b, i: (b, 0), pipeline_mode=pl.Buffered(1))
    kern = functools.partial(_nsa_prompt_kernel, tk=tk, n_sel=n_sel, k_top=min(SEL_TOPN, n_sel))
    return pl.pallas_call(
        kern,
        grid=(n_seq, nt),
        in_specs=[tile(QW), tile(QW), tile(LANES),
                  pl.BlockSpec((1, nch, KV_W), lambda b, i: (b, 0, 0)),
                  whole(), whole(), _const_spec((LANES, nch))],
        out_specs=tile(ATTN_W),
        out_shape=jax.ShapeDtypeStruct((n, ATTN_W), F32),
        scratch_shapes=[pltpu.VMEM((seq, 2 * LANES), BF16), pltpu.VMEM((VT_ROWS, seq), BF16),
                        pltpu.VMEM((VT_ROWS, seq), BF16), pltpu.VMEM((2 * LANES + ONES_ROWS, nch), BF16)],
        compiler_params=_params("arbitrary", "arbitrary"),
        name="nsa_prompt",
    )(q, qr, gates, ckv, ksel, kwin, _overlap_t(nch, LANES))


def _softmax_rows2(s_a, mask_a, s_b, mask_b):
    s_a = jnp.where(mask_a, s_a, NEG)
    s_b = jnp.where(mask_b, s_b, NEG)
    mx = jnp.maximum(jnp.max(s_a, axis=1, keepdims=True), jnp.max(s_b, axis=1, keepdims=True))
    e_a = jnp.where(mask_a, jnp.exp2(s_a - mx), 0.0)
    e_b = jnp.where(mask_b, jnp.exp2(s_b - mx), 0.0)
    den = jnp.sum(e_a, axis=1, keepdims=True) + jnp.sum(e_b, axis=1, keepdims=True)
    return e_a, e_b, jnp.maximum(den, 1e-30)


def _pad_rows(x, rows):
    return jnp.concatenate([x, jnp.zeros((rows - x.shape[0], x.shape[1]), x.dtype)], axis=0)


def _nsa_sample_kernel(*refs, n_seq, n_pg, t_new, past, n_cb, k_cache, single_step):
    pg_refs = refs[1:1 + n_seq * n_pg]
    (q_ref, qr_ref, g_ref, ckv_ref, seln_ref, cwin_ref, winn_ref, ovl_ref,
     a_ref, swin_ref, qs_ref, oc_ref, ow_ref, m_ref, l_ref, acc_ref) = refs[1 + n_seq * n_pg:]
    j = pl.program_id(1)
    nr = N_HEADS * t_new
    row_t = lax.rem(lax.broadcasted_iota(jnp.int32, (nr, 1), 0), t_new)
    pos = past + row_t
    lane = lax.broadcasted_iota(jnp.int32, (1, LANES), 1)

    seqs = range(n_seq)

    def first():
        qc = [_heads_to_rows(q_ref[b].astype(F32)).astype(BF16) for b in seqs]
        qr = [_heads_to_rows(qr_ref[b].astype(F32)).astype(BF16) for b in seqs]
        nc = ckv_ref.shape[1]
        m_idx = lax.broadcasted_iota(jnp.int32, (1, nc), 1)
        cmask = (m_idx >= 1) & (m_idx * CMP_STRIDE + (CMP_STRIDE - 1) <= pos)
        sc = [jnp.where(cmask, _dot_nt(qc[b], ckv_ref[b, :, 0:LANES]), NEG) for b in seqs]
        ec = [jnp.where(cmask, jnp.exp2(s - jnp.max(s, axis=1, keepdims=True)), 0.0) for s in sc]
        p_c = [e / jnp.maximum(jnp.sum(e, axis=1, keepdims=True), 1e-30) for e in ec]
        for b in seqs:
            oc_ref[b] = _dot(p_c[b].astype(BF16), ckv_ref[b, :, LANES:2 * LANES])
        cw = [jnp.concatenate([cwin_ref[b, c * LANES:(c + 1) * LANES, :].T for c in range(KV_W // LANES)], axis=1)
              for b in seqs]
        wn = [_pad_rows(winn_ref[b], LANES) for b in seqs]
        r_idx = lax.broadcasted_iota(jnp.int32, (1, WINDOW), 1)
        sm = [_softmax_rows2(_dot_nt(qr[b], cw[b][:, 0:LANES].astype(BF16)), r_idx >= row_t,
                             _dot_nt(qr[b], wn[b][:, 0:LANES].astype(BF16)), lane <= row_t) for b in seqs]
        for b in seqs:
            e_a, e_b, den = sm[b]
            ow_ref[b] = (_dot(e_a.astype(BF16), cw[b][:, LANES:].astype(BF16))
                         + _dot(e_b.astype(BF16), wn[b][:, LANES:].astype(BF16))) / den
            shifted = pltpu.roll(cwin_ref[b], WINDOW - t_new, 1)
            fresh = pltpu.roll(wn[b].T, LANES - t_new, 1)
            tail = jnp.where(lane >= LANES - t_new, fresh, shifted[:, WINDOW - LANES:])
            swin_ref[b] = jnp.concatenate([shifted[:, :WINDOW - LANES], tail], axis=1)
        imps = [_split_dot(lambda v: _dot(v, ovl_ref[...]),
                           sum(p_c[b][(g * HEADS_PER_KV + hh) * t_new:(g * HEADS_PER_KV + hh + 1) * t_new, :]
                               for hh in range(HEADS_PER_KV)))
                for b in seqs for g in range(N_KV_HEADS)]
        imp_t = _pad_rows(jnp.concatenate(imps, axis=0), LANES).T
        bias = _topk_bias(imp_t, past + lax.rem(lane, t_new), n_cb, k_cache, 0).T
        for b in seqs:
            rows_b = [bias[(b * N_KV_HEADS + h // HEADS_PER_KV) * t_new:(b * N_KV_HEADS + h // HEADS_PER_KV + 1) * t_new, :]
                      for h in range(N_HEADS)]
            qs_ref[b] = jnp.concatenate([qr[b], jnp.concatenate(rows_b, axis=0).astype(BF16)], axis=1)
            m_ref[b] = jnp.full(m_ref.shape[1:], NEG, F32)
            l_ref[b] = jnp.zeros(l_ref.shape[1:], F32)
            acc_ref[b] = jnp.zeros(acc_ref.shape[1:], F32)

    def online(ss, pvs):
        m_new = [jnp.maximum(m_ref[b, :, 0:1], jnp.max(ss[b], axis=1, keepdims=True)) for b in seqs]
        ps = [jnp.exp2(ss[b] - m_new[b]) for b in seqs]
        pv = [pvs[b](ps[b].astype(BF16)) for b in seqs]
        for b in seqs:
            alpha = jnp.exp2(m_ref[b, :, 0:1] - m_new[b])
            l_ref[b] = jnp.broadcast_to(alpha * l_ref[b, :, 0:1] + jnp.sum(ps[b], axis=1, keepdims=True), l_ref.shape[1:])
            acc_ref[b] = alpha * acc_ref[b] + pv[b]
            m_ref[b] = jnp.broadcast_to(m_new[b], m_ref.shape[1:])

    def pages():
        cat = lambda b, lo: jnp.concatenate(
            [r[0, lo:lo + LANES, :] for r in pg_refs[b * n_pg:(b + 1) * n_pg]], axis=1).astype(BF16)
        k_t = [cat(b, 0) for b in seqs]
        v_t = [cat(b, LANES) for b in seqs]
        nk = k_t[0].shape[1]
        blk = lax.shift_right_logical(j * nk + lax.broadcasted_iota(jnp.int32, (1, nk), 1), 6)
        onehot_t = jnp.where(blk == lax.broadcasted_iota(jnp.int32, (LANES, 1), 0), 1.0, 0.0).astype(BF16)
        online([_dot(qs_ref[b], jnp.concatenate([k_t[b], onehot_t], axis=0)) for b in seqs],
               [functools.partial(_dot_nt, b=v_t[b]) for b in seqs])

    def last():
        sn = [_pad_rows(seln_ref[b], LANES) for b in seqs]
        online([jnp.where(lane <= row_t, _dot_nt(qs_ref[b, :, 0:LANES], sn[b][:, 0:LANES].astype(BF16)), NEG)
                for b in seqs],
               [functools.partial(_dot, b=sn[b][:, LANES:].astype(BF16)) for b in seqs])
        for b in seqs:
            os = acc_ref[b] / jnp.maximum(l_ref[b, :, 0:1], 1e-30)
            gts = g_ref[b]
            outs = []
            for h in range(N_HEADS):
                g = h // HEADS_PER_KV
                blk_h = lambda o: o[h * t_new:(h + 1) * t_new, g * HEAD_DIM:(g + 1) * HEAD_DIM]
                gate = lambda k: gts[:, 3 * h + k:3 * h + k + 1]
                outs.append(gate(0) * blk_h(oc_ref[b]) + gate(1) * blk_h(os) + gate(2) * blk_h(ow_ref[b]))
            a_ref[b] = jnp.concatenate(outs, axis=1)

    if single_step:
        first()
        pages()
        last()
    else:
        pl.when(j == 0)(first)
        pages()
        pl.when(j == pl.num_programs(1) - 1)(last)


def _nsa_sample(q, qr, gates, ckv, cache_sel, page_table, sel_new, cache_win, win_new, pages_per_step, seqs_per_step):
    n_all, t_new = q.shape[:2]
    n_pages = page_table.shape[1]
    past = n_pages * PAGE_SIZE
    n_cb = past // SEL_BLOCK
    assert cache_win.shape[2] == WINDOW and n_cb <= LANES and t_new <= SEL_BLOCK and t_new % 8 == 0
    k_cache = min(SEL_TOPN, n_cb + 1) - 1
    assert k_cache >= 2
    P = min(pages_per_step, n_pages)
    NS = min(seqs_per_step, n_all)
    assert n_pages % P == 0 and n_all % NS == 0
    nch = ckv.shape[1]
    nr = N_HEADS * t_new
    per_seq = lambda r, w: pl.BlockSpec((NS, r, w), lambda b, j, pt: (b, 0, 0))
    page = lambda s, k: pl.BlockSpec((1, KV_W, PAGE_SIZE), lambda b, j, pt: (pt[b * NS + s, j * P + k], 0, 0))
    kern = functools.partial(_nsa_sample_kernel, n_seq=NS, n_pg=P, t_new=t_new, past=past, n_cb=n_cb, k_cache=k_cache,
                             single_step=n_pages == P)
    return pl.pallas_call(
        kern,
        grid_spec=pltpu.PrefetchScalarGridSpec(
            num_scalar_prefetch=1,
            grid=(n_all // NS, n_pages // P),
            in_specs=[page(s, k) for s in range(NS) for k in range(P)] + [
                per_seq(t_new, QW), per_seq(t_new, QW), per_seq(t_new, LANES), per_seq(nch, KV_W),
                per_seq(t_new, KV_W), per_seq(KV_W, WINDOW), per_seq(t_new, KV_W),
                pl.BlockSpec((nch, LANES), lambda b, j, pt: (0, 0), pipeline_mode=pl.Buffered(1))],
            out_specs=[per_seq(t_new, ATTN_W), per_seq(KV_W, WINDOW)],
            scratch_shapes=[pltpu.VMEM((NS, nr, 2 * LANES), BF16)] + [pltpu.VMEM((NS, nr, LANES), F32)] * 5),
        out_shape=[jax.ShapeDtypeStruct((n_all, t_new, ATTN_W), F32),
                   jax.ShapeDtypeStruct((n_all, KV_W, WINDOW), F32)],
        compiler_params=_params("arbitrary", "arbitrary"),
        name="nsa_sample",
    )(page_table, *([cache_sel] * (NS * P)), q, qr, gates, ckv, sel_new, cache_win, win_new,
      _overlap_t(nch, LANES).T)


def _pool_windows(ext_ref, n_rows, pos, w_ref, lead):
    outs = []
    for gi, w in enumerate(POOL_WINDOWS):
        lanes = slice(gi * POOL_CH, (gi + 1) * POOL_CH)
        tot = None
        for k in range(w):
            v = ext_ref[lead + (slice(HALO - k, HALO - k + n_rows), lanes)]
            tot = v if tot is None else tot + v
        cnt = jnp.minimum(pos + 1, w).astype(F32)
        d = tot / cnt - ext_ref[lead + (slice(HALO, HALO + n_rows), lanes)]
        outs.append((d, w_ref[gi]))
    return outs


def _pool_prompt_kernel(x_ref, h_ref, w_ref, o_ref, ext_ref, *, tm):
    i = pl.program_id(1)
    ext_ref[0:HALO, :] = jnp.where(i == 0, 0.0, h_ref[...])
    ext_ref[HALO:, :] = x_ref[...]
    pos = i * tm + lax.broadcasted_iota(jnp.int32, (tm, 1), 0)
    res = [_dot(d.astype(BF16), w) for d, w in _pool_windows(ext_ref, tm, pos, w_ref, ())]
    o_ref[...] = jnp.concatenate(res, axis=1)


def _pool_prompt(pool_in, w_pool_b, n_seq, tm):
    n = pool_in.shape[0]
    nt = n // n_seq // tm
    hb = tm // HALO
    return pl.pallas_call(
        functools.partial(_pool_prompt_kernel, tm=tm),
        grid=(n_seq, nt),
        in_specs=[pl.BlockSpec((tm, POOL_W), lambda b, i: (b * nt + i, 0)),
                  pl.BlockSpec((HALO, POOL_W), lambda b, i: (jnp.maximum((b * nt + i) * hb - 1, 0), 0)),
                  _const_spec((N_POOL_GROUPS, POOL_CH, POOL_CH))],
        out_specs=pl.BlockSpec((tm, POOL_W), lambda b, i: (b * nt + i, 0)),
        out_shape=jax.ShapeDtypeStruct((n, POOL_W), F32),
        scratch_shapes=[pltpu.VMEM((HALO + tm, POOL_W), F32)],
        compiler_params=_params("parallel", "arbitrary"),
        name="pool_prompt",
    )(pool_in, pool_in, w_pool_b)


def _pool_sample_kernel(x_ref, st_ref, w_ref, o_ref, ns_ref, ext_ref, *, t_new, past):
    nb = x_ref.shape[0]
    ext_ref[:, HALO - POOL_HIST:HALO, :] = st_ref[...]
    ext_ref[:, HALO:, :] = x_ref[...]
    pos = past + lax.broadcasted_iota(jnp.int32, (1, t_new, 1), 1)
    res = [_dot(d.reshape(nb * t_new, POOL_CH).astype(BF16), w)
           for d, w in _pool_windows(ext_ref, t_new, pos, w_ref, (slice(None),))]
    o_ref[...] = jnp.concatenate(res, axis=1)
    ns_ref[...] = ext_ref[:, HALO + t_new - POOL_HIST:HALO + t_new, :]


def _pool_sample(pool_in, state, w_pool_b, past, nb):
    n_seq, t_new, _ = pool_in.shape
    assert t_new % 8 == 0 and n_seq % nb == 0 and past >= POOL_HIST
    return pl.pallas_call(
        functools.partial(_pool_sample_kernel, t_new=t_new, past=past),
        grid=(n_seq // nb,),
        in_specs=[pl.BlockSpec((nb, t_new, POOL_W), lambda i: (i, 0, 0)),
                  pl.BlockSpec((nb, POOL_HIST, POOL_W), lambda i: (i, 0, 0)),
                  _const_spec((N_POOL_GROUPS, POOL_CH, POOL_CH))],
        out_specs=[pl.BlockSpec((nb * t_new, POOL_W), lambda i: (i, 0)),
                   pl.BlockSpec((nb, POOL_HIST, POOL_W), lambda i: (i, 0, 0))],
        out_shape=[jax.ShapeDtypeStruct((n_seq * t_new, POOL_W), F32),
                   jax.ShapeDtypeStruct((n_seq, POOL_HIST, POOL_W), F32)],
        scratch_shapes=[pltpu.VMEM((nb, HALO + t_new, POOL_W), F32)],
        compiler_params=_params("parallel"),
        name="pool_sample",
    )(pool_in, state, w_pool_b)


def _mix_kernel(x_ref, a_ref, m_ref, ga_ref, gp_ref, wo_ref, g2_ref, wq_ref, x1_ref, qm_ref):
    mix = jnp.concatenate([_rms(a_ref[...], ga_ref[...]), _rms(m_ref[...], gp_ref[...])], axis=1)
    x1 = x_ref[...] + _dot(mix.astype(BF16), wo_ref[...])
    x1_ref[...] = x1
    qm_ref[...] = _dot(_rms(x1, g2_ref[...]).astype(BF16), wq_ref[...]).astype(BF16)


def _mix(x, a, m, ga, gp, w_out_b, g2, w_mq_b, tm):
    n = x.shape[0]
    row = lambda w: pl.BlockSpec((tm, w), lambda i: (i, 0))
    return pl.pallas_call(
        _mix_kernel,
        grid=(n // tm,),
        in_specs=[row(D_MODEL), row(ATTN_W), row(POOL_W), _const_spec((1, ATTN_W)), _const_spec((1, POOL_W)),
                  _const_spec((D_MODEL, D_MODEL)), _const_spec((1, D_MODEL)), _const_spec((D_MODEL, MEM_W))],
        out_specs=[row(D_MODEL), row(MEM_W)],
        out_shape=[jax.ShapeDtypeStruct((n, D_MODEL), F32), jax.ShapeDtypeStruct((n, MEM_W), BF16)],
        compiler_params=_params("parallel"),
        name="mix_out",
    )(x, a, m, ga, gp, w_out_b, g2, w_mq_b)


def _cross_kernel(q_ref, kv_ref, o_ref, *, n_mem_blk, rows):
    pairs = [(i, h) for i in range(n_mem_blk) for h in range(MEM_HEADS)]
    head = lambda off, i, h: kv_ref[i, pl.ds(off + h, N_MEM, stride=2 * MEM_HEADS), :].astype(BF16)
    scores = [_dot_nt(q_ref[i * rows:(i + 1) * rows, h * MEM_HEAD_DIM:(h + 1) * MEM_HEAD_DIM], head(0, i, h))
              * (MEM_HEAD_DIM ** -0.5) for i, h in pairs]
    es = [jnp.exp(s - jnp.max(s, axis=-1, keepdims=True)) for s in scores]
    outs = [_dot(e.astype(BF16), head(MEM_HEADS, i, h)) / jnp.sum(e, axis=-1, keepdims=True)
            for e, (i, h) in zip(es, pairs)]
    for i in range(n_mem_blk):
        o_ref[i * rows:(i + 1) * rows, :] = jnp.concatenate(
            outs[i * MEM_HEADS:(i + 1) * MEM_HEADS], axis=1).astype(BF16)


def _cross(qm, mem_rows, rows, n_mem_blk):
    n = qm.shape[0]
    tm = rows * n_mem_blk
    return pl.pallas_call(
        functools.partial(_cross_kernel, n_mem_blk=n_mem_blk, rows=rows),
        grid=(n // tm,),
        in_specs=[pl.BlockSpec((tm, MEM_W), lambda i: (i, 0)),
                  pl.BlockSpec((n_mem_blk,) + mem_rows.shape[1:], lambda i: (i, 0, 0))],
        out_specs=pl.BlockSpec((tm, MEM_W), lambda i: (i, 0)),
        out_shape=jax.ShapeDtypeStruct((n, MEM_W), BF16),
        compiler_params=_params("parallel"),
        name="cross_attn",
    )(qm, mem_rows)


def _cross_shared_kernel(q_ref, k_ref, vt_ref, o_ref):
    lanes = [slice(h * MEM_HEAD_DIM, (h + 1) * MEM_HEAD_DIM) for h in range(MEM_HEADS)]
    scores = [_dot_nt(k_ref[:, l], q_ref[:, l]) * (MEM_HEAD_DIM ** -0.5) for l in lanes]
    es = [jnp.exp(s - jnp.max(s, axis=0, keepdims=True)).astype(BF16) for s in scores]
    accs = [_dot(vt_ref[0, h], e) for h, e in enumerate(es)]
    outs = [(a[0:MEM_HEAD_DIM, :] / a[MEM_HEAD_DIM:MEM_HEAD_DIM + 1, :]).T for a in accs]
    o_ref[...] = jnp.concatenate(outs, axis=1).astype(BF16)


def _cross_shared(qm, mem_k, mem_vt, tm):
    n = qm.shape[0]
    tiles_per_mem = n // mem_vt.shape[0] // tm
    return pl.pallas_call(
        _cross_shared_kernel,
        grid=(n // tm,),
        in_specs=[pl.BlockSpec((tm, MEM_W), lambda i: (i, 0)),
                  pl.BlockSpec((N_MEM, MEM_W), lambda i: (i // tiles_per_mem, 0)),
                  pl.BlockSpec((1, MEM_HEADS, VT_ROWS, N_MEM), lambda i: (i // tiles_per_mem, 0, 0, 0))],
        out_specs=pl.BlockSpec((tm, MEM_W), lambda i: (i, 0)),
        out_shape=jax.ShapeDtypeStruct((n, MEM_W), BF16),
        compiler_params=_params("parallel"),
        name="cross_attn_shared",
    )(qm, mem_k, mem_vt)


def _ffn_kernel(x1_ref, o_ref, wmo_ref, g3_ref, wup_ref, wdn_ref, gf_ref, y_ref):
    x2 = x1_ref[...] + _dot(o_ref[...], wmo_ref[...])
    xn = _rms(x2, g3_ref[...]).astype(BF16)
    acc = x2
    for c in range(D_FF // FF_CHUNK):
        h = jnp.maximum(_dot(xn, wup_ref[:, c * FF_CHUNK:(c + 1) * FF_CHUNK]), 0.0)
        acc = acc + _dot((h * h).astype(BF16), wdn_ref[c * FF_CHUNK:(c + 1) * FF_CHUNK, :])
    y_ref[...] = _rms(acc, gf_ref[...])


def _ffn(x1, o, w_mo_b, g3, w_up_b, w_down_b, g_final, tm):
    n = x1.shape[0]
    row = lambda w: pl.BlockSpec((tm, w), lambda i: (i, 0))
    return pl.pallas_call(
        _ffn_kernel,
        grid=(n // tm,),
        in_specs=[row(D_MODEL), row(MEM_W), _const_spec((MEM_W, D_MODEL)), _const_spec((1, D_MODEL)),
                  _const_spec((D_MODEL, D_FF)), _const_spec((D_FF, D_MODEL)), _const_spec((1, D_MODEL))],
        out_specs=row(D_MODEL),
        out_shape=jax.ShapeDtypeStruct((n, D_MODEL), F32),
        compiler_params=_params("parallel"),
        name="ffn",
    )(x1, o, w_mo_b, g3, w_up_b, w_down_b, g_final)


def _tile(n, pref):
    t = min(n, pref)
    assert n % t == 0
    return t


def kernel(x_prompt, x_sample, cache_cmp_kv, cache_sel_kv, cache_win_kv, state_pool, cache_mem_kv, page_table, mem_prompt, w_in, pe_cmp, w_cmp1, w_cmp2, w_pool, g_attn_out, pool_scale, w_out, g_norm1, g_norm2, g_mem, w_mq, w_mkv, w_mo, g_norm3, w_up, w_down, g_final):
    assert w_in.shape[0] == 1, "single-layer trunk"
    B, S, _ = x_prompt.shape
    DB, T, _ = x_sample.shape
    past = page_table.shape[1] * PAGE_SIZE
    kv5 = (2, N_KV_HEADS, HEAD_DIM)
    r2 = lambda v: v.reshape(1, -1)

    w_in_p = _permute_w_in(w_in[0])
    w_out_b, w_mq_b, w_mkv_b, w_mo_b = (w[0].astype(BF16) for w in (w_out, w_mq, w_mkv, w_mo))
    w_up_b, w_down_b, w_pool_b = w_up[0].astype(BF16), w_down[0].astype(BF16), w_pool[0].astype(BF16)
    g1, g2, g3, gm, ga, gp, gf = (r2(v) for v in (g_norm1[0], g_norm2[0], g_norm3[0], g_mem[0], g_attn_out[0],
                                                  pool_scale[0], g_final))
    cw = _compress_weights(pe_cmp[0], w_cmp1[0], w_cmp2[0])

    np_tok = B * S
    tm_p = _tile(S, 512)
    xp = x_prompt.reshape(np_tok, D_MODEL)
    q, qr, cmp_t, sel_t, win_t, sel_b, win_b, gates, pool_in = _in_proj(
        xp, g1, w_in_p, jnp.arange(S), tm_p, n_seq=B)
    mem_f, mem_k, mem_vt = _mem_kv(mem_prompt.reshape(B * N_MEM, D_MODEL), gm, w_mkv_b)
    ckv = _compress_seq(cmp_t, cw, 64)
    a = _nsa_prompt(q, qr, gates, ckv, sel_b, win_b, B)
    m = _pool_prompt(pool_in, w_pool_b, B, tm_p)
    x1, qm = _mix(xp, a, m, ga, gp, w_out_b, g2, w_mq_b, tm_p)
    o = _cross_shared(qm, mem_k, mem_vt, tm_p)
    y_p = _ffn(x1, o, w_mo_b, g3, w_up_b, w_down_b, gf, tm_p)
    rows_first = lambda t: t.reshape(t.shape[:1] + kv5 + t.shape[-1:]).transpose(0, 4, 1, 2, 3)[None]
    p_state = (rows_first(cmp_t), rows_first(sel_t), rows_first(win_t[:, :, -min(WINDOW, S):]),
               pool_in.reshape(B, S, POOL_W)[None, :, -POOL_HIST:],
               mem_f.reshape(1, B, N_MEM, 2, MEM_HEADS, MEM_HEAD_DIM))

    ns_tok = DB * T
    tm_s = _tile(ns_tok, 512)
    xs = x_sample.reshape(ns_tok, D_MODEL)
    q, qr, cmp_kv, sel_kv, win_kv, _, _, gates, pool_in = _in_proj(
        xs, g1, w_in_p, jnp.tile(past + jnp.arange(T), DB), tm_s)
    r3 = lambda v: v.reshape(DB, T, v.shape[-1])
    ckv = _compress_paged(_pages_t(cache_cmp_kv[0]), page_table, cw, 64)
    a, s_win = _nsa_sample(r3(q), r3(qr), r3(gates), ckv, _pages_t(cache_sel_kv[0]), page_table,
                           r3(sel_kv), _pages_t(cache_win_kv[0]), r3(win_kv), 64, 2)
    m, s_pool = _pool_sample(r3(pool_in), state_pool[0], w_pool_b, past, _tile(DB, 16))
    x1, qm = _mix(xs, a.reshape(ns_tok, ATTN_W), m, ga, gp, w_out_b, g2, w_mq_b, tm_s)
    o = _cross(qm, cache_mem_kv[0].reshape(DB, N_MEM * 2 * MEM_HEADS, MEM_HEAD_DIM), T, _tile(DB, 4))
    y_s = _ffn(x1, o, w_mo_b, g3, w_up_b, w_down_b, gf, tm_s)
    s_state = (cmp_kv.reshape((1, DB, T) + kv5), sel_kv.reshape((1, DB, T) + kv5),
               rows_first(s_win), s_pool[None])

    return (y_p.reshape(B, S, D_MODEL), y_s.reshape(DB, T, D_MODEL)) + p_state + s_state
```

```python
import functools

import jax
import jax.numpy as jnp
from jax import lax
from jax.experimental import pallas as pl
from jax.experimental.pallas import tpu as pltpu

D_MODEL = 1024
PAGE_SIZE = 128
ATTN_W = 512
POOL_W = 512
HEAD_DIM = 64
N_HEADS = 8
N_KV_HEADS = 2
HEADS_PER_KV = N_HEADS // N_KV_HEADS
SCALE = HEAD_DIM ** -0.5
LOG2E = 1.4426950408889634
Q_SCALE = SCALE * LOG2E
ROT_DIM = HEAD_DIM // 4
ROPE_THETA = 500000.0
CMP_LEN = 32
CMP_STRIDE = 16
CMP_HID = HEAD_DIM
SEL_BLOCK = 64
SEL_TOPN = 16
WINDOW = 512
FORCE_BONUS = 1000.0
POOL_WINDOWS = (2, 4, 8, 16)
N_POOL_GROUPS = len(POOL_WINDOWS)
POOL_CH = POOL_W // N_POOL_GROUPS
POOL_HIST = max(POOL_WINDOWS) - 1
N_MEM = 256
MEM_HEADS = 4
MEM_HEAD_DIM = 128
MEM_W = MEM_HEADS * MEM_HEAD_DIM
D_FF = 4 * D_MODEL
EPS = 1e-6
KV_W = 2 * N_KV_HEADS * HEAD_DIM
N_GATES = 3 * N_HEADS
LANES = 128
VMEM_LIMIT = 56 * 1024 * 1024
QW = N_HEADS * LANES
TQ = 128
SEL_HALVINGS = 2
NEG = -1e30
CHUNK_W = CMP_STRIDE * KV_W
CMP_R = CMP_LEN // CMP_STRIDE
CHUNKS_PER_PAGE = PAGE_SIZE // CMP_STRIDE
HALO = 16
FF_CHUNK = 1024

F32 = jnp.float32
BF16 = jnp.bfloat16


def _rms(x, g):
    return x * lax.rsqrt(jnp.mean(x * x, axis=-1, keepdims=True) + EPS) * g


def _dot(a, b):
    return jnp.dot(a, b, preferred_element_type=F32)


def _dot_nt(a, b):
    return lax.dot_general(a, b, (((1,), (1,)), ((), ())), preferred_element_type=F32)


ONES_ROWS = 16
VT_ROWS = LANES + ONES_ROWS


def _ones_rows(n):
    return jnp.where(lax.broadcasted_iota(jnp.int32, (ONES_ROWS, n), 0) == 0, 1.0, 0.0).astype(BF16)


def _const_spec(shape):
    n = len(shape)
    return pl.BlockSpec(shape, lambda *_: (0,) * n, pipeline_mode=pl.Buffered(1))


def _params(*sem):
    return pltpu.CompilerParams(dimension_semantics=sem, vmem_limit_bytes=VMEM_LIMIT)


X_Q, X_CMP, X_SEL, X_WIN, X_POOL, X_GATE = 0, 1024, 1280, 1536, 1792, 2304
X_IN = X_GATE + LANES


def _permute_w_in(w_in):
    q, kc, ks, kw, gl, pool = jnp.split(w_in, (512, 768, 1024, 1280, 1280 + N_GATES), axis=1)
    z = jnp.zeros((D_MODEL, HEAD_DIM), w_in.dtype)
    qx = []
    for h in range(N_HEADS):
        qh = q[:, h * HEAD_DIM:(h + 1) * HEAD_DIM]
        qx += [qh, z] if h // HEADS_PER_KV == 0 else [z, qh]
    pad = jnp.zeros((D_MODEL, LANES - N_GATES), w_in.dtype)
    return jnp.concatenate(qx + [kc, ks, kw, pool, gl, pad], axis=1).astype(BF16)


def _rope_tables(pos):
    half = ROT_DIM // 2
    inv = ROPE_THETA ** (-jnp.arange(half, dtype=F32) / half)
    ang = pos.astype(F32)[:, None] * inv[None, :]
    cos, sin = jnp.cos(ang), jnp.sin(ang)
    z = jnp.zeros_like(cos)
    rest0 = jnp.zeros((pos.shape[0], HEAD_DIM - ROT_DIM), F32)
    cs = jnp.concatenate([cos, cos, rest0 + 1.0], axis=1)
    sa = jnp.concatenate([-sin, z, rest0], axis=1)
    sb = jnp.concatenate([z, sin, rest0], axis=1)
    return tuple(jnp.tile(t, (1, LANES // HEAD_DIM)) for t in (cs, sa, sb))


def _rope(v, cs, sa, sb):
    n = v.shape[1] // LANES
    w = v.shape[1]
    cs, sa, sb = (jnp.tile(t, (1, n)) for t in (cs, sa, sb))
    return v * cs + pltpu.roll(v, w - ROT_DIM // 2, 1) * sa + pltpu.roll(v, ROT_DIM // 2, 1) * sb


def _in_proj_kernel(x_ref, g_ref, w_ref, cs_ref, sa_ref, sb_ref,
                    q_ref, qr_ref, cmp_ref, sel_ref, win_ref, selb_ref, winb_ref, gate_ref, pool_ref, *, feature_major):
    xb = _rms(x_ref[...], g_ref[...]).astype(BF16)
    cs, sa, sb = cs_ref[...], sa_ref[...], sb_ref[...]

    def put_state(ref, v):
        if feature_major:
            ref[0] = jnp.concatenate([v[:, c * LANES:(c + 1) * LANES].T for c in range(KV_W // LANES)], axis=0)
        else:
            ref[...] = v

    q = _dot(xb, w_ref[:, X_Q:X_CMP])
    q_ref[...] = (q * Q_SCALE).astype(BF16)
    qr_ref[...] = (_rope(q, cs, sa, sb) * Q_SCALE).astype(BF16)
    put_state(cmp_ref, _dot(xb, w_ref[:, X_CMP:X_SEL]))
    for lo, f_ref, b_ref in ((X_SEL, sel_ref, selb_ref), (X_WIN, win_ref, winb_ref)):
        kv = _dot(xb, w_ref[:, lo:lo + KV_W])
        kv = jnp.concatenate([_rope(kv[:, :LANES], cs, sa, sb), kv[:, LANES:]], axis=1)
        put_state(f_ref, kv)
        b_ref[...] = kv.astype(BF16)
    pool_ref[...] = _dot(xb, w_ref[:, X_POOL:X_GATE])
    gate_ref[...] = jax.nn.sigmoid(_dot(xb, w_ref[:, X_GATE:X_IN]))


def _in_proj(x, g1, w_in_p, pos_rows, tm, n_seq=None):
    n = x.shape[0]
    cs, sa, sb = _rope_tables(pos_rows)
    row = lambda w: pl.BlockSpec((tm, w), lambda i: (i, 0))
    tab = pl.BlockSpec((tm, LANES), lambda i: (i % (pos_rows.shape[0] // tm), 0))
    sds = lambda w, dt: jax.ShapeDtypeStruct((n, w), dt)
    if n_seq is None:
        st_spec, st_shape = row(KV_W), sds(KV_W, F32)
    else:
        nt = n // n_seq // tm
        st_spec = pl.BlockSpec((1, KV_W, tm), lambda i: (i // nt, 0, i % nt))
        st_shape = jax.ShapeDtypeStruct((n_seq, KV_W, n // n_seq), F32)
    return pl.pallas_call(
        functools.partial(_in_proj_kernel, feature_major=n_seq is not None),
        grid=(n // tm,),
        in_specs=[row(D_MODEL), _const_spec((1, D_MODEL)), _const_spec((D_MODEL, X_IN)),
                  tab, tab, tab],
        out_specs=[row(QW), row(QW), st_spec, st_spec, st_spec, row(KV_W), row(KV_W), row(LANES), row(POOL_W)],
        out_shape=[sds(QW, BF16), sds(QW, BF16), st_shape, st_shape, st_shape,
                   sds(KV_W, BF16), sds(KV_W, BF16), sds(LANES, F32), sds(POOL_W, F32)],
        compiler_params=_params("parallel"),
        name="in_proj",
    )(x, g1, w_in_p, cs, sa, sb)


def _mem_kv_kernel(m_ref, g_ref, w_ref, o_ref, kb_ref, vt_ref):
    kv = _dot(_rms(m_ref[...], g_ref[...]).astype(BF16), w_ref[...])
    o_ref[...] = kv
    kb_ref[...] = kv[:, :MEM_W].astype(BF16)
    for h in range(MEM_HEADS):
        v = kv[:, MEM_W + h * MEM_HEAD_DIM:MEM_W + (h + 1) * MEM_HEAD_DIM]
        vt_ref[0, h] = jnp.concatenate([v.T.astype(BF16), _ones_rows(N_MEM)], axis=0)


def _mem_kv(mem, g_mem, w_mkv_b):
    n = mem.shape[0]
    row = lambda w: pl.BlockSpec((N_MEM, w), lambda i: (i, 0))
    return pl.pallas_call(
        _mem_kv_kernel,
        grid=(n // N_MEM,),
        in_specs=[row(D_MODEL), _const_spec((1, D_MODEL)), _const_spec((D_MODEL, 2 * MEM_W))],
        out_specs=[row(2 * MEM_W), row(MEM_W),
                   pl.BlockSpec((1, MEM_HEADS, VT_ROWS, N_MEM), lambda i: (i, 0, 0, 0))],
        out_shape=[jax.ShapeDtypeStruct((n, 2 * MEM_W), F32), jax.ShapeDtypeStruct((n, MEM_W), BF16),
                   jax.ShapeDtypeStruct((n // N_MEM, MEM_HEADS, VT_ROWS, N_MEM), BF16)],
        compiler_params=_params("parallel"),
        name="mem_kv",
    )(mem, g_mem, w_mkv_b)


def _compress_weights(pe, w1, w2):
    w1r = w1.reshape(2, CMP_R, CMP_STRIDE, HEAD_DIM, CMP_HID)
    eye_g = jnp.eye(N_KV_HEADS, dtype=w1.dtype)
    w1c = jnp.einsum('crsdh,gy->csgdryh', w1r, eye_g).reshape(2, CHUNK_W // 2, CMP_R * LANES)
    pe_term = jnp.einsum('cld,cldh->ch', pe, w1.reshape(2, CMP_LEN, HEAD_DIM, CMP_HID))
    bias = jnp.broadcast_to(pe_term[:, None, :], (2, N_KV_HEADS, CMP_HID)).reshape(2, 1, LANES)
    w2c = jnp.einsum('chd,gy->cghyd', w2, eye_g).reshape(2, LANES, LANES)
    return w1c.astype(BF16), bias, w2c.astype(BF16)


def _compress_kernel(*refs, n_in, ppr, has_pt):
    if has_pt:
        refs = refs[1:]
    x_refs = refs[:n_in]
    w1_ref, b_ref, w2_ref, o_ref, carry_ref = refs[n_in:]

    @pl.when(pl.program_id(1) == 0)
    def _():
        carry_ref[...] = jnp.zeros_like(carry_ref)

    r_i = lax.broadcasted_iota(jnp.int32, (PAGE_SIZE, PAGE_SIZE), 0)
    t_i = lax.broadcasted_iota(jnp.int32, (PAGE_SIZE, PAGE_SIZE), 1)
    src = CMP_STRIDE * (r_i & (CHUNKS_PER_PAGE - 1)) + lax.shift_right_logical(r_i, 3)
    pick = jnp.where(t_i == src, 1.0, 0.0).astype(BF16)
    pages = [_dot_nt(pick, r[0, :, j * PAGE_SIZE:(j + 1) * PAGE_SIZE].astype(BF16))
             for r in x_refs for j in range(ppr)]
    rows = len(pages) * CHUNKS_PER_PAGE
    row = lax.broadcasted_iota(jnp.int32, (rows, LANES), 0)
    outs = []
    for c in range(KV_W // LANES):
        lanes = slice(c * LANES, (c + 1) * LANES)
        x = jnp.concatenate(
            [jnp.concatenate([pg[s * CHUNKS_PER_PAGE:(s + 1) * CHUNKS_PER_PAGE, lanes] for pg in pages], axis=0)
             for s in range(CMP_STRIDE)], axis=1)
        u = _dot(x.astype(BF16), w1_ref[c])
        u0, u1 = u[:, :LANES], u[:, LANES:]
        prev = jnp.where(row == 0, carry_ref[0:1, lanes], pltpu.roll(u0, 1, 0))
        carry_ref[:, lanes] = jnp.broadcast_to(u0[rows - 1:rows, :], (carry_ref.shape[0], LANES))
        h = prev + u1 + b_ref[c]
        outs.append(_dot(jax.nn.gelu(h).astype(BF16), w2_ref[c]))
    o_ref[0] = jnp.concatenate(outs, axis=1).astype(BF16)


def _compress_call(n_seq, n_pages, n_in, ppr, has_pt, x_spec_fn):
    P = n_in * ppr
    assert n_pages % P == 0
    grid = (n_seq, n_pages // P)
    in_specs = [x_spec_fn(k) for k in range(n_in)] + [
        _const_spec((2, CHUNK_W // 2, CMP_R * LANES)), _const_spec((2, 1, LANES)), _const_spec((2, LANES, LANES))]
    out_spec = pl.BlockSpec((1, P * CHUNKS_PER_PAGE, KV_W), (lambda b, i, *_: (b, i, 0)))
    kern = functools.partial(_compress_kernel, n_in=n_in, ppr=ppr, has_pt=has_pt)
    common = dict(out_shape=jax.ShapeDtypeStruct((n_seq, n_pages * CHUNKS_PER_PAGE, KV_W), BF16),
                  compiler_params=_params("parallel", "arbitrary"), name="compress")
    scratch = [pltpu.VMEM((8, KV_W), F32)]
    if has_pt:
        return pl.pallas_call(kern, grid_spec=pltpu.PrefetchScalarGridSpec(
            num_scalar_prefetch=1, grid=grid, in_specs=in_specs, out_specs=out_spec, scratch_shapes=scratch), **common)
    return pl.pallas_call(kern, grid=grid, in_specs=in_specs, out_specs=out_spec, scratch_shapes=scratch, **common)


def _compress_seq(cmp_t, cw, pages_per_step):
    n_seq, _, seq = cmp_t.shape
    n_pages = seq // PAGE_SIZE
    P = min(pages_per_step, n_pages)
    spec = lambda k: pl.BlockSpec((1, KV_W, P * PAGE_SIZE), lambda b, i: (b, 0, i))
    return _compress_call(n_seq, n_pages, 1, P, False, spec)(cmp_t, *cw)


def _pages_t(cache):
    return cache.transpose(0, 2, 3, 4, 1).reshape(cache.shape[0], KV_W, cache.shape[1])


def _compress_paged(cache_t, page_table, cw, pages_per_step):
    n_seq, n_pages = page_table.shape
    P = min(pages_per_step, n_pages)
    spec = lambda k: pl.BlockSpec((1, KV_W, PAGE_SIZE), lambda b, i, pt: (pt[b, i * P + k], 0, 0))
    return _compress_call(n_seq, n_pages, P, 1, True, spec)(page_table, *([cache_t] * P), *cw)


def _overlap_t(nch, n_sel_pad):
    m = jnp.arange(nch)[None, :]
    j = jnp.arange(n_sel_pad)[:, None]
    lo = (m - 1) * CMP_STRIDE
    ov = jnp.maximum(jnp.minimum(lo + CMP_LEN, (j + 1) * SEL_BLOCK) - jnp.maximum(lo, j * SEL_BLOCK), 0)
    return jnp.where(m >= 1, ov.astype(F32) / CMP_LEN, 0.0).astype(BF16)


def _heads_to_rows(x):
    return jnp.concatenate([x[:, h * LANES:(h + 1) * LANES] for h in range(N_HEADS)], axis=0)


def _topk_bias(imp, pos, n_blk, k_top, axis):
    j = lax.broadcasted_iota(jnp.int32, imp.shape, axis)
    cur = lax.shift_right_logical(pos, 6)
    forced = (j == 0) | (j == cur) | (j == cur - 1)
    score = jnp.where(j * SEL_BLOCK <= pos, imp + jnp.where(forced, FORCE_BONUS, 0.0), -1e9)
    score = jnp.where(j < n_blk, score, -jnp.inf)
    bias = jnp.full(imp.shape, NEG, F32)
    for _ in range(k_top):
        mx = jnp.max(score, axis=axis, keepdims=True)
        idx = jnp.min(jnp.where(score == mx, j, LANES), axis=axis, keepdims=True)
        hit = j == idx
        bias = jnp.where(hit, 0.0, bias)
        score = jnp.where(hit, -jnp.inf, score)
    return bias


def _split_dot(dot, ps):
    hi = ps.astype(BF16)
    lo = (ps - hi.astype(F32)).astype(BF16)
    return dot(hi) + dot(lo)


def _nsa_prompt_kernel(q_ref, qr_ref, g_ref, ckv_ref, ksel_ref, kwin_ref, ovt_ref, a_ref,
                       kx_ref, vts_ref, vtw_ref, lc_ref, *, tk, n_sel, k_top):
    i = pl.program_id(1)
    q0 = pl.multiple_of(i * TQ, TQ)
    nc = ckv_ref.shape[1]

    @pl.when(i == 0)
    def _():
        kx_ref[...] = jnp.zeros_like(kx_ref)
        vts_ref[...] = jnp.zeros_like(vts_ref)
        vtw_ref[...] = jnp.zeros_like(vtw_ref)
        lc_ref[0:LANES, :] = ckv_ref[0, :, LANES:2 * LANES].astype(F32).T.astype(BF16)
        lc_ref[LANES:2 * LANES, :] = ovt_ref[...]
        lc_ref[2 * LANES:, :] = _ones_rows(nc)

    key_t = q0 + lax.broadcasted_iota(jnp.int32, (TQ, 1), 0)
    onehot = jnp.where(lax.shift_right_logical(key_t, 6) == lax.broadcasted_iota(jnp.int32, (TQ, LANES), 1), 1.0, 0.0)
    kx_ref[pl.ds(q0, TQ), :] = jnp.concatenate([ksel_ref[pl.ds(q0, TQ), 0:LANES], onehot.astype(BF16)], axis=1)
    for k_ref, vt_ref in ((ksel_ref, vts_ref), (kwin_ref, vtw_ref)):
        v = k_ref[pl.ds(q0, TQ), LANES:2 * LANES].astype(F32)
        vt_ref[:, pl.ds(q0, TQ)] = jnp.concatenate([v.T.astype(BF16), _ones_rows(TQ)], axis=0)

    nl = N_HEADS * TQ
    lane = lax.broadcasted_iota(jnp.int32, (1, nl), 1)
    pos = q0 + (lane & (TQ - 1))
    qc = _heads_to_rows(q_ref)
    qr = _heads_to_rows(qr_ref)

    m_idx = lax.broadcasted_iota(jnp.int32, (nc, 1), 0)
    c_end = jnp.where(m_idx >= 1, m_idx * CMP_STRIDE + (CMP_STRIDE - 1), 2 ** 30)
    s = jnp.where(c_end <= pos, _dot_nt(ckv_ref[0, :, 0:LANES], qc), NEG)
    acc = _dot(lc_ref[...], jnp.exp2(s - jnp.max(s, axis=0, keepdims=True)).astype(BF16))
    rden = jnp.where(pos >= CMP_LEN - 1, 1.0 / jnp.maximum(acc[2 * LANES:2 * LANES + 1, :], 1e-30), 0.0)
    oc_t = acc[0:LANES, :] * rden
    imp_h = acc[LANES:2 * LANES, :] * rden

    wk = WINDOW + TQ
    w0 = pl.multiple_of(jnp.maximum(q0 - WINDOW, 0), TQ)
    sw = _dot_nt(kwin_ref[pl.ds(w0, wk), 0:LANES], qr)
    kp = w0 + lax.broadcasted_iota(jnp.int32, (wk, 1), 0)
    sw = jnp.concatenate([jnp.where((kp[:TQ] <= pos) & (kp[:TQ] >= pos - WINDOW), sw[:TQ], NEG),
                         jnp.where(kp[TQ:] <= pos, sw[TQ:], NEG)], axis=0)
    accw = _dot(vtw_ref[:, pl.ds(w0, wk)], jnp.exp2(sw - jnp.max(sw, axis=0, keepdims=True)).astype(BF16))
    ow_t = accw[0:LANES, :] / jnp.maximum(accw[LANES:LANES + 1, :], 1e-30)

    imp_t = jnp.concatenate(
        [sum(imp_h[:, (g * HEADS_PER_KV + hh) * TQ:(g * HEADS_PER_KV + hh + 1) * TQ] for hh in range(HEADS_PER_KV))
         for g in range(N_KV_HEADS)], axis=1)
    bias_t = _topk_bias(imp_t, pos[:, :N_KV_HEADS * TQ], n_sel, k_top, 0)
    biases = [bias_t[:, g * TQ:(g + 1) * TQ].T.astype(BF16) for g in range(N_KV_HEADS)]
    bias_rows = jnp.concatenate([biases[h // HEADS_PER_KV] for h in range(N_HEADS)], axis=0)
    qs = jnp.concatenate([qr, bias_rows], axis=1)

    def sel_chunk(c0, n, masked, carry):
        m, acc = carry
        s = _dot_nt(kx_ref[pl.ds(c0, n), :], qs)
        if masked:
            s = jnp.where(c0 + lax.broadcasted_iota(jnp.int32, (n, 1), 0) <= pos, s, NEG)
        m_new = jnp.maximum(m, jnp.max(s, axis=0, keepdims=True))
        p = jnp.exp2(s - m_new).astype(BF16)
        acc = jnp.exp2(m - m_new) * acc + _dot(vts_ref[:, pl.ds(c0, n)], p)
        return m_new, acc

    chunk = lambda n: (lambda c, cr, masked=False: sel_chunk(pl.multiple_of(c * n, n), n, masked, cr))
    big = chunk(tk)
    n_big = q0 // tk
    carry = (jnp.full((1, nl), NEG, F32), jnp.zeros((VT_ROWS, nl), F32))
    carry = lax.fori_loop(0, n_big // 2, lambda c, cr: big(2 * c + 1, big(2 * c, cr)), carry)
    carry = lax.fori_loop(n_big // 2 * 2, n_big, big, carry)
    done, n = n_big, tk
    for _ in range(SEL_HALVINGS):
        n //= 2
        carry = lax.fori_loop(done * 2, q0 // n, chunk(n), carry)
        done = q0 // n
    _, acc = chunk(n)(done, carry, True)
    os_t = acc[0:LANES, :] / jnp.maximum(acc[LANES:LANES + 1, :], 1e-30)

    g_t = g_ref[...].T
    outs = []
    for h in range(N_HEADS):
        g = h // HEADS_PER_KV
        blk_h = lambda o: o[g * HEAD_DIM:(g + 1) * HEAD_DIM, h * TQ:(h + 1) * TQ]
        gate = lambda j: g_t[3 * h + j:3 * h + j + 1, :]
        outs.append(gate(0) * blk_h(oc_t) + gate(1) * blk_h(os_t) + gate(2) * blk_h(ow_t))
    a_ref[...] = jnp.concatenate(outs, axis=0).T


def _nsa_prompt(q, qr, gates, ckv, ksel, kwin, n_seq):
    n = q.shape[0]
    seq = n // n_seq
    nch = ckv.shape[1]
    n_sel = seq // SEL_BLOCK
    assert seq % TQ == 0 and seq >= WINDOW + TQ and n_sel <= LANES and seq % SEL_BLOCK == 0
    tk = min(2048, seq)
    assert seq % tk == 0 and (tk >> SEL_HALVINGS) % TQ == 0
    nt = seq // TQ
    tile = lambda w: pl.BlockSpec((TQ, w), lambda b, i: (b * nt + i, 0))
    whole = lambda: pl.BlockSpec((seq, KV_W), lambda b, i: (b, 0), pipeline_mode=pl.Buffered(1))
    kern = functools.partial(_nsa_prompt_kernel, tk=tk, n_sel=n_sel, k_top=min(SEL_TOPN, n_sel))
    return pl.pallas_call(
        kern,
        grid=(n_seq, nt),
        in_specs=[tile(QW), tile(QW), tile(LANES),
                  pl.BlockSpec((1, nch, KV_W), lambda b, i: (b, 0, 0)),
                  whole(), whole(), _const_spec((LANES, nch))],
        out_specs=tile(ATTN_W),
        out_shape=jax.ShapeDtypeStruct((n, ATTN_W), F32),
        scratch_shapes=[pltpu.VMEM((seq, 2 * LANES), BF16), pltpu.VMEM((VT_ROWS, seq), BF16),
                        pltpu.VMEM((VT_ROWS, seq), BF16), pltpu.VMEM((2 * LANES + ONES_ROWS, nch), BF16)],
        compiler_params=_params("arbitrary", "arbitrary"),
        name="nsa_prompt",
    )(q, qr, gates, ckv, ksel, kwin, _overlap_t(nch, LANES))


def _softmax_rows2(s_a, mask_a, s_b, mask_b):
    s_a = jnp.where(mask_a, s_a, NEG)
    s_b = jnp.where(mask_b, s_b, NEG)
    mx = jnp.maximum(jnp.max(s_a, axis=1, keepdims=True), jnp.max(s_b, axis=1, keepdims=True))
    e_a = jnp.where(mask_a, jnp.exp2(s_a - mx), 0.0)
    e_b = jnp.where(mask_b, jnp.exp2(s_b - mx), 0.0)
    den = jnp.sum(e_a, axis=1, keepdims=True) + jnp.sum(e_b, axis=1, keepdims=True)
    return e_a, e_b, jnp.maximum(den, 1e-30)


def _pad_rows(x, rows):
    return jnp.concatenate([x, jnp.zeros((rows - x.shape[0], x.shape[1]), x.dtype)], axis=0)


def _nsa_sample_kernel(*refs, n_seq, n_pg, t_new, past, n_cb, k_cache, single_step):
    pg_refs = refs[1:1 + n_seq * n_pg]
    (q_ref, qr_ref, g_ref, ckv_ref, seln_ref, cwin_ref, winn_ref, ovl_ref,
     a_ref, swin_ref, qs_ref, oc_ref, ow_ref, m_ref, l_ref, acc_ref) = refs[1 + n_seq * n_pg:]
    j = pl.program_id(1)
    nr = N_HEADS * t_new
    row_t = lax.rem(lax.broadcasted_iota(jnp.int32, (nr, 1), 0), t_new)
    pos = past + row_t
    lane = lax.broadcasted_iota(jnp.int32, (1, LANES), 1)

    seqs = range(n_seq)

    def first():
        qc = [_heads_to_rows(q_ref[b].astype(F32)).astype(BF16) for b in seqs]
        qr = [_heads_to_rows(qr_ref[b].astype(F32)).astype(BF16) for b in seqs]
        nc = ckv_ref.shape[1]
        m_idx = lax.broadcasted_iota(jnp.int32, (1, nc), 1)
        cmask = (m_idx >= 1) & (m_idx * CMP_STRIDE + (CMP_STRIDE - 1) <= pos)
        sc = [jnp.where(cmask, _dot_nt(qc[b], ckv_ref[b, :, 0:LANES]), NEG) for b in seqs]
        ec = [jnp.where(cmask, jnp.exp2(s - jnp.max(s, axis=1, keepdims=True)), 0.0) for s in sc]
        p_c = [e / jnp.maximum(jnp.sum(e, axis=1, keepdims=True), 1e-30) for e in ec]
        for b in seqs:
            oc_ref[b] = _dot(p_c[b].astype(BF16), ckv_ref[b, :, LANES:2 * LANES])
        cw = [jnp.concatenate([cwin_ref[b, c * LANES:(c + 1) * LANES, :].T for c in range(KV_W // LANES)], axis=1)
              for b in seqs]
        wn = [_pad_rows(winn_ref[b], LANES) for b in seqs]
        r_idx = lax.broadcasted_iota(jnp.int32, (1, WINDOW), 1)
        sm = [_softmax_rows2(_dot_nt(qr[b], cw[b][:, 0:LANES].astype(BF16)), r_idx >= row_t,
                             _dot_nt(qr[b], wn[b][:, 0:LANES].astype(BF16)), lane <= row_t) for b in seqs]
        for b in seqs:
            e_a, e_b, den = sm[b]
            ow_ref[b] = (_dot(e_a.astype(BF16), cw[b][:, LANES:].astype(BF16))
                         + _dot(e_b.astype(BF16), wn[b][:, LANES:].astype(BF16))) / den
            shifted = pltpu.roll(cwin_ref[b], WINDOW - t_new, 1)
            fresh = pltpu.roll(wn[b].T, LANES - t_new, 1)
            tail = jnp.where(lane >= LANES - t_new, fresh, shifted[:, WINDOW - LANES:])
            swin_ref[b] = jnp.concatenate([shifted[:, :WINDOW - LANES], tail], axis=1)
        imps = [_split_dot(lambda v: _dot(v, ovl_ref[...]),
                           sum(p_c[b][(g * HEADS_PER_KV + hh) * t_new:(g * HEADS_PER_KV + hh + 1) * t_new, :]
                               for hh in range(HEADS_PER_KV)))
                for b in seqs for g in range(N_KV_HEADS)]
        imp_t = _pad_rows(jnp.concatenate(imps, axis=0), LANES).T
        bias = _topk_bias(imp_t, past + lax.rem(lane, t_new), n_cb, k_cache, 0).T
        for b in seqs:
            rows_b = [bias[(b * N_KV_HEADS + h // HEADS_PER_KV) * t_new:(b * N_KV_HEADS + h // HEADS_PER_KV + 1) * t_new, :]
                      for h in range(N_HEADS)]
            qs_ref[b] = jnp.concatenate([qr[b], jnp.concatenate(rows_b, axis=0).astype(BF16)], axis=1)
            m_ref[b] = jnp.full(m_ref.shape[1:], NEG, F32)
            l_ref[b] = jnp.zeros(l_ref.shape[1:], F32)
            acc_ref[b] = jnp.zeros(acc_ref.shape[1:], F32)

    def online(ss, pvs):
        m_new = [jnp.maximum(m_ref[b, :, 0:1], jnp.max(ss[b], axis=1, keepdims=True)) for b in seqs]
        ps = [jnp.exp2(ss[b] - m_new[b]) for b in seqs]
        pv = [pvs[b](ps[b].astype(BF16)) for b in seqs]
        for b in seqs:
            alpha = jnp.exp2(m_ref[b, :, 0:1] - m_new[b])
            l_ref[b] = jnp.broadcast_to(alpha * l_ref[b, :, 0:1] + jnp.sum(ps[b], axis=1, keepdims=True), l_ref.shape[1:])
            acc_ref[b] = alpha * acc_ref[b] + pv[b]
            m_ref[b] = jnp.broadcast_to(m_new[b], m_ref.shape[1:])

    def pages():
        cat = lambda b, lo: jnp.concatenate(
            [r[0, lo:lo + LANES, :] for r in pg_refs[b * n_pg:(b + 1) * n_pg]], axis=1).astype(BF16)
        k_t = [cat(b, 0) for b in seqs]
        v_t = [cat(b, LANES) for b in seqs]
        nk = k_t[0].shape[1]
        blk = lax.shift_right_logical(j * nk + lax.broadcasted_iota(jnp.int32, (1, nk), 1), 6)
        onehot_t = jnp.where(blk == lax.broadcasted_iota(jnp.int32, (LANES, 1), 0), 1.0, 0.0).astype(BF16)
        online([_dot(qs_ref[b], jnp.concatenate([k_t[b], onehot_t], axis=0)) for b in seqs],
               [functools.partial(_dot_nt, b=v_t[b]) for b in seqs])

    def last():
        sn = [_pad_rows(seln_ref[b], LANES) for b in seqs]
        online([jnp.where(lane <= row_t, _dot_nt(qs_ref[b, :, 0:LANES], sn[b][:, 0:LANES].astype(BF16)), NEG)
                for b in seqs],
               [functools.partial(_dot, b=sn[b][:, LANES:].astype(BF16)) for b in seqs])
        for b in seqs:
            os = acc_ref[b] / jnp.maximum(l_ref[b, :, 0:1], 1e-30)
            gts = g_ref[b]
            outs = []
            for h in range(N_HEADS):
                g = h // HEADS_PER_KV
                blk_h = lambda o: o[h * t_new:(h + 1) * t_new, g * HEAD_DIM:(g + 1) * HEAD_DIM]
                gate = lambda k: gts[:, 3 * h + k:3 * h + k + 1]
                outs.append(gate(0) * blk_h(oc_ref[b]) + gate(1) * blk_h(os) + gate(2) * blk_h(ow_ref[b]))
            a_ref[b] = jnp.concatenate(outs, axis=1)

    if single_step:
        first()
        pages()
        last()
    else:
        pl.when(j == 0)(first)
        pages()
        pl.when(j == pl.num_programs(1) - 1)(last)


def _nsa_sample(q, qr, gates, ckv, cache_sel, page_table, sel_new, cache_win, win_new, pages_per_step, seqs_per_step):
    n_all, t_new = q.shape[:2]
    n_pages = page_table.shape[1]
    past = n_pages * PAGE_SIZE
    n_cb = past // SEL_BLOCK
    assert cache_win.shape[2] == WINDOW and n_cb <= LANES and t_new <= SEL_BLOCK and t_new % 8 == 0
    k_cache = min(SEL_TOPN, n_cb + 1) - 1
    assert k_cache >= 2
    P = min(pages_per_step, n_pages)
    NS = min(seqs_per_step, n_all)
    assert n_pages % P == 0 and n_all % NS == 0
    nch = ckv.shape[1]
    nr = N_HEADS * t_new
    per_seq = lambda r, w: pl.BlockSpec((NS, r, w), lambda b, j, pt: (b, 0, 0))
    page = lambda s, k: pl.BlockSpec((1, KV_W, PAGE_SIZE), lambda b, j, pt: (pt[b * NS + s, j * P + k], 0, 0))
    kern = functools.partial(_nsa_sample_kernel, n_seq=NS, n_pg=P, t_new=t_new, past=past, n_cb=n_cb, k_cache=k_cache,
                             single_step=n_pages == P)
    return pl.pallas_call(
        kern,
        grid_spec=pltpu.PrefetchScalarGridSpec(
            num_scalar_prefetch=1,
            grid=(n_all // NS, n_pages // P),
            in_specs=[page(s, k) for s in range(NS) for k in range(P)] + [
                per_seq(t_new, QW), per_seq(t_new, QW), per_seq(t_new, LANES), per_seq(nch, KV_W),
                per_seq(t_new, KV_W), per_seq(KV_W, WINDOW), per_seq(t_new, KV_W),
                pl.BlockSpec((nch, LANES), lambda b, j, pt: (0, 0), pipeline_mode=pl.Buffered(1))],
            out_specs=[per_seq(t_new, ATTN_W), per_seq(KV_W, WINDOW)],
            scratch_shapes=[pltpu.VMEM((NS, nr, 2 * LANES), BF16)] + [pltpu.VMEM((NS, nr, LANES), F32)] * 5),
        out_shape=[jax.ShapeDtypeStruct((n_all, t_new, ATTN_W), F32),
                   jax.ShapeDtypeStruct((n_all, KV_W, WINDOW), F32)],
        compiler_params=_params("arbitrary", "arbitrary"),
        name="nsa_sample",
    )(page_table, *([cache_sel] * (NS * P)), q, qr, gates, ckv, sel_new, cache_win, win_new,
      _overlap_t(nch, LANES).T)


def _pool_windows(ext_ref, n_rows, pos, w_ref, lead):
    outs = []
    for gi, w in enumerate(POOL_WINDOWS):
        lanes = slice(gi * POOL_CH, (gi + 1) * POOL_CH)
        tot = None
        for k in range(w):
            v = ext_ref[lead + (slice(HALO - k, HALO - k + n_rows), lanes)]
            tot = v if tot is None else tot + v
        cnt = jnp.minimum(pos + 1, w).astype(F32)
        d = tot / cnt - ext_ref[lead + (slice(HALO, HALO + n_rows), lanes)]
        outs.append((d, w_ref[gi]))
    return outs


def _pool_prompt_kernel(x_ref, h_ref, w_ref, o_ref, ext_ref, *, tm):
    i = pl.program_id(1)
    ext_ref[0:HALO, :] = jnp.where(i == 0, 0.0, h_ref[...])
    ext_ref[HALO:, :] = x_ref[...]
    pos = i * tm + lax.broadcasted_iota(jnp.int32, (tm, 1), 0)
    res = [_dot(d.astype(BF16), w) for d, w in _pool_windows(ext_ref, tm, pos, w_ref, ())]
    o_ref[...] = jnp.concatenate(res, axis=1)


def _pool_prompt(pool_in, w_pool_b, n_seq, tm):
    n = pool_in.shape[0]
    nt = n // n_seq // tm
    hb = tm // HALO
    return pl.pallas_call(
        functools.partial(_pool_prompt_kernel, tm=tm),
        grid=(n_seq, nt),
        in_specs=[pl.BlockSpec((tm, POOL_W), lambda b, i: (b * nt + i, 0)),
                  pl.BlockSpec((HALO, POOL_W), lambda b, i: (jnp.maximum((b * nt + i) * hb - 1, 0), 0)),
                  _const_spec((N_POOL_GROUPS, POOL_CH, POOL_CH))],
        out_specs=pl.BlockSpec((tm, POOL_W), lambda b, i: (b * nt + i, 0)),
        out_shape=jax.ShapeDtypeStruct((n, POOL_W), F32),
        scratch_shapes=[pltpu.VMEM((HALO + tm, POOL_W), F32)],
        compiler_params=_params("parallel", "arbitrary"),
        name="pool_prompt",
    )(pool_in, pool_in, w_pool_b)


def _pool_sample_kernel(x_ref, st_ref, w_ref, o_ref, ns_ref, ext_ref, *, t_new, past):
    nb = x_ref.shape[0]
    ext_ref[:, HALO - POOL_HIST:HALO, :] = st_ref[...]
    ext_ref[:, HALO:, :] = x_ref[...]
    pos = past + lax.broadcasted_iota(jnp.int32, (1, t_new, 1), 1)
    res = [_dot(d.reshape(nb * t_new, POOL_CH).astype(BF16), w)
           for d, w in _pool_windows(ext_ref, t_new, pos, w_ref, (slice(None),))]
    o_ref[...] = jnp.concatenate(res, axis=1)
    ns_ref[...] = ext_ref[:, HALO + t_new - POOL_HIST:HALO + t_new, :]


def _pool_sample(pool_in, state, w_pool_b, past, nb):
    n_seq, t_new, _ = pool_in.shape
    assert t_new % 8 == 0 and n_seq % nb == 0 and past >= POOL_HIST
    return pl.pallas_call(
        functools.partial(_pool_sample_kernel, t_new=t_new, past=past),
        grid=(n_seq // nb,),
        in_specs=[pl.BlockSpec((nb, t_new, POOL_W), lambda i: (i, 0, 0)),
                  pl.BlockSpec((nb, POOL_HIST, POOL_W), lambda i: (i, 0, 0)),
                  _const_spec((N_POOL_GROUPS, POOL_CH, POOL_CH))],
        out_specs=[pl.BlockSpec((nb * t_new, POOL_W), lambda i: (i, 0)),
                   pl.BlockSpec((nb, POOL_HIST, POOL_W), lambda i: (i, 0, 0))],
        out_shape=[jax.ShapeDtypeStruct((n_seq * t_new, POOL_W), F32),
                   jax.ShapeDtypeStruct((n_seq, POOL_HIST, POOL_W), F32)],
        scratch_shapes=[pltpu.VMEM((nb, HALO + t_new, POOL_W), F32)],
        compiler_params=_params("parallel"),
        name="pool_sample",
    )(pool_in, state, w_pool_b)


def _mix_kernel(x_ref, a_ref, m_ref, ga_ref, gp_ref, wo_ref, g2_ref, wq_ref, x1_ref, qm_ref):
    mix = jnp.concatenate([_rms(a_ref[...], ga_ref[...]), _rms(m_ref[...], gp_ref[...])], axis=1)
    x1 = x_ref[...] + _dot(mix.astype(BF16), wo_ref[...])
    x1_ref[...] = x1
    qm_ref[...] = _dot(_rms(x1, g2_ref[...]).astype(BF16), wq_ref[...]).astype(BF16)


def _mix(x, a, m, ga, gp, w_out_b, g2, w_mq_b, tm):
    n = x.shape[0]
    row = lambda w: pl.BlockSpec((tm, w), lambda i: (i, 0))
    return pl.pallas_call(
        _mix_kernel,
        grid=(n // tm,),
        in_specs=[row(D_MODEL), row(ATTN_W), row(POOL_W), _const_spec((1, ATTN_W)), _const_spec((1, POOL_W)),
                  _const_spec((D_MODEL, D_MODEL)), _const_spec((1, D_MODEL)), _const_spec((D_MODEL, MEM_W))],
        out_specs=[row(D_MODEL), row(MEM_W)],
        out_shape=[jax.ShapeDtypeStruct((n, D_MODEL), F32), jax.ShapeDtypeStruct((n, MEM_W), BF16)],
        compiler_params=_params("parallel"),
        name="mix_out",
    )(x, a, m, ga, gp, w_out_b, g2, w_mq_b)


def _cross_kernel(q_ref, kv_ref, o_ref, *, n_mem_blk, rows):
    pairs = [(i, h) for i in range(n_mem_blk) for h in range(MEM_HEADS)]
    head = lambda off, i, h: kv_ref[i, pl.ds(off + h, N_MEM, stride=2 * MEM_HEADS), :].astype(BF16)
    scores = [_dot_nt(q_ref[i * rows:(i + 1) * rows, h * MEM_HEAD_DIM:(h + 1) * MEM_HEAD_DIM], head(0, i, h))
              * (MEM_HEAD_DIM ** -0.5) for i, h in pairs]
    es = [jnp.exp(s - jnp.max(s, axis=-1, keepdims=True)) for s in scores]
    outs = [_dot(e.astype(BF16), head(MEM_HEADS, i, h)) / jnp.sum(e, axis=-1, keepdims=True)
            for e, (i, h) in zip(es, pairs)]
    for i in range(n_mem_blk):
        o_ref[i * rows:(i + 1) * rows, :] = jnp.concatenate(
            outs[i * MEM_HEADS:(i + 1) * MEM_HEADS], axis=1).astype(BF16)


def _cross(qm, mem_rows, rows, n_mem_blk):
    n = qm.shape[0]
    tm = rows * n_mem_blk
    return pl.pallas_call(
        functools.partial(_cross_kernel, n_mem_blk=n_mem_blk, rows=rows),
        grid=(n // tm,),
        in_specs=[pl.BlockSpec((tm, MEM_W), lambda i: (i, 0)),
                  pl.BlockSpec((n_mem_blk,) + mem_rows.shape[1:], lambda i: (i, 0, 0))],
        out_specs=pl.BlockSpec((tm, MEM_W), lambda i: (i, 0)),
        out_shape=jax.ShapeDtypeStruct((n, MEM_W), BF16),
        compiler_params=_params("parallel"),
        name="cross_attn",
    )(qm, mem_rows)


def _cross_shared_kernel(q_ref, k_ref, vt_ref, o_ref):
    lanes = [slice(h * MEM_HEAD_DIM, (h + 1) * MEM_HEAD_DIM) for h in range(MEM_HEADS)]
    scores = [_dot_nt(k_ref[:, l], q_ref[:, l]) * (MEM_HEAD_DIM ** -0.5) for l in lanes]
    es = [jnp.exp(s - jnp.max(s, axis=0, keepdims=True)).astype(BF16) for s in scores]
    accs = [_dot(vt_ref[0, h], e) for h, e in enumerate(es)]
    outs = [(a[0:MEM_HEAD_DIM, :] / a[MEM_HEAD_DIM:MEM_HEAD_DIM + 1, :]).T for a in accs]
    o_ref[...] = jnp.concatenate(outs, axis=1).astype(BF16)


def _cross_shared(qm, mem_k, mem_vt, tm):
    n = qm.shape[0]
    tiles_per_mem = n // mem_vt.shape[0] // tm
    return pl.pallas_call(
        _cross_shared_kernel,
        grid=(n // tm,),
        in_specs=[pl.BlockSpec((tm, MEM_W), lambda i: (i, 0)),
                  pl.BlockSpec((N_MEM, MEM_W), lambda i: (i // tiles_per_mem, 0)),
                  pl.BlockSpec((1, MEM_HEADS, VT_ROWS, N_MEM), lambda i: (i // tiles_per_mem, 0, 0, 0))],
        out_specs=pl.BlockSpec((tm, MEM_W), lambda i: (i, 0)),
        out_shape=jax.ShapeDtypeStruct((n, MEM_W), BF16),
        compiler_params=_params("parallel"),
        name="cross_attn_shared",
    )(qm, mem_k, mem_vt)


def _ffn_kernel(x1_ref, o_ref, wmo_ref, g3_ref, wup_ref, wdn_ref, gf_ref, y_ref):
    x2 = x1_ref[...] + _dot(o_ref[...], wmo_ref[...])
    xn = _rms(x2, g3_ref[...]).astype(BF16)
    acc = x2
    for c in range(D_FF // FF_CHUNK):
        h = jnp.maximum(_dot(xn, wup_ref[:, c * FF_CHUNK:(c + 1) * FF_CHUNK]), 0.0)
        acc = acc + _dot((h * h).astype(BF16), wdn_ref[c * FF_CHUNK:(c + 1) * FF_CHUNK, :])
    y_ref[...] = _rms(acc, gf_ref[...])


def _ffn(x1, o, w_mo_b, g3, w_up_b, w_down_b, g_final, tm):
    n = x1.shape[0]
    row = lambda w: pl.BlockSpec((tm, w), lambda i: (i, 0))
    return pl.pallas_call(
        _ffn_kernel,
        grid=(n // tm,),
        in_specs=[row(D_MODEL), row(MEM_W), _const_spec((MEM_W, D_MODEL)), _const_spec((1, D_MODEL)),
                  _const_spec((D_MODEL, D_FF)), _const_spec((D_FF, D_MODEL)), _const_spec((1, D_MODEL))],
        out_specs=row(D_MODEL),
        out_shape=jax.ShapeDtypeStruct((n, D_MODEL), F32),
        compiler_params=_params("parallel"),
        name="ffn",
    )(x1, o, w_mo_b, g3, w_up_b, w_down_b, g_final)


def _tile(n, pref):
    t = min(n, pref)
    assert n % t == 0
    return t


def kernel(x_prompt, x_sample, cache_cmp_kv, cache_sel_kv, cache_win_kv, state_pool, cache_mem_kv, page_table, mem_prompt, w_in, pe_cmp, w_cmp1, w_cmp2, w_pool, g_attn_out, pool_scale, w_out, g_norm1, g_norm2, g_mem, w_mq, w_mkv, w_mo, g_norm3, w_up, w_down, g_final):
    assert w_in.shape[0] == 1, "single-layer trunk"
    B, S, _ = x_prompt.shape
    DB, T, _ = x_sample.shape
    past = page_table.shape[1] * PAGE_SIZE
    kv5 = (2, N_KV_HEADS, HEAD_DIM)
    r2 = lambda v: v.reshape(1, -1)

    w_in_p = _permute_w_in(w_in[0])
    w_out_b, w_mq_b, w_mkv_b, w_mo_b = (w[0].astype(BF16) for w in (w_out, w_mq, w_mkv, w_mo))
    w_up_b, w_down_b, w_pool_b = w_up[0].astype(BF16), w_down[0].astype(BF16), w_pool[0].astype(BF16)
    g1, g2, g3, gm, ga, gp, gf = (r2(v) for v in (g_norm1[0], g_norm2[0], g_norm3[0], g_mem[0], g_attn_out[0],
                                                  pool_scale[0], g_final))
    cw = _compress_weights(pe_cmp[0], w_cmp1[0], w_cmp2[0])

    np_tok = B * S
    tm_p = _tile(S, 512)
    xp = x_prompt.reshape(np_tok, D_MODEL)
    q, qr, cmp_t, sel_t, win_t, sel_b, win_b, gates, pool_in = _in_proj(
        xp, g1, w_in_p, jnp.arange(S), tm_p, n_seq=B)
    mem_f, mem_k, mem_vt = _mem_kv(mem_prompt.reshape(B * N_MEM, D_MODEL), gm, w_mkv_b)
    ckv = _compress_seq(cmp_t, cw, 64)
    a = _nsa_prompt(q, qr, gates, ckv, sel_b, win_b, B)
    m = _pool_prompt(pool_in, w_pool_b, B, tm_p)
    x1, qm = _mix(xp, a, m, ga, gp, w_out_b, g2, w_mq_b, tm_p)
    o = _cross_shared(qm, mem_k, mem_vt, tm_p)
    y_p = _ffn(x1, o, w_mo_b, g3, w_up_b, w_down_b, gf, tm_p)
    rows_first = lambda t: t.reshape(t.shape[:1] + kv5 + t.shape[-1:]).transpose(0, 4, 1, 2, 3)[None]
    p_state = (rows_first(cmp_t), rows_first(sel_t), rows_first(win_t[:, :, -min(WINDOW, S):]),
               pool_in.reshape(B, S, POOL_W)[None, :, -POOL_HIST:],
               mem_f.reshape(1, B, N_MEM, 2, MEM_HEADS, MEM_HEAD_DIM))

    ns_tok = DB * T
    tm_s = _tile(ns_tok, 512)
    xs = x_sample.reshape(ns_tok, D_MODEL)
    q, qr, cmp_kv, sel_kv, win_kv, _, _, gates, pool_in = _in_proj(
        xs, g1, w_in_p, jnp.tile(past + jnp.arange(T), DB), tm_s)
    r3 = lambda v: v.reshape(DB, T, v.shape[-1])
    ckv = _compress_paged(_pages_t(cache_cmp_kv[0]), page_table, cw, 64)
    a, s_win = _nsa_sample(r3(q), r3(qr), r3(gates), ckv, _pages_t(cache_sel_kv[0]), page_table,
                           r3(sel_kv), _pages_t(cache_win_kv[0]), r3(win_kv), 64, 2)
    m, s_pool = _pool_sample(r3(pool_in), state_pool[0], w_pool_b, past, _tile(DB, 16))
    x1, qm = _mix(xs, a.reshape(ns_tok, ATTN_W), m, ga, gp, w_out_b, g2, w_mq_b, tm_s)
    o = _cross(qm, cache_mem_kv[0].reshape(DB, N_MEM * 2 * MEM_HEADS, MEM_HEAD_DIM), T, _tile(DB, 4))
    y_s = _ffn(x1, o, w_mo_b, g3, w_up_b, w_down_b, gf, tm_s)
    s_state = (cmp_kv.reshape((1, DB, T) + kv5), sel_kv.reshape((1, DB, T) + kv5),
               rows_first(s_win), s_pool[None])

    return (y_p.reshape(B, S, D_MODEL), y_s.reshape(DB, T, D_MODEL)) + p_state + s_state
```
